```python
import math
import jax
import jax.numpy as jnp
from jax import lax
import numpy as np

D_MODEL = 1024
BATCH = 16
SEQ = 2048
DEPTH = 1
DEC_BATCH = 8
DEC_SEQ = 16
PAST_LEN = 2048

CHUNK = 64
LEFT_CHUNKS = 8
ATTN_REACH = LEFT_CHUNKS * CHUNK
D_SSM = D_MODEL // 2
D_ATTN = D_MODEL // 2
HEAD_DIM = 64
N_HEADS = D_ATTN // HEAD_DIM
MAX_REL = 128
N_REL = 2 * MAX_REL + 1
SSM_GROUP = 16
N_GROUPS = D_SSM // SSM_GROUP
SSM_STATE = 64
DT_MIN = 0.001
DT_MAX = 0.1
D_FF = ((8 * D_MODEL // 3 + 255) // 256) * 256
CONV_W = 3
D_IN = D_SSM + 3 * D_ATTN + 2 * D_MODEL
RMS_EPS = 1e-6
MASK_VALUE = -1e30

kernel_name = 'hybrid_s5_chunk_attn_convglu_stream_step'


def rms_norm(x, g):
    xf = x.astype(jnp.float32)
    y = xf * lax.rsqrt(jnp.mean(xf * xf, axis=-1, keepdims=True) + RMS_EPS)
    return (y * g.astype(jnp.float32)).astype(x.dtype)


def _linear_combine(left, right):
    a1, b1 = left
    a2, b2 = right
    return a1 * a2, a2 * b1 + b2


def s5_mixer(u, s0_re, s0_im, lam_re, lam_im, log_dt, b_re, b_im, c_re, c_im, d_skip):
    n, t, _ = u.shape
    f32 = jnp.float32
    uf = u.astype(f32).reshape(n, t, N_GROUPS, SSM_GROUP)
    lam = lax.complex(lam_re.astype(f32), lam_im.astype(f32))
    dt = jnp.exp(log_dt.astype(f32))[:, None]
    lam_bar = jnp.exp(lam * dt)
    b_mat = lax.complex(b_re.astype(f32), b_im.astype(f32))
    b_bar = ((lam_bar - 1.0) / lam)[..., None] * b_mat
    bu = jnp.einsum('ntgh,gph->ntgp', uf, b_bar)
    s0 = lax.complex(s0_re.astype(f32), s0_im.astype(f32))
    bu = bu.at[:, 0].add(lam_bar * s0)
    a = jnp.broadcast_to(lam_bar, (1, t) + lam_bar.shape)
    _, s = lax.associative_scan(_linear_combine, (a, bu), axis=1)
    c_mat = lax.complex(c_re.astype(f32), c_im.astype(f32))
    y = jnp.einsum('ntgp,ghp->ntgh', s, c_mat).real + d_skip.astype(f32) * uf
    s_last = s[:, -1]
    return y.reshape(n, t, D_SSM).astype(u.dtype), jnp.real(s_last), jnp.imag(s_last)


def rel_bias(table, q_off, k_off):
    rel = jnp.clip(q_off[:, None] - k_off[None, :], -MAX_REL, MAX_REL) + MAX_REL
    return table[:, rel].astype(jnp.float32)


def chunk_band_attention_prompt(q, k, v, table):
    n, s, h, dh = q.shape
    nc = s // CHUNK
    pad = LEFT_CHUNKS * CHUNK
    band = pad + CHUNK
    kp = jnp.pad(k, ((0, 0), (pad, 0), (0, 0), (0, 0)))
    vp = jnp.pad(v, ((0, 0), (pad, 0), (0, 0), (0, 0)))
    w = jnp.arange(band)
    bias = rel_bias(table, jnp.arange(CHUNK), w - pad)
    scale = HEAD_DIM ** -0.5

    def one_chunk(c):
        start = c * CHUNK
        qc = lax.dynamic_slice_in_dim(q, start, CHUNK, axis=1)
        kb = lax.dynamic_slice_in_dim(kp, start, band, axis=1)
        vb = lax.dynamic_slice_in_dim(vp, start, band, axis=1)
        sc = jnp.einsum('nihd,nkhd->nhik', qc, kb).astype(jnp.float32) * scale + bias
        valid = (start - pad + w) >= 0
        sc = jnp.where(valid, sc, MASK_VALUE)
        p = jax.nn.softmax(sc, axis=-1).astype(v.dtype)
        return jnp.einsum('nhik,nkhd->nihd', p, vb)

    out = lax.map(one_chunk, jnp.arange(nc))
    return out.transpose(1, 0, 2, 3, 4).reshape(n, s, h * dh)


def chunk_attention_sample(q, k, v, cache_k, cache_v, table):
    n, t, h, dh = q.shape
    w = cache_k.shape[1]
    kk = jnp.concatenate([cache_k.astype(k.dtype), k], axis=1)
    vv = jnp.concatenate([cache_v.astype(v.dtype), v], axis=1)
    k_off = jnp.concatenate([jnp.arange(w) - w, jnp.arange(t)])
    bias = rel_bias(table, jnp.arange(t), k_off)
    sc = jnp.einsum('nihd,nkhd->nhik', q, kk).astype(jnp.float32) * (HEAD_DIM ** -0.5) + bias
    p = jax.nn.softmax(sc, axis=-1).astype(v.dtype)
    return jnp.einsum('nhik,nkhd->nihd', p, vv).reshape(n, t, h * dh)


def causal_depthwise_conv(a, hist, taps, bias):
    t = a.shape[1]
    hp = jnp.concatenate([hist.astype(a.dtype), a], axis=1)
    y = bias
    for j in range(CONV_W):
        y = y + taps[j] * hp[:, j:j + t]
    return y, hp[:, hp.shape[1] - (CONV_W - 1):]


def hybrid_layer(x, ssm_re0, ssm_im0, past_k, past_v, conv_hist, p):
    n, t, _ = x.shape
    xn = rms_norm(x, p['g_mix'])
    z = xn @ p['w_in']
    cuts = [D_SSM, D_SSM + D_ATTN, D_SSM + 2 * D_ATTN, D_SSM + 3 * D_ATTN,
            D_SSM + 3 * D_ATTN + D_MODEL]
    u, q, k, v, g_ssm, g_att = jnp.split(z, cuts, axis=-1)
    y_ssm, s_re, s_im = s5_mixer(u, ssm_re0, ssm_im0, p['lam_re'], p['lam_im'], p['log_dt'],
                                 p['b_re'], p['b_im'], p['c_re'], p['c_im'], p['d'])
    gl = jax.nn.gelu(y_ssm) @ p['w_ssm_glu']
    br_ssm = gl[..., :D_MODEL] * jax.nn.sigmoid(gl[..., D_MODEL:])
    q = q.reshape(n, t, N_HEADS, HEAD_DIM)
    k = k.reshape(n, t, N_HEADS, HEAD_DIM)
    v = v.reshape(n, t, N_HEADS, HEAD_DIM)
    if past_k is None:
        att = chunk_band_attention_prompt(q, k, v, p['rel_bias'])
        keep = min(ATTN_REACH, t)
        new_k, new_v = k[:, t - keep:], v[:, t - keep:]
    else:
        att = chunk_attention_sample(q, k, v, past_k, past_v, p['rel_bias'])
        new_k, new_v = k, v
    br_att = att @ p['w_attn_up']
    mix = jax.nn.sigmoid(g_ssm) * br_ssm + jax.nn.sigmoid(g_att) * br_att
    h = x + mix @ p['w_o']
    hn = rms_norm(h, p['g_ffn'])
    up = hn @ p['w_up']
    a, b = jnp.split(up, [D_FF], axis=-1)
    c, new_conv = causal_depthwise_conv(a, conv_hist, p['conv_w'], p['conv_b'])
    out = h + (jax.nn.gelu(c) * b) @ p['w_down']
    return out, s_re, s_im, new_k, new_v, new_conv


def setup_inputs(seed: int = 0) -> dict:
    key = jax.random.key(seed)
    ks = jax.random.split(key, 32)
    f32 = jnp.float32

    def nrm(k, shape, s):
        return s * jax.random.normal(k, shape, f32)

    w_att = min(ATTN_REACH, PAST_LEN)
    n_idx = jnp.arange(SSM_STATE, dtype=f32)
    gp = (DEPTH, N_GROUPS, SSM_STATE)
    return {
        'x_prompt': nrm(ks[0], (BATCH, SEQ, D_MODEL), 1.0),
        'x_sample': nrm(ks[1], (DEC_BATCH, DEC_SEQ, D_MODEL), 1.0),
        'state_ssm_re': nrm(ks[2], (DEPTH, DEC_BATCH, N_GROUPS, SSM_STATE), 0.1),
        'state_ssm_im': nrm(ks[3], (DEPTH, DEC_BATCH, N_GROUPS, SSM_STATE), 0.1),
        'cache_attn_k': nrm(ks[4], (DEPTH, DEC_BATCH, w_att, N_HEADS, HEAD_DIM), 1.0),
        'cache_attn_v': nrm(ks[5], (DEPTH, DEC_BATCH, w_att, N_HEADS, HEAD_DIM), 1.0),
        'cache_conv': nrm(ks[6], (DEPTH, DEC_BATCH, CONV_W - 1, D_FF), 1.0),
        'g_mix': 1.0 + nrm(ks[7], (DEPTH, D_MODEL), 0.02),
        'w_in': nrm(ks[8], (DEPTH, D_MODEL, D_IN), D_MODEL ** -0.5),
        'ssm_lambda_re': -0.5 + nrm(ks[9], gp, 0.01),
        'ssm_lambda_im': math.pi * n_idx + nrm(ks[10], gp, 0.01),
        'ssm_log_dt': jax.random.uniform(ks[11], (DEPTH, N_GROUPS), dtype=f32,
                                         minval=math.log(DT_MIN), maxval=math.log(DT_MAX)),
        'ssm_b_re': nrm(ks[12], (DEPTH, N_GROUPS, SSM_STATE, SSM_GROUP), (2 * SSM_GROUP) ** -0.5),
        'ssm_b_im': nrm(ks[13], (DEPTH, N_GROUPS, SSM_STATE, SSM_GROUP), (2 * SSM_GROUP) ** -0.5),
        'ssm_c_re': nrm(ks[14], (DEPTH, N_GROUPS, SSM_GROUP, SSM_STATE), SSM_STATE ** -0.5),
        'ssm_c_im': nrm(ks[15], (DEPTH, N_GROUPS, SSM_GROUP, SSM_STATE), SSM_STATE ** -0.5),
        'ssm_d': nrm(ks[16], (DEPTH, N_GROUPS, SSM_GROUP), 1.0),
        'w_ssm_glu': nrm(ks[17], (DEPTH, D_SSM, 2 * D_MODEL), D_SSM ** -0.5),
        'attn_rel_bias': nrm(ks[18], (DEPTH, N_HEADS, N_REL), 0.1),
        'w_attn_up': nrm(ks[19], (DEPTH, D_ATTN, D_MODEL), D_ATTN ** -0.5),
        'w_o': nrm(ks[20], (DEPTH, D_MODEL, D_MODEL), D_MODEL ** -0.5),
        'g_ffn': 1.0 + nrm(ks[21], (DEPTH, D_MODEL), 0.02),
        'w_up': nrm(ks[22], (DEPTH, D_MODEL, 2 * D_FF), D_MODEL ** -0.5),
        'conv_w': nrm(ks[23], (DEPTH, CONV_W, D_FF), CONV_W ** -0.5),
        'conv_b': nrm(ks[24], (DEPTH, D_FF), 0.02),
        'w_down': nrm(ks[25], (DEPTH, D_FF, D_MODEL), D_FF ** -0.5),
        'g_final': 1.0 + nrm(ks[26], (D_MODEL,), 0.02),
    }


def reference(x_prompt, x_sample, state_ssm_re, state_ssm_im, cache_attn_k, cache_attn_v, cache_conv,
              g_mix, w_in, ssm_lambda_re, ssm_lambda_im, ssm_log_dt, ssm_b_re, ssm_b_im,
              ssm_c_re, ssm_c_im, ssm_d, w_ssm_glu, attn_rel_bias, w_attn_up, w_o,
              g_ffn, w_up, conv_w, conv_b, w_down, g_final):
    hp, hs = x_prompt, x_sample
    nb = x_prompt.shape[0]
    p_re, p_im, p_k, p_v, p_c = [], [], [], [], []
    s_re, s_im, s_k, s_v, s_c = [], [], [], [], []
    for l in range(DEPTH):
        prm = {
            'g_mix': g_mix[l], 'w_in': w_in[l],
            'lam_re': ssm_lambda_re[l], 'lam_im': ssm_lambda_im[l], 'log_dt': ssm_log_dt[l],
            'b_re': ssm_b_re[l], 'b_im': ssm_b_im[l], 'c_re': ssm_c_re[l], 'c_im': ssm_c_im[l],
            'd': ssm_d[l], 'w_ssm_glu': w_ssm_glu[l], 'rel_bias': attn_rel_bias[l],
            'w_attn_up': w_attn_up[l], 'w_o': w_o[l], 'g_ffn': g_ffn[l], 'w_up': w_up[l],
            'conv_w': conv_w[l], 'conv_b': conv_b[l], 'w_down': w_down[l],
        }
        zero_state = jnp.zeros((nb, N_GROUPS, SSM_STATE), jnp.float32)
        zero_conv = jnp.zeros((nb, CONV_W - 1, D_FF), x_prompt.dtype)
        hp, a_re, a_im, a_k, a_v, a_c = hybrid_layer(hp, zero_state, zero_state, None, None, zero_conv, prm)
        hs, b_re, b_im, b_k, b_v, b_c = hybrid_layer(hs, state_ssm_re[l], state_ssm_im[l],
                                                     cache_attn_k[l], cache_attn_v[l], cache_conv[l], prm)
        p_re.append(a_re); p_im.append(a_im); p_k.append(a_k); p_v.append(a_v); p_c.append(a_c)
        s_re.append(b_re); s_im.append(b_im); s_k.append(b_k); s_v.append(b_v); s_c.append(b_c)
    y_prompt = rms_norm(hp, g_final)
    y_sample = rms_norm(hs, g_final)
    new_ssm_re_prompt = jnp.stack(p_re)
    new_ssm_im_prompt = jnp.stack(p_im)
    new_k_prompt = jnp.stack(p_k)
    new_v_prompt = jnp.stack(p_v)
    new_conv_prompt = jnp.stack(p_c)
    new_ssm_re_sample = jnp.stack(s_re)
    new_ssm_im_sample = jnp.stack(s_im)
    new_k_sample = jnp.stack(s_k)
    new_v_sample = jnp.stack(s_v)
    new_conv_sample = jnp.stack(s_c)
    return (y_prompt, y_sample, new_ssm_re_prompt, new_ssm_im_prompt, new_k_prompt, new_v_prompt,
            new_conv_prompt, new_ssm_re_sample, new_ssm_im_sample, new_k_sample, new_v_sample,
            new_conv_sample)
```

```python
import functools
import math

import jax
import jax.numpy as jnp
from jax import lax
from jax.experimental import pallas as pl
from jax.experimental.pallas import tpu as pltpu

F32 = jnp.float32
BF16 = jnp.bfloat16

D_MODEL = 1024
CHUNK = 64
LEFT_CHUNKS = 8
ATTN_REACH = LEFT_CHUNKS * CHUNK
BAND = ATTN_REACH + CHUNK
D_SSM = D_MODEL // 2
D_ATTN = D_MODEL // 2
HEAD_DIM = 64
N_HEADS = D_ATTN // HEAD_DIM
MAX_REL = 128
SSM_GROUP = 16
N_GROUPS = D_SSM // SSM_GROUP
SSM_STATE = 64
D_FF = 2816
CONV_W = 3
D_IN = D_SSM + 3 * D_ATTN + 2 * D_MODEL
RMS_EPS = 1e-6
MASK_VALUE = -1e30

V7X_VMEM_LIMIT_BYTES = 56 * 1024 * 1024
LANES = 128
FF_CHUNKS = ((0, 768), (768, 1536), (1536, 2304), (2304, 2816))


def _params(*sem):
    return pltpu.CompilerParams(dimension_semantics=sem, vmem_limit_bytes=V7X_VMEM_LIMIT_BYTES)


def _const_spec(shape):
    zeros = (0,) * len(shape)
    return pl.BlockSpec(shape, lambda *_: zeros, pipeline_mode=pl.Buffered(1))


def _rms(x, g):
    return x * lax.rsqrt(jnp.mean(x * x, axis=-1, keepdims=True) + RMS_EPS) * g


def _gelu(x):
    return 0.5 * x * (1.0 + jnp.tanh(math.sqrt(2.0 / math.pi) * (x + 0.044715 * (x * x * x))))


def _sigmoid(x):
    return 1.0 / (1.0 + jnp.exp(-x))


def _dot(a, b):
    return jnp.dot(a, b, preferred_element_type=F32)


def _inproj_body(x_ref, g_ref, w_ref, u_ref, qkv_ref, gate_ref, k32_ref, v32_ref):
    xn = _rms(x_ref[0], g_ref[...]).astype(BF16)

    def seg(lo, hi):
        return _dot(xn, w_ref[:, lo:hi])

    u_ref[0] = seg(0, D_SSM).astype(BF16)
    q0 = D_SSM
    qkv_ref[0, :, 0:D_ATTN] = seg(q0, q0 + D_ATTN).astype(BF16)
    k = seg(q0 + D_ATTN, q0 + 2 * D_ATTN)
    v = seg(q0 + 2 * D_ATTN, q0 + 3 * D_ATTN)
    qkv_ref[0, :, D_ATTN:2 * D_ATTN] = k.astype(BF16)
    qkv_ref[0, :, 2 * D_ATTN:3 * D_ATTN] = v.astype(BF16)
    g0 = q0 + 3 * D_ATTN
    gate_ref[0, :, 0:D_MODEL] = seg(g0, g0 + D_MODEL).astype(BF16)
    gate_ref[0, :, D_MODEL:2 * D_MODEL] = seg(g0 + D_MODEL, g0 + 2 * D_MODEL).astype(BF16)

    @pl.when(pl.program_id(1) == pl.num_programs(1) - 1)
    def _():
        k32_ref[0] = k
        v32_ref[0] = v


def _inproj(x, g_mix, w_in_b, tm):
    n, t, _ = x.shape
    tok = lambda w: pl.BlockSpec((1, tm, w), lambda i, j: (i, j, 0))
    keep = pl.BlockSpec((1, tm, D_ATTN), lambda i, j: (i, 0, 0))
    return pl.pallas_call(
        _inproj_body,
        grid=(n, t // tm),
        in_specs=[tok(D_MODEL), _const_spec((1, D_MODEL)), _const_spec((D_MODEL, D_IN))],
        out_specs=[tok(D_SSM), tok(3 * D_ATTN), tok(2 * D_MODEL), keep, keep],
        out_shape=[
            jax.ShapeDtypeStruct((n, t, D_SSM), BF16),
            jax.ShapeDtypeStruct((n, t, 3 * D_ATTN), BF16),
            jax.ShapeDtypeStruct((n, t, 2 * D_MODEL), BF16),
            jax.ShapeDtypeStruct((n, tm, D_ATTN), F32),
            jax.ShapeDtypeStruct((n, tm, D_ATTN), F32),
        ],
        compiler_params=_params("arbitrary", "arbitrary"),
        name="inproj",
    )(x, g_mix, w_in_b)


def _ssm_weights(lam_re, lam_im, log_dt, b_re, b_im, c_re, c_im, d_skip, L):
    hi = lax.Precision.HIGHEST
    lam = lax.complex(lam_re.astype(F32), lam_im.astype(F32))
    dt = jnp.exp(log_dt.astype(F32))[:, None]
    lam_bar = jnp.exp(lam * dt)
    b_bar = ((lam_bar - 1.0) / lam)[..., None] * lax.complex(b_re.astype(F32), b_im.astype(F32))
    c_mat = lax.complex(c_re.astype(F32), c_im.astype(F32))
    tau = jnp.arange(L + 1, dtype=F32)
    pw = jnp.exp((lam * dt)[:, None, :] * tau[None, :, None])
    kt = jnp.einsum('gap,gtp,gph->ghta', c_mat, pw[:, :L], b_bar, precision=hi).real
    eye = jnp.eye(SSM_GROUP, dtype=F32)
    kt = kt.at[:, :, 0, :].add(d_skip.astype(F32)[:, :, None] * eye[None])
    kflat = kt.reshape(N_GROUPS, SSM_GROUP, L * SSM_GROUP)
    pw_rev = pw[:, L - 1 - jnp.arange(L)]
    wst = pw_rev[:, :, None, :] * jnp.swapaxes(b_bar, 1, 2)[:, None, :, :]
    wst = wst.reshape(N_GROUPS, L * SSM_GROUP, SSM_STATE)
    cl = c_mat[:, :, None, :] * pw[:, None, 1:L + 1, :]
    cl = jnp.transpose(cl, (0, 3, 2, 1)).reshape(N_GROUPS, SSM_STATE, L * SSM_GROUP)
    lam_l = pw[:, L]
    return dict(
        kflat=kflat,
        wst_re=wst.real.astype(BF16), wst_im=wst.imag.astype(BF16),
        wint_re=cl.real.astype(BF16), wint_im=(-cl.imag).astype(BF16),
        lam_re=lam_l.real[:, None, :], lam_im=lam_l.imag[:, None, :],
    )


def _ssm_body(u_ref, kflat_ref, wst_re_ref, wst_im_ref, wint_re_ref, wint_im_ref, lam_re_ref, lam_im_ref,
              s0_re_ref, s0_im_ref, y_ref, sre_ref, sim_ref, toep, prev_re, prev_im, *, L, nb, nc):
    lh = L * SSM_GROUP
    kf = kflat_ref[0]
    lane = lax.broadcasted_iota(jnp.int32, kf.shape, 1)
    for j in range(L):
        if j == 0:
            blk = kf
        else:
            blk = jnp.where(lane >= j * SSM_GROUP, pltpu.roll(kf, j * SSM_GROUP, axis=1), 0.0)
        toep[j * SSM_GROUP:(j + 1) * SSM_GROUP, :] = blk.astype(BF16)

    u = u_ref[0]
    x_re = _dot(u, wst_re_ref[0])
    x_im = _dot(u, wst_im_ref[0])
    l_re = lam_re_ref[0]
    l_im = lam_im_ref[0]
    s_re = s0_re_ref[0]
    s_im = s0_im_ref[0]
    for c in range(nc):
        rows = slice(c * nb, (c + 1) * nb)
        prev_re[rows, :] = s_re
        prev_im[rows, :] = s_im
        n_re = l_re * s_re - l_im * s_im + x_re[rows, :]
        n_im = l_re * s_im + l_im * s_re + x_im[rows, :]
        s_re, s_im = n_re, n_im
    sre_ref[0] = s_re
    sim_ref[0] = s_im
    y = _dot(u, toep[...])
    y = y + _dot(prev_re[...].astype(BF16), wint_re_ref[0]) + _dot(prev_im[...].astype(BF16), wint_im_ref[0])
    y_ref[0] = _gelu(y).astype(BF16)


def _ssm(u2, w, s0_re, s0_im, L, nb, nc):
    lh = L * SSM_GROUP
    r = nc * nb
    grp = lambda a, b: pl.BlockSpec((1, a, b), lambda g: (g, 0, 0))
    return pl.pallas_call(
        functools.partial(_ssm_body, L=L, nb=nb, nc=nc),
        grid=(N_GROUPS,),
        in_specs=[grp(r, lh), grp(SSM_GROUP, lh), grp(lh, SSM_STATE), grp(lh, SSM_STATE),
                  grp(SSM_STATE, lh), grp(SSM_STATE, lh), grp(1, SSM_STATE), grp(1, SSM_STATE),
                  grp(nb, SSM_STATE), grp(nb, SSM_STATE)],
        out_specs=[grp(r, lh), grp(nb, SSM_STATE), grp(nb, SSM_STATE)],
        out_shape=[
            jax.ShapeDtypeStruct((N_GROUPS, r, lh), BF16),
            jax.ShapeDtypeStruct((N_GROUPS, nb, SSM_STATE), F32),
            jax.ShapeDtypeStruct((N_GROUPS, nb, SSM_STATE), F32),
        ],
        scratch_shapes=[pltpu.VMEM((lh, lh), BF16), pltpu.VMEM((r, SSM_STATE), F32),
                        pltpu.VMEM((r, SSM_STATE), F32)],
        compiler_params=_params("arbitrary"),
        name="s5",
    )(u2, w['kflat'], w['wst_re'], w['wst_im'], w['wint_re'], w['wint_im'], w['lam_re'], w['lam_im'], s0_re, s0_im)


def _s5_branch(u, w, s0_re, s0_im, L, nb):
    n, t, _ = u.shape
    nc = t // L
    pad = nb - n
    u5 = u.reshape(n, nc, L, N_GROUPS, SSM_GROUP)
    u5 = jnp.pad(u5, ((0, pad), (0, 0), (0, 0), (0, 0), (0, 0)))
    u2 = jnp.transpose(u5, (3, 1, 0, 2, 4)).reshape(N_GROUPS, nc * nb, L * SSM_GROUP)
    s0 = lambda s: jnp.pad(jnp.transpose(s.astype(F32), (1, 0, 2)), ((0, 0), (0, pad), (0, 0)))
    y2, s_re, s_im = _ssm(u2, w, s0(s0_re), s0(s0_im), L, nb, nc)
    y5 = y2.reshape(N_GROUPS, nc, nb, L, SSM_GROUP)
    y = jnp.transpose(y5, (2, 1, 3, 0, 4))[:n].reshape(n, t, D_SSM)
    fin = lambda s: jnp.transpose(s, (1, 0, 2))[:n]
    return y, fin(s_re), fin(s_im)


def _attend(qc, kb, vb, bias_ref, valid):
    tq = qc.shape[0]
    left = lax.broadcasted_iota(jnp.int32, (tq, LANES), 1) < HEAD_DIM
    zero = jnp.zeros((), BF16)
    outs = []
    for hp in range(N_HEADS // 2):
        cols = slice(hp * LANES, (hp + 1) * LANES)
        qp, kp, vp = qc[:, cols], kb[:, cols], vb[:, cols]
        pair = []
        for s in range(2):
            qm = jnp.where(left, qp, zero) if s == 0 else jnp.where(left, zero, qp)
            sc = lax.dot_general(qm, kp, (((1,), (1,)), ((), ())), preferred_element_type=F32)
            sc = sc + bias_ref[2 * hp + s]
            if valid is not None:
                sc = jnp.where(valid, sc, MASK_VALUE)
            m = jnp.max(sc, axis=-1, keepdims=True)
            e = jnp.exp(sc - m)
            den = jnp.sum(e, axis=-1, keepdims=True)
            pair.append(_dot(e.astype(BF16), vp) / den)
        outs.append(jnp.where(left, pair[0], pair[1]))
    return jnp.concatenate(outs, axis=-1)


def _attn_prompt_body(q_ref, k_ref, v_ref, bias_ref, o_ref, kpad, vpad):
    t = q_ref.shape[1]
    zeros = jnp.zeros((ATTN_REACH, D_ATTN), BF16)
    kpad[0:ATTN_REACH, :] = zeros
    vpad[0:ATTN_REACH, :] = zeros
    kpad[ATTN_REACH:ATTN_REACH + t, :] = k_ref[0]
    vpad[ATTN_REACH:ATTN_REACH + t, :] = v_ref[0]
    col = lax.broadcasted_iota(jnp.int32, (CHUNK, BAND), 1)

    def chunk(c, carry):
        r0 = pl.multiple_of(c * CHUNK, CHUNK)
        qc = q_ref[0, pl.ds(r0, CHUNK), :]
        kb = kpad[pl.ds(r0, BAND), :]
        vb = vpad[pl.ds(r0, BAND), :]
        valid = col >= ATTN_REACH - r0
        o_ref[0, pl.ds(r0, CHUNK), :] = _attend(qc, kb, vb, bias_ref, valid).astype(BF16)
        return carry

    lax.fori_loop(0, t // CHUNK, chunk, 0)


def _attn_prompt(qkv, bias):
    n, t, _ = qkv.shape
    part = lambda j: pl.BlockSpec((1, t, D_ATTN), lambda i: (i, 0, j))
    return pl.pallas_call(
        _attn_prompt_body,
        grid=(n,),
        in_specs=[part(0), part(1), part(2), _const_spec((N_HEADS, CHUNK, BAND))],
        out_specs=pl.BlockSpec((1, t, D_ATTN), lambda i: (i, 0, 0)),
        out_shape=jax.ShapeDtypeStruct((n, t, D_ATTN), BF16),
        scratch_shapes=[pltpu.VMEM((ATTN_REACH + t, D_ATTN), BF16), pltpu.VMEM((ATTN_REACH + t, D_ATTN), BF16)],
        compiler_params=_params("arbitrary"),
        name="attn_prompt",
    )(qkv, qkv, qkv, bias)


def _attn_sample_body(q_ref, k_ref, v_ref, bias_ref, o_ref):
    o_ref[0] = _attend(q_ref[0], k_ref[0], v_ref[0], bias_ref, None).astype(BF16)


def _attn_sample(q, kk, vv, bias):
    n, tq, _ = q.shape
    tk = kk.shape[1]
    return pl.pallas_call(
        _attn_sample_body,
        grid=(n,),
        in_specs=[pl.BlockSpec((1, tq, D_ATTN), lambda i: (i, 0, 0)),
                  pl.BlockSpec((1, tk, D_ATTN), lambda i: (i, 0, 0)),
                  pl.BlockSpec((1, tk, D_ATTN), lambda i: (i, 0, 0)),
                  _const_spec((N_HEADS, tq, tk))],
        out_specs=pl.BlockSpec((1, tq, D_ATTN), lambda i: (i, 0, 0)),
        out_shape=jax.ShapeDtypeStruct((n, tq, D_ATTN), BF16),
        compiler_params=_params("arbitrary"),
        name="attn_sample",
    )(q, kk, vv, bias)


def _rel_bias(table, q_off, k_off):
    rel = jnp.clip(q_off[:, None] - k_off[None, :], -MAX_REL, MAX_REL) + MAX_REL
    return table[:, rel].astype(F32)


def _ffn_body(*refs, multi_seq):
    if multi_seq:
        (x_ref, ys_ref, att_ref, gate_ref, h1_ref, h2_ref, wglu, watt, wo, gffn, wup, cw, cb, wdown, gfin,
         y_ref, conv_ref, a_scr) = refs
    else:
        (x_ref, ys_ref, att_ref, gate_ref, hist_ref, wglu, watt, wo, gffn, wup, cw, cb, wdown, gfin,
         y_ref, conv_ref, a_scr, carry) = refs
    tm = x_ref.shape[1]
    gl = _dot(ys_ref[0], wglu[...])
    br_ssm = gl[:, :D_MODEL] * _sigmoid(gl[:, D_MODEL:])
    br_att = _dot(att_ref[0], watt[...])
    mix = (_sigmoid(gate_ref[0, :, :D_MODEL].astype(F32)) * br_ssm
           + _sigmoid(gate_ref[0, :, D_MODEL:].astype(F32)) * br_att)
    h = x_ref[0] + _dot(mix.astype(BF16), wo[...])
    hn = _rms(h, gffn[...]).astype(BF16)

    if multi_seq:
        pos = lax.broadcasted_iota(jnp.int32, (tm, 1), 0) % multi_seq
    else:
        @pl.when(pl.program_id(1) == 0)
        def _():
            carry[...] = hist_ref[0]

    acc = jnp.zeros((tm, D_MODEL), F32)
    for lo, hi in FF_CHUNKS:
        a = _dot(hn, wup[:, lo:hi])
        b = _dot(hn, wup[:, D_FF + lo:D_FF + hi])
        a_scr[8:8 + tm, 0:hi - lo] = a
        if multi_seq:
            a_scr[6:8, 0:hi - lo] = jnp.zeros((2, hi - lo), F32)
            conv_ref[0, :, lo:hi] = a
        else:
            a_scr[6:8, 0:hi - lo] = carry[:, lo:hi]
            carry[:, lo:hi] = a[tm - 2:tm, :]
        a1 = a_scr[7:7 + tm, 0:hi - lo]
        a2 = a_scr[6:6 + tm, 0:hi - lo]
        if multi_seq:
            a1 = jnp.where(pos < 1, h1_ref[:, lo:hi], a1)
            a2 = jnp.where(pos < 2, h2_ref[:, lo:hi], a2)
        c = cb[:, lo:hi] + cw[0:1, lo:hi] * a2 + cw[1:2, lo:hi] * a1 + cw[2:3, lo:hi] * a
        act = (_gelu(c) * b).astype(BF16)
        acc = acc + _dot(act, wdown[lo:hi, :])
    y_ref[0] = _rms(h + acc, gfin[...])

    if not multi_seq:
        @pl.when(pl.program_id(1) == pl.num_programs(1) - 1)
        def _():
            conv_ref[0] = carry[...]


def _ffn_weight_specs():
    return [_const_spec((D_SSM, 2 * D_MODEL)), _const_spec((D_ATTN, D_MODEL)), _const_spec((D_MODEL, D_MODEL)),
            _const_spec((1, D_MODEL)), _const_spec((D_MODEL, 2 * D_FF)), _const_spec((CONV_W, D_FF)),
            _const_spec((1, D_FF)), _const_spec((D_FF, D_MODEL)), _const_spec((1, D_MODEL))]


def _ffn_prompt(x, ys, att, gates, hist, wts, tm):
    n, t, _ = x.shape
    tok = lambda w: pl.BlockSpec((1, tm, w), lambda i, j: (i, j, 0))
    seq = pl.BlockSpec((1, CONV_W - 1, D_FF), lambda i, j: (i, 0, 0))
    fc = max(hi - lo for lo, hi in FF_CHUNKS)
    return pl.pallas_call(
        functools.partial(_ffn_body, multi_seq=0),
        grid=(n, t // tm),
        in_specs=[tok(D_MODEL), tok(D_SSM), tok(D_ATTN), tok(2 * D_MODEL), seq] + _ffn_weight_specs(),
        out_specs=[tok(D_MODEL), seq],
        out_shape=[jax.ShapeDtypeStruct((n, t, D_MODEL), F32),
                   jax.ShapeDtypeStruct((n, CONV_W - 1, D_FF), F32)],
        scratch_shapes=[pltpu.VMEM((tm + 8, fc), F32), pltpu.VMEM((CONV_W - 1, D_FF), F32)],
        compiler_params=_params("arbitrary", "arbitrary"),
        name="merge_ffn_prompt",
    )(x, ys, att, gates, hist, *wts)


def _ffn_sample(x, ys, att, gates, h1, h2, wts, seq_len):
    rows = x.shape[1]
    tok = lambda w: pl.BlockSpec((1, rows, w), lambda i: (0, 0, 0))
    full = pl.BlockSpec((rows, D_FF), lambda i: (0, 0))
    fc = max(hi - lo for lo, hi in FF_CHUNKS)
    return pl.pallas_call(
        functools.partial(_ffn_body, multi_seq=seq_len),
        grid=(1,),
        in_specs=[tok(D_MODEL), tok(D_SSM), tok(D_ATTN), tok(2 * D_MODEL), full, full] + _ffn_weight_specs(),
        out_specs=[tok(D_MODEL), tok(D_FF)],
        out_shape=[jax.ShapeDtypeStruct((1, rows, D_MODEL), F32),
                   jax.ShapeDtypeStruct((1, rows, D_FF), F32)],
        scratch_shapes=[pltpu.VMEM((rows + 8, fc), F32)],
        compiler_params=_params("arbitrary"),
        name="merge_ffn_sample",
    )(x, ys, att, gates, h1, h2, *wts)


PROMPT_TOKEN_TILE = 512
FFN_TOKEN_TILE = 256
SAMPLE_STREAM_PAD = 16


def kernel(x_prompt, x_sample, state_ssm_re, state_ssm_im, cache_attn_k, cache_attn_v, cache_conv,
           g_mix, w_in, ssm_lambda_re, ssm_lambda_im, ssm_log_dt, ssm_b_re, ssm_b_im,
           ssm_c_re, ssm_c_im, ssm_d, w_ssm_glu, attn_rel_bias, w_attn_up, w_o,
           g_ffn, w_up, conv_w, conv_b, w_down, g_final):
    depth = w_in.shape[0]
    nb, seq, _ = x_prompt.shape
    db, dseq, _ = x_sample.shape
    assert depth == 1 and seq % PROMPT_TOKEN_TILE == 0 and min(ATTN_REACH, seq) == PROMPT_TOKEN_TILE
    l = 0

    col_scale = jnp.ones((D_IN,), F32).at[D_SSM:D_SSM + D_ATTN].set(HEAD_DIM ** -0.5)
    w_in_b = (w_in[l] * col_scale[None, :]).astype(BF16)
    wts = (w_ssm_glu[l].astype(BF16), w_attn_up[l].astype(BF16), w_o[l].astype(BF16), g_ffn[l][None, :],
           w_up[l].astype(BF16), conv_w[l], conv_b[l][None, :], w_down[l].astype(BF16), g_final[None, :])
    g_mix_l = g_mix[l][None, :]
    ssm_args = (ssm_lambda_re[l], ssm_lambda_im[l], ssm_log_dt[l], ssm_b_re[l], ssm_b_im[l],
                ssm_c_re[l], ssm_c_im[l], ssm_d[l])
    w_ssm_p = _ssm_weights(*ssm_args, CHUNK)
    w_ssm_s = _ssm_weights(*ssm_args, dseq)
    table = attn_rel_bias[l]
    bias_p = _rel_bias(table, jnp.arange(CHUNK), jnp.arange(BAND) - ATTN_REACH)
    w_cache = cache_attn_k.shape[2]
    bias_s = _rel_bias(table, jnp.arange(dseq), jnp.concatenate([jnp.arange(w_cache) - w_cache, jnp.arange(dseq)]))

    u, qkv, gates, k32, v32 = _inproj(x_prompt, g_mix_l, w_in_b, PROMPT_TOKEN_TILE)
    zero_state = jnp.zeros((nb, N_GROUPS, SSM_STATE), F32)
    ys, p_re, p_im = _s5_branch(u, w_ssm_p, zero_state, zero_state, CHUNK, nb)
    att = _attn_prompt(qkv, bias_p)
    zero_conv = jnp.zeros((nb, CONV_W - 1, D_FF), F32)
    y_prompt, p_conv = _ffn_prompt(x_prompt, ys, att, gates, zero_conv, wts, FFN_TOKEN_TILE)

    rows = db * dseq
    xs = x_sample.reshape(1, rows, D_MODEL)
    us, qkvs, gates_s, ks32, vs32 = _inproj(xs, g_mix_l, w_in_b, rows)
    yss, s_re, s_im = _s5_branch(us.reshape(db, dseq, D_SSM), w_ssm_s, state_ssm_re[l], state_ssm_im[l],
                                 dseq, SAMPLE_STREAM_PAD)
    qkvs3 = qkvs.reshape(db, dseq, 3 * D_ATTN)
    kk = jnp.concatenate([cache_attn_k[l].reshape(db, w_cache, D_ATTN).astype(BF16),
                          qkvs3[:, :, D_ATTN:2 * D_ATTN]], axis=1)
    vv = jnp.concatenate([cache_attn_v[l].reshape(db, w_cache, D_ATTN).astype(BF16),
                          qkvs3[:, :, 2 * D_ATTN:]], axis=1)
    att_s = _attn_sample(qkvs3[:, :, :D_ATTN], kk, vv, bias_s)
    hist = cache_conv[l].astype(F32)
    pad_rows = ((0, 0), (0, dseq - 1), (0, 0))
    h1 = jnp.pad(hist[:, 1:2], pad_rows).reshape(rows, D_FF)
    h2 = jnp.pad(hist, ((0, 0), (0, dseq - 2), (0, 0))).reshape(rows, D_FF)
    y_s, a_s = _ffn_sample(xs, yss.reshape(1, rows, D_SSM), att_s.reshape(1, rows, D_ATTN), gates_s, h1, h2,
                           wts, dseq)
    s_conv = a_s.reshape(db, dseq, D_FF)[:, dseq - (CONV_W - 1):]

    heads = lambda a, n, t: a.reshape(1, n, t, N_HEADS, HEAD_DIM)
    return (y_prompt, y_s.reshape(db, dseq, D_MODEL),
            p_re[None], p_im[None],
            heads(k32, nb, PROMPT_TOKEN_TILE), heads(v32, nb, PROMPT_TOKEN_TILE), p_conv[None],
            s_re[None], s_im[None],
            heads(ks32, db, dseq), heads(vs32, db, dseq), s_conv[None])
```

```python
import functools
import math

import jax
import jax.numpy as jnp
import numpy as np
from jax import lax
from jax.experimental import pallas as pl
from jax.experimental.pallas import tpu as pltpu

F32 = jnp.float32
BF16 = jnp.bfloat16

D_MODEL = 1024
CHUNK = 64
LEFT_CHUNKS = 8
ATTN_REACH = LEFT_CHUNKS * CHUNK
BAND = ATTN_REACH + CHUNK
D_SSM = D_MODEL // 2
D_ATTN = D_MODEL // 2
HEAD_DIM = 64
N_HEADS = D_ATTN // HEAD_DIM
MAX_REL = 128
SSM_GROUP = 16
N_GROUPS = D_SSM // SSM_GROUP
SSM_STATE = 64
D_FF = 2816
CONV_W = 3
D_IN = D_SSM + 3 * D_ATTN + 2 * D_MODEL
RMS_EPS = 1e-6
MASK_VALUE = -1e30

V7X_VMEM_LIMIT_BYTES = 56 * 1024 * 1024
LANES = 128
FF_CHUNKS = ((0, 768), (768, 1536), (1536, 2304), (2304, 2816))


def _params(*sem):
    return pltpu.CompilerParams(dimension_semantics=sem, vmem_limit_bytes=V7X_VMEM_LIMIT_BYTES)


def _const_spec(shape):
    zeros = (0,) * len(shape)
    return pl.BlockSpec(shape, lambda *_: zeros, pipeline_mode=pl.Buffered(1))


def _rms(x, g):
    return x * lax.rsqrt(jnp.mean(x * x, axis=-1, keepdims=True) + RMS_EPS) * g


def _gelu(x):
    return 0.5 * x * (1.0 + jnp.tanh(math.sqrt(2.0 / math.pi) * (x + 0.044715 * (x * x * x))))


def _sigmoid(x):
    return 1.0 / (1.0 + jnp.exp(-x))


def _dot(a, b):
    return jnp.dot(a, b, preferred_element_type=F32)


def _inproj_body(x_ref, g_ref, w_ref, u_ref, qkv_ref, gate_ref, k32_ref, v32_ref):
    xn = _rms(x_ref[0], g_ref[...]).astype(BF16)

    def seg(lo, hi):
        return _dot(xn, w_ref[:, lo:hi])

    u_ref[0] = seg(0, D_SSM).astype(BF16)
    q0 = D_SSM
    qkv_ref[0, :, 0:D_ATTN] = seg(q0, q0 + D_ATTN).astype(BF16)
    k = seg(q0 + D_ATTN, q0 + 2 * D_ATTN)
    v = seg(q0 + 2 * D_ATTN, q0 + 3 * D_ATTN)
    qkv_ref[0, :, D_ATTN:2 * D_ATTN] = k.astype(BF16)
    qkv_ref[0, :, 2 * D_ATTN:3 * D_ATTN] = v.astype(BF16)
    g0 = q0 + 3 * D_ATTN
    gate_ref[0, :, 0:D_MODEL] = seg(g0, g0 + D_MODEL).astype(BF16)
    gate_ref[0, :, D_MODEL:2 * D_MODEL] = seg(g0 + D_MODEL, g0 + 2 * D_MODEL).astype(BF16)

    @pl.when(pl.program_id(1) == pl.num_programs(1) - 1)
    def _():
        k32_ref[0] = k
        v32_ref[0] = v


def _inproj(x, g_mix, w_in_b, tm):
    n, t, _ = x.shape
    tok = lambda w: pl.BlockSpec((1, tm, w), lambda i, j: (i, j, 0))
    keep = pl.BlockSpec((1, tm, D_ATTN), lambda i, j: (i, 0, 0))
    return pl.pallas_call(
        _inproj_body,
        grid=(n, t // tm),
        in_specs=[tok(D_MODEL), _const_spec((1, D_MODEL)), _const_spec((D_MODEL, D_IN))],
        out_specs=[tok(D_SSM), tok(3 * D_ATTN), tok(2 * D_MODEL), keep, keep],
        out_shape=[
            jax.ShapeDtypeStruct((n, t, D_SSM), BF16),
            jax.ShapeDtypeStruct((n, t, 3 * D_ATTN), BF16),
            jax.ShapeDtypeStruct((n, t, 2 * D_MODEL), BF16),
            jax.ShapeDtypeStruct((n, tm, D_ATTN), F32),
            jax.ShapeDtypeStruct((n, tm, D_ATTN), F32),
        ],
        compiler_params=_params("arbitrary", "arbitrary"),
        name="inproj",
    )(x, g_mix, w_in_b)


def _ssm_weights(lam_re, lam_im, log_dt, b_re, b_im, c_re, c_im, d_skip, L):
    hi = lax.Precision.HIGHEST
    f = lambda a: a.astype(F32)
    cmul = lambda ar, ai, br, bi: (ar * br - ai * bi, ar * bi + ai * br)
    a_re, a_im = f(lam_re), f(lam_im)
    dt = jnp.exp(f(log_dt))[:, None]
    tau = jnp.arange(L + 1, dtype=F32)[None, :, None]
    mag = jnp.exp((a_re * dt)[:, None, :] * tau)
    ang = (a_im * dt)[:, None, :] * tau
    pw_re, pw_im = mag * jnp.cos(ang), mag * jnp.sin(ang)
    n_re, n_im = pw_re[:, 1] - 1.0, pw_im[:, 1]
    den = a_re * a_re + a_im * a_im
    f_re, f_im = (n_re * a_re + n_im * a_im) / den, (n_im * a_re - n_re * a_im) / den
    bb_re, bb_im = cmul(f_re[..., None], f_im[..., None], f(b_re), f(b_im))
    cp_re, cp_im = cmul(f(c_re)[:, :, None, :], f(c_im)[:, :, None, :], pw_re[:, None], pw_im[:, None])
    kt = (jnp.einsum('gatp,gph->ghta', cp_re[:, :, :L], bb_re, precision=hi)
          - jnp.einsum('gatp,gph->ghta', cp_im[:, :, :L], bb_im, precision=hi))
    eye = jnp.eye(SSM_GROUP, dtype=F32)
    kt = kt.at[:, :, 0, :].add(f(d_skip)[:, :, None] * eye[None])
    kflat = kt.reshape(N_GROUPS, SSM_GROUP, L * SSM_GROUP)
    rev = L - 1 - np.arange(L)
    bt_re, bt_im = jnp.swapaxes(bb_re, 1, 2)[:, None], jnp.swapaxes(bb_im, 1, 2)[:, None]
    wst_re, wst_im = cmul(pw_re[:, rev][:, :, None, :], pw_im[:, rev][:, :, None, :], bt_re, bt_im)
    flat_in = lambda a: a.reshape(N_GROUPS, L * SSM_GROUP, SSM_STATE).astype(BF16)
    flat_out = lambda a: jnp.transpose(a[:, :, 1:L + 1], (0, 3, 2, 1)).reshape(
        N_GROUPS, SSM_STATE, L * SSM_GROUP).astype(BF16)
    return dict(
        kflat=kflat,
        wst_re=flat_in(wst_re), wst_im=flat_in(wst_im),
        wint_re=flat_out(cp_re), wint_im=flat_out(-cp_im),
        lam_re=pw_re[:, L][:, None, :], lam_im=pw_im[:, L][:, None, :],
    )


def _ssm_body(u_ref, kflat_ref, wst_re_ref, wst_im_ref, wint_re_ref, wint_im_ref, lam_re_ref, lam_im_ref,
              s0_re_ref, s0_im_ref, y_ref, sre_ref, sim_ref, toep, prev_re, prev_im, *, L, nb, nc):
    lh = L * SSM_GROUP
    kf = kflat_ref[0]
    lane = lax.broadcasted_iota(jnp.int32, kf.shape, 1)
    for j in range(L):
        if j == 0:
            blk = kf
        else:
            blk = jnp.where(lane >= j * SSM_GROUP, pltpu.roll(kf, j * SSM_GROUP, axis=1), 0.0)
        toep[j * SSM_GROUP:(j + 1) * SSM_GROUP, :] = blk.astype(BF16)

    u = u_ref[0]
    x_re = _dot(u, wst_re_ref[0])
    x_im = _dot(u, wst_im_ref[0])
    l_re = lam_re_ref[0]
    l_im = lam_im_ref[0]
    s_re = s0_re_ref[0]
    s_im = s0_im_ref[0]
    for c in range(nc):
        rows = slice(c * nb, (c + 1) * nb)
        prev_re[rows, :] = s_re
        prev_im[rows, :] = s_im
        n_re = l_re * s_re - l_im * s_im + x_re[rows, :]
        n_im = l_re * s_im + l_im * s_re + x_im[rows, :]
        s_re, s_im = n_re, n_im
    sre_ref[0] = s_re
    sim_ref[0] = s_im
    y = _dot(u, toep[...])
    y = y + _dot(prev_re[...].astype(BF16), wint_re_ref[0]) + _dot(prev_im[...].astype(BF16), wint_im_ref[0])
    y_ref[0] = _gelu(y).astype(BF16)


def _ssm(u2, w, s0_re, s0_im, L, nb, nc):
    lh = L * SSM_GROUP
    r = nc * nb
    grp = lambda a, b: pl.BlockSpec((1, a, b), lambda g: (g, 0, 0))
    return pl.pallas_call(
        functools.partial(_ssm_body, L=L, nb=nb, nc=nc),
        grid=(N_GROUPS,),
        in_specs=[grp(r, lh), grp(SSM_GROUP, lh), grp(lh, SSM_STATE), grp(lh, SSM_STATE),
                  grp(SSM_STATE, lh), grp(SSM_STATE, lh), grp(1, SSM_STATE), grp(1, SSM_STATE),
                  grp(nb, SSM_STATE), grp(nb, SSM_STATE)],
        out_specs=[grp(r, lh), grp(nb, SSM_STATE), grp(nb, SSM_STATE)],
        out_shape=[
            jax.ShapeDtypeStruct((N_GROUPS, r, lh), BF16),
            jax.ShapeDtypeStruct((N_GROUPS, nb, SSM_STATE), F32),
            jax.ShapeDtypeStruct((N_GROUPS, nb, SSM_STATE), F32),
        ],
        scratch_shapes=[pltpu.VMEM((lh, lh), BF16), pltpu.VMEM((r, SSM_STATE), F32),
                        pltpu.VMEM((r, SSM_STATE), F32)],
        compiler_params=_params("arbitrary"),
        name="s5",
    )(u2, w['kflat'], w['wst_re'], w['wst_im'], w['wint_re'], w['wint_im'], w['lam_re'], w['lam_im'], s0_re, s0_im)


def _s5_branch(u, w, s0_re, s0_im, L, nb):
    n, t, _ = u.shape
    nc = t // L
    pad = nb - n
    u5 = u.reshape(n, nc, L, N_GROUPS, SSM_GROUP)
    u5 = jnp.pad(u5, ((0, pad), (0, 0), (0, 0), (0, 0), (0, 0)))
    u2 = jnp.transpose(u5, (3, 1, 0, 2, 4)).reshape(N_GROUPS, nc * nb, L * SSM_GROUP)
    s0 = lambda s: jnp.pad(jnp.transpose(s.astype(F32), (1, 0, 2)), ((0, 0), (0, pad), (0, 0)))
    y2, s_re, s_im = _ssm(u2, w, s0(s0_re), s0(s0_im), L, nb, nc)
    y5 = y2.reshape(N_GROUPS, nc, nb, L, SSM_GROUP)
    y = jnp.transpose(y5, (2, 1, 3, 0, 4))[:n].reshape(n, t, D_SSM)
    fin = lambda s: jnp.transpose(s, (1, 0, 2))[:n]
    return y, fin(s_re), fin(s_im)


def _attend(qc, kb, vb, bias_ref, valid):
    tq = qc.shape[0]
    left = lax.broadcasted_iota(jnp.int32, (tq, LANES), 1) < HEAD_DIM
    zero = jnp.zeros((), BF16)
    pairs = range(N_HEADS // 2)
    cols = [slice(hp * LANES, (hp + 1) * LANES) for hp in pairs]
    scores = []
    for hp in pairs:
        qp = qc[:, cols[hp]]
        q2 = jnp.concatenate([jnp.where(left, qp, zero), jnp.where(left, zero, qp)], axis=0)
        sc = lax.dot_general(q2, kb[:, cols[hp]], (((1,), (1,)), ((), ())), preferred_element_type=F32)
        sc = sc + bias_ref[hp]
        if valid is not None:
            sc = jnp.where(valid, sc, MASK_VALUE)
        scores.append(sc)
    probs, dens = [], []
    for sc in scores:
        e = jnp.exp(sc - jnp.max(sc, axis=-1, keepdims=True))
        dens.append(jnp.sum(e, axis=-1, keepdims=True))
        probs.append(e.astype(BF16))
    outs = []
    for hp in pairs:
        o2 = _dot(probs[hp], vb[:, cols[hp]]) / dens[hp]
        outs.append(jnp.where(left, o2[:tq], o2[tq:]))
    return jnp.concatenate(outs, axis=-1)


def _attn_prompt_body(q_ref, k_ref, v_ref, bias_ref, o_ref, kpad, vpad):
    t = q_ref.shape[1]
    zeros = jnp.zeros((ATTN_REACH, D_ATTN), BF16)
    kpad[0:ATTN_REACH, :] = zeros
    vpad[0:ATTN_REACH, :] = zeros
    kpad[ATTN_REACH:ATTN_REACH + t, :] = k_ref[0]
    vpad[ATTN_REACH:ATTN_REACH + t, :] = v_ref[0]
    col = lax.broadcasted_iota(jnp.int32, (2 * CHUNK, BAND), 1)

    def chunk(c, masked):
        r0 = pl.multiple_of(c * CHUNK, CHUNK)
        qc = q_ref[0, pl.ds(r0, CHUNK), :]
        kb = kpad[pl.ds(r0, BAND), :]
        vb = vpad[pl.ds(r0, BAND), :]
        valid = (col >= ATTN_REACH - r0) if masked else None
        o_ref[0, pl.ds(r0, CHUNK), :] = _attend(qc, kb, vb, bias_ref, valid).astype(BF16)

    n_chunks = t // CHUNK
    lax.fori_loop(0, min(LEFT_CHUNKS, n_chunks), lambda c, _: chunk(c, True), None)
    lax.fori_loop(LEFT_CHUNKS, n_chunks, lambda c, _: chunk(c, False), None)


def _attn_prompt(qkv, bias):
    n, t, _ = qkv.shape
    part = lambda j: pl.BlockSpec((1, t, D_ATTN), lambda i: (i, 0, j))
    return pl.pallas_call(
        _attn_prompt_body,
        grid=(n,),
        in_specs=[part(0), part(1), part(2), _const_spec((N_HEADS // 2, 2 * CHUNK, BAND))],
        out_specs=pl.BlockSpec((1, t, D_ATTN), lambda i: (i, 0, 0)),
        out_shape=jax.ShapeDtypeStruct((n, t, D_ATTN), BF16),
        scratch_shapes=[pltpu.VMEM((ATTN_REACH + t, D_ATTN), BF16), pltpu.VMEM((ATTN_REACH + t, D_ATTN), BF16)],
        compiler_params=_params("arbitrary"),
        name="attn_prompt",
    )(qkv, qkv, qkv, bias)


def _attn_sample_body(q_ref, k_ref, v_ref, bias_ref, o_ref):
    o_ref[0] = _attend(q_ref[0], k_ref[0], v_ref[0], bias_ref, None).astype(BF16)


def _attn_sample(q, kk, vv, bias):
    n, tq, _ = q.shape
    tk = kk.shape[1]
    return pl.pallas_call(
        _attn_sample_body,
        grid=(n,),
        in_specs=[pl.BlockSpec((1, tq, D_ATTN), lambda i: (i, 0, 0)),
                  pl.BlockSpec((1, tk, D_ATTN), lambda i: (i, 0, 0)),
                  pl.BlockSpec((1, tk, D_ATTN), lambda i: (i, 0, 0)),
                  _const_spec((N_HEADS // 2, 2 * tq, tk))],
        out_specs=pl.BlockSpec((1, tq, D_ATTN), lambda i: (i, 0, 0)),
        out_shape=jax.ShapeDtypeStruct((n, tq, D_ATTN), BF16),
        compiler_params=_params("arbitrary"),
        name="attn_sample",
    )(q, kk, vv, bias)


def _rel_bias(table, tq, tk):
    z = np.arange(tq + tk - 1)
    idx = np.clip(ATTN_REACH + (tq - 1) - z, -MAX_REL, MAX_REL) + MAX_REL
    ext = table.astype(F32)[:, idx]
    bias = jnp.stack([ext[:, tq - 1 - i:tq - 1 - i + tk] for i in range(tq)], axis=1)
    return bias.reshape(N_HEADS // 2, 2 * tq, tk)


def _ffn_body(*refs, multi_seq):
    if multi_seq:
        (x_ref, ys_ref, att_ref, gate_ref, h1_ref, h2_ref, wglu, watt, wo, gffn, wup, cw, cb, wdown, gfin,
         y_ref, conv_ref, a_scr) = refs
    else:
        (x_ref, ys_ref, att_ref, gate_ref, hist_ref, wglu, watt, wo, gffn, wup, cw, cb, wdown, gfin,
         y_ref, conv_ref, a_scr, carry) = refs
    tm = x_ref.shape[1]
    gl = _dot(ys_ref[0], wglu[...])
    br_ssm = gl[:, :D_MODEL] * _sigmoid(gl[:, D_MODEL:])
    br_att = _dot(att_ref[0], watt[...])
    mix = (_sigmoid(gate_ref[0, :, :D_MODEL].astype(F32)) * br_ssm
           + _sigmoid(gate_ref[0, :, D_MODEL:].astype(F32)) * br_att)
    h = x_ref[0] + _dot(mix.astype(BF16), wo[...])
    hn = _rms(h, gffn[...]).astype(BF16)

    if multi_seq:
        pos = lax.broadcasted_iota(jnp.int32, (tm, 1), 0) % multi_seq
    else:
        @pl.when(pl.program_id(1) == 0)
        def _():
            carry[...] = hist_ref[0]

    acc = jnp.zeros((tm, D_MODEL), F32)
    for lo, hi in FF_CHUNKS:
        a = _dot(hn, wup[:, lo:hi])
        b = _dot(hn, wup[:, D_FF + lo:D_FF + hi])
        a_scr[8:8 + tm, 0:hi - lo] = a
        if multi_seq:
            a_scr[6:8, 0:hi - lo] = jnp.zeros((2, hi - lo), F32)
            conv_ref[0, :, lo:hi] = a
        else:
            a_scr[6:8, 0:hi - lo] = carry[:, lo:hi]
            carry[:, lo:hi] = a[tm - 2:tm, :]
        a1 = a_scr[7:7 + tm, 0:hi - lo]
        a2 = a_scr[6:6 + tm, 0:hi - lo]
        if multi_seq:
            a1 = jnp.where(pos < 1, h1_ref[:, lo:hi], a1)
            a2 = jnp.where(pos < 2, h2_ref[:, lo:hi], a2)
        c = cb[:, lo:hi] + cw[0:1, lo:hi] * a2 + cw[1:2, lo:hi] * a1 + cw[2:3, lo:hi] * a
        act = (_gelu(c) * b).astype(BF16)
        acc = acc + _dot(act, wdown[lo:hi, :])
    y_ref[0] = _rms(h + acc, gfin[...])

    if not multi_seq:
        @pl.when(pl.program_id(1) == pl.num_programs(1) - 1)
        def _():
            conv_ref[0] = carry[...]


def _ffn_weight_specs():
    return [_const_spec((D_SSM, 2 * D_MODEL)), _const_spec((D_ATTN, D_MODEL)), _const_spec((D_MODEL, D_MODEL)),
            _const_spec((1, D_MODEL)), _const_spec((D_MODEL, 2 * D_FF)), _const_spec((CONV_W, D_FF)),
            _const_spec((1, D_FF)), _const_spec((D_FF, D_MODEL)), _const_spec((1, D_MODEL))]


def _ffn_prompt(x, ys, att, gates, hist, wts, tm):
    n, t, _ = x.shape
    tok = lambda w: pl.BlockSpec((1, tm, w), lambda i, j: (i, j, 0))
    seq = pl.BlockSpec((1, CONV_W - 1, D_FF), lambda i, j: (i, 0, 0))
    fc = max(hi - lo for lo, hi in FF_CHUNKS)
    return pl.pallas_call(
        functools.partial(_ffn_body, multi_seq=0),
        grid=(n, t // tm),
        in_specs=[tok(D_MODEL), tok(D_SSM), tok(D_ATTN), tok(2 * D_MODEL), seq] + _ffn_weight_specs(),
        out_specs=[tok(D_MODEL), seq],
        out_shape=[jax.ShapeDtypeStruct((n, t, D_MODEL), F32),
                   jax.ShapeDtypeStruct((n, CONV_W - 1, D_FF), F32)],
        scratch_shapes=[pltpu.VMEM((tm + 8, fc), F32), pltpu.VMEM((CONV_W - 1, D_FF), F32)],
        compiler_params=_params("arbitrary", "arbitrary"),
        name="merge_ffn_prompt",
    )(x, ys, att, gates, hist, *wts)


def _ffn_sample(x, ys, att, gates, h1, h2, wts, seq_len):
    rows = x.shape[1]
    tok = lambda w: pl.BlockSpec((1, rows, w), lambda i: (0, 0, 0))
    full = pl.BlockSpec((rows, D_FF), lambda i: (0, 0))
    fc = max(hi - lo for lo, hi in FF_CHUNKS)
    return pl.pallas_call(
        functools.partial(_ffn_body, multi_seq=seq_len),
        grid=(1,),
        in_specs=[tok(D_MODEL), tok(D_SSM), tok(D_ATTN), tok(2 * D_MODEL), full, full] + _ffn_weight_specs(),
        out_specs=[tok(D_MODEL), tok(D_FF)],
        out_shape=[jax.ShapeDtypeStruct((1, rows, D_MODEL), F32),
                   jax.ShapeDtypeStruct((1, rows, D_FF), F32)],
        scratch_shapes=[pltpu.VMEM((rows + 8, fc), F32)],
        compiler_params=_params("arbitrary"),
        name="merge_ffn_sample",
    )(x, ys, att, gates, h1, h2, *wts)


PROMPT_TOKEN_TILE = 512
FFN_TOKEN_TILE = 256
SAMPLE_STREAM_PAD = 16


def kernel(x_prompt, x_sample, state_ssm_re, state_ssm_im, cache_attn_k, cache_attn_v, cache_conv,
           g_mix, w_in, ssm_lambda_re, ssm_lambda_im, ssm_log_dt, ssm_b_re, ssm_b_im,
           ssm_c_re, ssm_c_im, ssm_d, w_ssm_glu, attn_rel_bias, w_attn_up, w_o,
           g_ffn, w_up, conv_w, conv_b, w_down, g_final):
    depth = w_in.shape[0]
    nb, seq, _ = x_prompt.shape
    db, dseq, _ = x_sample.shape
    assert depth == 1 and seq % PROMPT_TOKEN_TILE == 0 and min(ATTN_REACH, seq) == PROMPT_TOKEN_TILE
    l = 0

    col_scale = jnp.ones((D_IN,), F32).at[D_SSM:D_SSM + D_ATTN].set(HEAD_DIM ** -0.5)
    w_in_b = (w_in[l] * col_scale[None, :]).astype(BF16)
    wts = (w_ssm_glu[l].astype(BF16), w_attn_up[l].astype(BF16), w_o[l].astype(BF16), g_ffn[l][None, :],
           w_up[l].astype(BF16), conv_w[l], conv_b[l][None, :], w_down[l].astype(BF16), g_final[None, :])
    g_mix_l = g_mix[l][None, :]
    ssm_args = (ssm_lambda_re[l], ssm_lambda_im[l], ssm_log_dt[l], ssm_b_re[l], ssm_b_im[l],
                ssm_c_re[l], ssm_c_im[l], ssm_d[l])
    w_ssm_p = _ssm_weights(*ssm_args, CHUNK)
    w_ssm_s = _ssm_weights(*ssm_args, dseq)
    table = attn_rel_bias[l]
    bias_p = _rel_bias(table, CHUNK, BAND)
    w_cache = cache_attn_k.shape[2]
    assert w_cache == ATTN_REACH
    bias_s = _rel_bias(table, dseq, w_cache + dseq)

    u, qkv, gates, k32, v32 = _inproj(x_prompt, g_mix_l, w_in_b, PROMPT_TOKEN_TILE)
    zero_state = jnp.zeros((nb, N_GROUPS, SSM_STATE), F32)
    ys, p_re, p_im = _s5_branch(u, w_ssm_p, zero_state, zero_state, CHUNK, nb)
    att = _attn_prompt(qkv, bias_p)
    zero_conv = jnp.zeros((nb, CONV_W - 1, D_FF), F32)
    y_prompt, p_conv = _ffn_prompt(x_prompt, ys, att, gates, zero_conv, wts, FFN_TOKEN_TILE)

    rows = db * dseq
    xs = x_sample.reshape(1, rows, D_MODEL)
    us, qkvs, gates_s, ks32, vs32 = _inproj(xs, g_mix_l, w_in_b, rows)
    yss, s_re, s_im = _s5_branch(us.reshape(db, dseq, D_SSM), w_ssm_s, state_ssm_re[l], state_ssm_im[l],
                                 dseq, SAMPLE_STREAM_PAD)
    qkvs3 = qkvs.reshape(db, dseq, 3 * D_ATTN)
    kk = jnp.concatenate([cache_attn_k[l].reshape(db, w_cache, D_ATTN).astype(BF16),
                          qkvs3[:, :, D_ATTN:2 * D_ATTN]], axis=1)
    vv = jnp.concatenate([cache_attn_v[l].reshape(db, w_cache, D_ATTN).astype(BF16),
                          qkvs3[:, :, 2 * D_ATTN:]], axis=1)
    att_s = _attn_sample(qkvs3[:, :, :D_ATTN], kk, vv, bias_s)
    hist = cache_conv[l].astype(F32)
    pad_rows = ((0, 0), (0, dseq - 1), (0, 0))
    h1 = jnp.pad(hist[:, 1:2], pad_rows).reshape(rows, D_FF)
    h2 = jnp.pad(hist, ((0, 0), (0, dseq - 2), (0, 0))).reshape(rows, D_FF)
    y_s, a_s = _ffn_sample(xs, yss.reshape(1, rows, D_SSM), att_s.reshape(1, rows, D_ATTN), gates_s, h1, h2,
                           wts, dseq)
    s_conv = a_s.reshape(db, dseq, D_FF)[:, dseq - (CONV_W - 1):]

    heads = lambda a, n, t: a.reshape(1, n, t, N_HEADS, HEAD_DIM)
    return (y_prompt, y_s.reshape(db, dseq, D_MODEL),
            p_re[None], p_im[None],
            heads(k32, nb, PROMPT_TOKEN_TILE), heads(v32, nb, PROMPT_TOKEN_TILE), p_conv[None],
            s_re[None], s_im[None],
            heads(ks32, db, dseq), heads(vs32, db, dseq), s_conv[None])
```

```python
import functools
import math

import jax
import jax.numpy as jnp
import numpy as np
from jax import lax
from jax.experimental import pallas as pl
from jax.experimental.pallas import tpu as pltpu

F32 = jnp.float32
BF16 = jnp.bfloat16

D_MODEL = 1024
CHUNK = 64
LEFT_CHUNKS = 8
ATTN_REACH = LEFT_CHUNKS * CHUNK
BAND = ATTN_REACH + CHUNK
D_SSM = D_MODEL // 2
D_ATTN = D_MODEL // 2
HEAD_DIM = 64
N_HEADS = D_ATTN // HEAD_DIM
MAX_REL = 128
SSM_GROUP = 16
N_GROUPS = D_SSM // SSM_GROUP
SSM_STATE = 64
D_FF = 2816
CONV_W = 3
D_IN = D_SSM + 3 * D_ATTN + 2 * D_MODEL
RMS_EPS = 1e-6
MASK_VALUE = -1e30

V7X_VMEM_LIMIT_BYTES = 56 * 1024 * 1024
LANES = 128
FF_CHUNKS = ((0, 768), (768, 1536), (1536, 2304), (2304, 2816))


def _params(*sem):
    return pltpu.CompilerParams(dimension_semantics=sem, vmem_limit_bytes=V7X_VMEM_LIMIT_BYTES)


def _const_spec(shape):
    zeros = (0,) * len(shape)
    return pl.BlockSpec(shape, lambda *_: zeros, pipeline_mode=pl.Buffered(1))


def _rms(x, g):
    return x * lax.rsqrt(jnp.mean(x * x, axis=-1, keepdims=True) + RMS_EPS) * g


def _gelu(x):
    k = math.sqrt(2.0 / math.pi)
    return (0.5 * x) * (1.0 + jnp.tanh(x * (k + (k * 0.044715) * (x * x))))


def _sigmoid(x):
    return 1.0 / (1.0 + jnp.exp(-x))


def _dot(a, b):
    return jnp.dot(a, b, preferred_element_type=F32)


S5_CHUNK = 16
SUPER = D_SSM // LANES
SUPER_GROUPS = N_GROUPS // SUPER
SUPER_STATE = 2 * SUPER_GROUPS * SSM_STATE
S5_ROW = S5_CHUNK * LANES
S5_PROMPT_SEQS = 4


def _inproj_body(x_ref, g_ref, w_ref, u_ref, qkv_ref, gate_ref, k32_ref, v32_ref):
    xn = _rms(x_ref[0], g_ref[...]).astype(BF16)

    def seg(lo, hi):
        return _dot(xn, w_ref[:, lo:hi])

    u = seg(0, D_SSM).astype(BF16)
    for s in range(SUPER):
        u_ref[s, 0] = u[:, s * LANES:(s + 1) * LANES]
    q0 = D_SSM
    qkv_ref[0, :, 0:D_ATTN] = seg(q0, q0 + D_ATTN).astype(BF16)
    k = seg(q0 + D_ATTN, q0 + 2 * D_ATTN)
    v = seg(q0 + 2 * D_ATTN, q0 + 3 * D_ATTN)
    qkv_ref[0, :, D_ATTN:2 * D_ATTN] = k.astype(BF16)
    qkv_ref[0, :, 2 * D_ATTN:3 * D_ATTN] = v.astype(BF16)
    g0 = q0 + 3 * D_ATTN
    gate_ref[0, :, 0:D_MODEL] = seg(g0, g0 + D_MODEL).astype(BF16)
    gate_ref[0, :, D_MODEL:2 * D_MODEL] = seg(g0 + D_MODEL, g0 + 2 * D_MODEL).astype(BF16)

    @pl.when(pl.program_id(1) == pl.num_programs(1) - 1)
    def _():
        k32_ref[0] = k
        v32_ref[0] = v


def _inproj(x, g_mix, w_in_b, tm):
    n, t, _ = x.shape
    tok = lambda w: pl.BlockSpec((1, tm, w), lambda i, j: (i, j, 0))
    keep = pl.BlockSpec((1, tm, D_ATTN), lambda i, j: (i, 0, 0))
    u_spec = pl.BlockSpec((SUPER, 1, tm, LANES), lambda i, j: (0, i, j, 0))
    return pl.pallas_call(
        _inproj_body,
        grid=(n, t // tm),
        in_specs=[tok(D_MODEL), _const_spec((1, D_MODEL)), _const_spec((D_MODEL, D_IN))],
        out_specs=[u_spec, tok(3 * D_ATTN), tok(2 * D_MODEL), keep, keep],
        out_shape=[
            jax.ShapeDtypeStruct((SUPER, n, t, LANES), BF16),
            jax.ShapeDtypeStruct((n, t, 3 * D_ATTN), BF16),
            jax.ShapeDtypeStruct((n, t, 2 * D_MODEL), BF16),
            jax.ShapeDtypeStruct((n, tm, D_ATTN), F32),
            jax.ShapeDtypeStruct((n, tm, D_ATTN), F32),
        ],
        compiler_params=_params("arbitrary", "arbitrary"),
        name="inproj",
    )(x, g_mix, w_in_b)


def _s5_weights(lam_re, lam_im, log_dt, b_re, b_im, c_re, c_im, d_skip):
    L = S5_CHUNK
    hi = lax.Precision.HIGHEST
    f = lambda a: a.astype(F32)
    cmul = lambda ar, ai, br, bi: (ar * br - ai * bi, ar * bi + ai * br)
    a_re, a_im = f(lam_re), f(lam_im)
    dt = jnp.exp(f(log_dt))[:, None]
    tau = jnp.arange(L + 1, dtype=F32)[None, :, None]
    mag = jnp.exp((a_re * dt)[:, None, :] * tau)
    ang = (a_im * dt)[:, None, :] * tau
    pw_re, pw_im = mag * jnp.cos(ang), mag * jnp.sin(ang)
    n_re, n_im = pw_re[:, 1] - 1.0, pw_im[:, 1]
    den = a_re * a_re + a_im * a_im
    f_re, f_im = (n_re * a_re + n_im * a_im) / den, (n_im * a_re - n_re * a_im) / den
    bb_re, bb_im = cmul(f_re[..., None], f_im[..., None], f(b_re), f(b_im))
    cp_re, cp_im = cmul(f(c_re)[:, :, None, :], f(c_im)[:, :, None, :], pw_re[:, None], pw_im[:, None])
    kt = (jnp.einsum('gatp,gph->ghta', cp_re[:, :, :L], bb_re, precision=hi)
          - jnp.einsum('gatp,gph->ghta', cp_im[:, :, :L], bb_im, precision=hi))
    kt = kt.at[:, :, 0, :].add(f(d_skip)[:, :, None] * jnp.eye(SSM_GROUP, dtype=F32)[None])
    rev = L - 1 - np.arange(L)
    bt_re, bt_im = jnp.swapaxes(bb_re, 1, 2)[:, None], jnp.swapaxes(bb_im, 1, 2)[:, None]
    wst_re, wst_im = cmul(pw_re[:, rev][:, :, None, :], pw_im[:, rev][:, :, None, :], bt_re, bt_im)

    eye = jnp.eye(SUPER_GROUPS, dtype=F32)
    eye7 = eye.reshape(1, 1, SUPER_GROUPS, 1, 1, SUPER_GROUPS, 1)
    sg = lambda a: a.reshape((SUPER, SUPER_GROUPS) + a.shape[1:])
    bblk = sg(kt)[:, :, :, :, None, :] * eye[None, :, None, None, :, None]
    bblk = bblk.reshape(SUPER, LANES, S5_ROW)
    w = jnp.stack([sg(wst_re), sg(wst_im)], axis=4)
    w = jnp.transpose(w, (0, 2, 1, 3, 4, 5))[:, :, :, :, :, None, :]
    wst = (w * eye7).reshape(SUPER, S5_ROW, SUPER_STATE)
    c = jnp.stack([sg(cp_re[:, :, 1:L + 1]), sg(-cp_im[:, :, 1:L + 1])], axis=1)
    c = jnp.transpose(c, (0, 1, 2, 5, 4, 3))[:, :, :, :, :, None, :]
    wint = (c * eye7).reshape(SUPER, SUPER_STATE, S5_ROW)
    half = SUPER_STATE // 2
    lam = jnp.concatenate([pw_re[:, L].reshape(SUPER, half), pw_im[:, L].reshape(SUPER, half)], axis=-1)
    return bblk.astype(BF16), wst.astype(BF16), wint.astype(BF16), lam[:, None, :]


def _s5_body(u_ref, bblk_ref, wst_ref, wint_ref, lam_ref, s0_ref, y_ref, sfin_ref, toep, x_scr, prev_scr,
             *, ns, nc):
    half = SUPER_STATE // 2

    @pl.when(pl.program_id(1) == 0)
    def _():
        for j in range(S5_CHUNK):
            rows = slice(j * LANES, (j + 1) * LANES)
            if j:
                toep[rows, 0:j * LANES] = jnp.zeros((LANES, j * LANES), BF16)
            toep[rows, j * LANES:S5_ROW] = bblk_ref[0, :, 0:S5_ROW - j * LANES]

    u = u_ref[0]
    x = _dot(u, wst_ref[0])
    nblk = SUPER_STATE // LANES
    hblk = nblk // 2
    for k in range(nblk):
        x_scr[k] = x[:, k * LANES:(k + 1) * LANES]
    lam = [lam_ref[0, :, k * LANES:(k + 1) * LANES] for k in range(nblk)]

    def step(c, s):
        rows = pl.ds(c, ns, stride=nc)
        nxt = []
        for k in range(nblk):
            prev_scr[k, rows, :] = s[k]
        for k in range(hblk):
            l_re, l_im, s_re, s_im = lam[k], lam[hblk + k], s[k], s[hblk + k]
            nxt.append((l_re * s_re - l_im * s_im + x_scr[k, rows, :],
                        l_re * s_im + l_im * s_re + x_scr[hblk + k, rows, :]))
        return tuple(p[0] for p in nxt) + tuple(p[1] for p in nxt)

    s0 = s0_ref[0, 0]
    s_fin = lax.fori_loop(0, nc, step, tuple(s0[:, k * LANES:(k + 1) * LANES] for k in range(nblk)))
    for k in range(nblk):
        sfin_ref[0, 0, :, k * LANES:(k + 1) * LANES] = s_fin[k]

    prev = jnp.concatenate([prev_scr[k] for k in range(nblk)], axis=-1).astype(BF16)
    blk = 2 * LANES
    for tb in range(S5_ROW // blk):
        cols = slice(tb * blk, (tb + 1) * blk)
        depth = (tb + 1) * blk
        y = _dot(u[:, 0:depth], toep[0:depth, cols]) + _dot(prev, wint_ref[0, :, cols])
        y_ref[0, :, cols] = _gelu(y).astype(BF16)


def _s5(u2, weights, s0, ns, nc):
    bblk, wst, wint, lam = weights
    tiles = s0.shape[1]
    r = ns * nc
    per_sg = lambda a, b: pl.BlockSpec((1, a, b), lambda s, i: (s, 0, 0))
    rows = pl.BlockSpec((1, r, S5_ROW), lambda s, i: (s, i, 0))
    state = pl.BlockSpec((1, 1, ns, SUPER_STATE), lambda s, i: (s, i, 0, 0))
    return pl.pallas_call(
        functools.partial(_s5_body, ns=ns, nc=nc),
        grid=(SUPER, tiles),
        in_specs=[rows, per_sg(LANES, S5_ROW), per_sg(S5_ROW, SUPER_STATE), per_sg(SUPER_STATE, S5_ROW),
                  per_sg(1, SUPER_STATE), state],
        out_specs=[rows, state],
        out_shape=[jax.ShapeDtypeStruct((SUPER, tiles * r, S5_ROW), BF16),
                   jax.ShapeDtypeStruct((SUPER, tiles, ns, SUPER_STATE), F32)],
        scratch_shapes=[pltpu.VMEM((S5_ROW, S5_ROW), BF16), pltpu.VMEM((SUPER_STATE // LANES, r, LANES), F32),
                        pltpu.VMEM((SUPER_STATE // LANES, r, LANES), F32)],
        compiler_params=_params("arbitrary", "arbitrary"),
        name="s5",
    )(u2, bblk, wst, wint, lam, s0)


def _state_to_super(s_re, s_im, tiles, ns):
    n = s_re.shape[0]
    half = SUPER_STATE // 2
    s = jnp.concatenate([s_re.astype(F32).reshape(n, SUPER, half), s_im.astype(F32).reshape(n, SUPER, half)], -1)
    s = jnp.pad(jnp.transpose(s, (1, 0, 2)), ((0, 0), (0, tiles * ns - n), (0, 0)))
    return s.reshape(SUPER, tiles, ns, SUPER_STATE)


def _state_from_super(s, n):
    half = SUPER_STATE // 2
    s = jnp.transpose(s.reshape(SUPER, -1, SUPER_STATE)[:, :n], (1, 0, 2))
    return (s[:, :, :half].reshape(n, N_GROUPS, SSM_STATE), s[:, :, half:].reshape(n, N_GROUPS, SSM_STATE))


def _attend(qc, kb, vb, bias_ref, valid):
    tq = qc.shape[0]
    left = lax.broadcasted_iota(jnp.int32, (tq, LANES), 1) < HEAD_DIM
    zero = jnp.zeros((), BF16)
    pairs = range(N_HEADS // 2)
    cols = [slice(hp * LANES, (hp + 1) * LANES) for hp in pairs]
    scores = []
    for hp in pairs:
        qp = qc[:, cols[hp]]
        q2 = jnp.concatenate([jnp.where(left, qp, zero), jnp.where(left, zero, qp)], axis=0)
        sc = lax.dot_general(q2, kb[:, cols[hp]], (((1,), (1,)), ((), ())), preferred_element_type=F32)
        sc = sc + bias_ref[hp]
        if valid is not None:
            sc = jnp.where(valid, sc, MASK_VALUE)
        scores.append(sc)
    probs, dens = [], []
    for sc in scores:
        e = jnp.exp(sc - jnp.max(sc, axis=-1, keepdims=True))
        dens.append(jnp.sum(e, axis=-1, keepdims=True))
        probs.append(e.astype(BF16))
    outs = []
    for hp in pairs:
        o2 = _dot(probs[hp], vb[:, cols[hp]]) / dens[hp]
        outs.append(jnp.where(left, o2[:tq], o2[tq:]))
    return jnp.concatenate(outs, axis=-1)


def _attn_prompt_body(q_ref, k_ref, v_ref, bias_ref, o_ref, kpad, vpad):
    t = q_ref.shape[1]
    zeros = jnp.zeros((ATTN_REACH, D_ATTN), BF16)
    kpad[0:ATTN_REACH, :] = zeros
    vpad[0:ATTN_REACH, :] = zeros
    kpad[ATTN_REACH:ATTN_REACH + t, :] = k_ref[0]
    vpad[ATTN_REACH:ATTN_REACH + t, :] = v_ref[0]
    col = lax.broadcasted_iota(jnp.int32, (2 * CHUNK, BAND), 1)

    def chunk(c, masked):
        r0 = pl.multiple_of(c * CHUNK, CHUNK)
        qc = q_ref[0, pl.ds(r0, CHUNK), :]
        kb = kpad[pl.ds(r0, BAND), :]
        vb = vpad[pl.ds(r0, BAND), :]
        valid = (col >= ATTN_REACH - r0) if masked else None
        o_ref[0, pl.ds(r0, CHUNK), :] = _attend(qc, kb, vb, bias_ref, valid).astype(BF16)

    n_chunks = t // CHUNK
    lax.fori_loop(0, min(LEFT_CHUNKS, n_chunks), lambda c, _: chunk(c, True), None, unroll=2)
    lax.fori_loop(LEFT_CHUNKS, n_chunks, lambda c, _: chunk(c, False), None, unroll=2)


def _attn_prompt(qkv, bias):
    n, t, _ = qkv.shape
    part = lambda j: pl.BlockSpec((1, t, D_ATTN), lambda i: (i, 0, j))
    return pl.pallas_call(
        _attn_prompt_body,
        grid=(n,),
        in_specs=[part(0), part(1), part(2), _const_spec((N_HEADS // 2, 2 * CHUNK, BAND))],
        out_specs=pl.BlockSpec((1, t, D_ATTN), lambda i: (i, 0, 0)),
        out_shape=jax.ShapeDtypeStruct((n, t, D_ATTN), BF16),
        scratch_shapes=[pltpu.VMEM((ATTN_REACH + t, D_ATTN), BF16), pltpu.VMEM((ATTN_REACH + t, D_ATTN), BF16)],
        compiler_params=_params("arbitrary"),
        name="attn_prompt",
    )(qkv, qkv, qkv, bias)


def _attn_sample_body(q_ref, k_ref, v_ref, bias_ref, o_ref):
    o_ref[0] = _attend(q_ref[0], k_ref[0], v_ref[0], bias_ref, None).astype(BF16)


def _attn_sample(q, kk, vv, bias):
    n, tq, _ = q.shape
    tk = kk.shape[1]
    return pl.pallas_call(
        _attn_sample_body,
        grid=(n,),
        in_specs=[pl.BlockSpec((1, tq, D_ATTN), lambda i: (i, 0, 0)),
                  pl.BlockSpec((1, tk, D_ATTN), lambda i: (i, 0, 0)),
                  pl.BlockSpec((1, tk, D_ATTN), lambda i: (i, 0, 0)),
                  _const_spec((N_HEADS // 2, 2 * tq, tk))],
        out_specs=pl.BlockSpec((1, tq, D_ATTN), lambda i: (i, 0, 0)),
        out_shape=jax.ShapeDtypeStruct((n, tq, D_ATTN), BF16),
        compiler_params=_params("arbitrary"),
        name="attn_sample",
    )(q, kk, vv, bias)


def _rel_bias(table, tq, tk):
    width = tq + tk - 1
    z = np.arange(width)
    idx = np.clip(ATTN_REACH - np.where(z < tk, z, z - width), -MAX_REL, MAX_REL) + MAX_REL
    ext = table.astype(F32)[:, idx]
    flat = jnp.tile(ext, (1, tq))[:, :tq * (width - 1)]
    bias = flat.reshape(N_HEADS, tq, width - 1)[:, :, :tk]
    return bias.reshape(N_HEADS // 2, 2 * tq, tk)


def _ffn_body(*refs, multi_seq):
    if multi_seq:
        (x_ref, ys_ref, att_ref, gate_ref, h1_ref, h2_ref, wglu, watt, wo, gffn, wup, cw, cb, wdown, gfin,
         y_ref, conv_ref, a_scr) = refs
    else:
        (x_ref, ys_ref, att_ref, gate_ref, hist_ref, wglu, watt, wo, gffn, wup, cw, cb, wdown, gfin,
         y_ref, conv_ref, a_scr, carry) = refs
    tm = x_ref.shape[1]
    ys = jnp.concatenate([ys_ref[s, 0] for s in range(SUPER)], axis=-1)
    gl = _dot(ys, wglu[...])
    br_ssm = gl[:, :D_MODEL] * _sigmoid(gl[:, D_MODEL:])
    br_att = _dot(att_ref[0], watt[...])
    mix = (_sigmoid(gate_ref[0, :, :D_MODEL].astype(F32)) * br_ssm
           + _sigmoid(gate_ref[0, :, D_MODEL:].astype(F32)) * br_att)
    h = x_ref[0] + _dot(mix.astype(BF16), wo[...])
    hn = _rms(h, gffn[...]).astype(BF16)

    if multi_seq:
        pos = lax.broadcasted_iota(jnp.int32, (tm, 1), 0) % multi_seq
    else:
        @pl.when(pl.program_id(1) == 0)
        def _():
            carry[...] = hist_ref[0]

    acc = jnp.zeros((tm, D_MODEL), F32)
    for lo, hi in FF_CHUNKS:
        a = _dot(hn, wup[:, lo:hi])
        b = _dot(hn, wup[:, D_FF + lo:D_FF + hi])
        a_scr[8:8 + tm, 0:hi - lo] = a
        if multi_seq:
            a_scr[6:8, 0:hi - lo] = jnp.zeros((2, hi - lo), F32)
            conv_ref[0, :, lo:hi] = a
        else:
            a_scr[6:8, 0:hi - lo] = carry[:, lo:hi]
            carry[:, lo:hi] = a[tm - 2:tm, :]
        a1 = a_scr[7:7 + tm, 0:hi - lo]
        a2 = a_scr[6:6 + tm, 0:hi - lo]
        if multi_seq:
            a1 = jnp.where(pos < 1, h1_ref[:, lo:hi], a1)
            a2 = jnp.where(pos < 2, h2_ref[:, lo:hi], a2)
        c = cb[:, lo:hi] + cw[0:1, lo:hi] * a2 + cw[1:2, lo:hi] * a1 + cw[2:3, lo:hi] * a
        act = (_gelu(c) * b).astype(BF16)
        acc = acc + _dot(act, wdown[lo:hi, :])
    y_ref[0] = _rms(h + acc, gfin[...])

    if not multi_seq:
        @pl.when(pl.program_id(1) == pl.num_programs(1) - 1)
        def _():
            conv_ref[0] = carry[...]


def _ffn_weight_specs():
    return [_const_spec((D_SSM, 2 * D_MODEL)), _const_spec((D_ATTN, D_MODEL)), _const_spec((D_MODEL, D_MODEL)),
            _const_spec((1, D_MODEL)), _const_spec((D_MODEL, 2 * D_FF)), _const_spec((CONV_W, D_FF)),
            _const_spec((1, D_FF)), _const_spec((D_FF, D_MODEL)), _const_spec((1, D_MODEL))]


def _ffn_prompt(x, ys, att, gates, hist, wts, tm):
    n, t, _ = x.shape
    tok = lambda w: pl.BlockSpec((1, tm, w), lambda i, j: (i, j, 0))
    ys_spec = pl.BlockSpec((SUPER, 1, tm, LANES), lambda i, j: (0, i, j, 0))
    seq = pl.BlockSpec((1, CONV_W - 1, D_FF), lambda i, j: (i, 0, 0))
    fc = max(hi - lo for lo, hi in FF_CHUNKS)
    return pl.pallas_call(
        functools.partial(_ffn_body, multi_seq=0),
        grid=(n, t // tm),
        in_specs=[tok(D_MODEL), ys_spec, tok(D_ATTN), tok(2 * D_MODEL), seq] + _ffn_weight_specs(),
        out_specs=[tok(D_MODEL), seq],
        out_shape=[jax.ShapeDtypeStruct((n, t, D_MODEL), F32),
                   jax.ShapeDtypeStruct((n, CONV_W - 1, D_FF), F32)],
        scratch_shapes=[pltpu.VMEM((tm + 8, fc), F32), pltpu.VMEM((CONV_W - 1, D_FF), F32)],
        compiler_params=_params("arbitrary", "arbitrary"),
        name="merge_ffn_prompt",
    )(x, ys, att, gates, hist, *wts)


def _ffn_sample(x, ys, att, gates, h1, h2, wts, seq_len):
    rows = x.shape[1]
    tok = lambda w: pl.BlockSpec((1, rows, w), lambda i: (0, 0, 0))
    ys_spec = pl.BlockSpec((SUPER, 1, rows, LANES), lambda i: (0, 0, 0, 0))
    full = pl.BlockSpec((rows, D_FF), lambda i: (0, 0))
    fc = max(hi - lo for lo, hi in FF_CHUNKS)
    return pl.pallas_call(
        functools.partial(_ffn_body, multi_seq=seq_len),
        grid=(1,),
        in_specs=[tok(D_MODEL), ys_spec, tok(D_ATTN), tok(2 * D_MODEL), full, full] + _ffn_weight_specs(),
        out_specs=[tok(D_MODEL), tok(D_FF)],
        out_shape=[jax.ShapeDtypeStruct((1, rows, D_MODEL), F32),
                   jax.ShapeDtypeStruct((1, rows, D_FF), F32)],
        scratch_shapes=[pltpu.VMEM((rows + 8, fc), F32)],
        compiler_params=_params("arbitrary"),
        name="merge_ffn_sample",
    )(x, ys, att, gates, h1, h2, *wts)


PROMPT_TOKEN_TILE = 512
FFN_TOKEN_TILE = 512
SAMPLE_STREAM_PAD = 16


def kernel(x_prompt, x_sample, state_ssm_re, state_ssm_im, cache_attn_k, cache_attn_v, cache_conv,
           g_mix, w_in, ssm_lambda_re, ssm_lambda_im, ssm_log_dt, ssm_b_re, ssm_b_im,
           ssm_c_re, ssm_c_im, ssm_d, w_ssm_glu, attn_rel_bias, w_attn_up, w_o,
           g_ffn, w_up, conv_w, conv_b, w_down, g_final):
    depth = w_in.shape[0]
    nb, seq, _ = x_prompt.shape
    db, dseq, _ = x_sample.shape
    assert depth == 1 and seq % PROMPT_TOKEN_TILE == 0 and min(ATTN_REACH, seq) == PROMPT_TOKEN_TILE
    assert seq % S5_CHUNK == 0 and dseq == S5_CHUNK and nb % S5_PROMPT_SEQS == 0 and db <= SAMPLE_STREAM_PAD
    l = 0

    col_scale = jnp.ones((D_IN,), F32).at[D_SSM:D_SSM + D_ATTN].set(HEAD_DIM ** -0.5)
    w_in_b = (w_in[l] * col_scale[None, :]).astype(BF16)
    wts = (w_ssm_glu[l].astype(BF16), w_attn_up[l].astype(BF16), w_o[l].astype(BF16), g_ffn[l][None, :],
           w_up[l].astype(BF16), conv_w[l], conv_b[l][None, :], w_down[l].astype(BF16), g_final[None, :])
    g_mix_l = g_mix[l][None, :]
    w_s5 = _s5_weights(ssm_lambda_re[l], ssm_lambda_im[l], ssm_log_dt[l], ssm_b_re[l], ssm_b_im[l],
                       ssm_c_re[l], ssm_c_im[l], ssm_d[l])
    table = attn_rel_bias[l]
    bias_p = _rel_bias(table, CHUNK, BAND)
    w_cache = cache_attn_k.shape[2]
    assert w_cache == ATTN_REACH
    bias_s = _rel_bias(table, dseq, w_cache + dseq)

    u, qkv, gates, k32, v32 = _inproj(x_prompt, g_mix_l, w_in_b, PROMPT_TOKEN_TILE)
    n_chunks = seq // S5_CHUNK
    tiles = nb // S5_PROMPT_SEQS
    zero_state = jnp.zeros((SUPER, tiles, S5_PROMPT_SEQS, SUPER_STATE), F32)
    ys, p_state = _s5(u.reshape(SUPER, nb * n_chunks, S5_ROW), w_s5, zero_state, S5_PROMPT_SEQS, n_chunks)
    p_re, p_im = _state_from_super(p_state, nb)
    att = _attn_prompt(qkv, bias_p)
    zero_conv = jnp.zeros((nb, CONV_W - 1, D_FF), F32)
    y_prompt, p_conv = _ffn_prompt(x_prompt, ys.reshape(SUPER, nb, seq, LANES), att, gates, zero_conv, wts,
                                   FFN_TOKEN_TILE)

    rows = db * dseq
    xs = x_sample.reshape(1, rows, D_MODEL)
    us, qkvs, gates_s, ks32, vs32 = _inproj(xs, g_mix_l, w_in_b, rows)
    us2 = jnp.pad(us.reshape(SUPER, db, S5_ROW), ((0, 0), (0, SAMPLE_STREAM_PAD - db), (0, 0)))
    s0 = _state_to_super(state_ssm_re[l], state_ssm_im[l], 1, SAMPLE_STREAM_PAD)
    yss, s_state = _s5(us2, w_s5, s0, SAMPLE_STREAM_PAD, 1)
    s_re, s_im = _state_from_super(s_state, db)
    yss = yss[:, :db].reshape(SUPER, 1, rows, LANES)
    qkvs3 = qkvs.reshape(db, dseq, 3 * D_ATTN)
    kk = jnp.concatenate([cache_attn_k[l].reshape(db, w_cache, D_ATTN).astype(BF16),
                          qkvs3[:, :, D_ATTN:2 * D_ATTN]], axis=1)
    vv = jnp.concatenate([cache_attn_v[l].reshape(db, w_cache, D_ATTN).astype(BF16),
                          qkvs3[:, :, 2 * D_ATTN:]], axis=1)
    att_s = _attn_sample(qkvs3[:, :, :D_ATTN], kk, vv, bias_s)
    hist = cache_conv[l].astype(F32)
    pad_rows = ((0, 0), (0, dseq - 1), (0, 0))
    h1 = jnp.pad(hist[:, 1:2], pad_rows).reshape(rows, D_FF)
    h2 = jnp.pad(hist, ((0, 0), (0, dseq - 2), (0, 0))).reshape(rows, D_FF)
    y_s, a_s = _ffn_sample(xs, yss, att_s.reshape(1, rows, D_ATTN), gates_s, h1, h2, wts, dseq)
    s_conv = a_s.reshape(db, dseq, D_FF)[:, dseq - (CONV_W - 1):]

    heads = lambda a, n, t: a.reshape(1, n, t, N_HEADS, HEAD_DIM)
    return (y_prompt, y_s.reshape(db, dseq, D_MODEL),
            p_re[None], p_im[None],
            heads(k32, nb, PROMPT_TOKEN_TILE), heads(v32, nb, PROMPT_TOKEN_TILE), p_conv[None],
            s_re[None], s_im[None],
            heads(ks32, db, dseq), heads(vs32, db, dseq), s_conv[None])
```

```python
import functools
import math

import jax
import jax.numpy as jnp
import numpy as np
from jax import lax
from jax.experimental import pallas as pl
from jax.experimental.pallas import tpu as pltpu

F32 = jnp.float32
BF16 = jnp.bfloat16

D_MODEL = 1024
CHUNK = 64
LEFT_CHUNKS = 8
ATTN_REACH = LEFT_CHUNKS * CHUNK
BAND = ATTN_REACH + CHUNK
D_SSM = D_MODEL // 2
D_ATTN = D_MODEL // 2
HEAD_DIM = 64
N_HEADS = D_ATTN // HEAD_DIM
MAX_REL = 128
SSM_GROUP = 16
N_GROUPS = D_SSM // SSM_GROUP
SSM_STATE = 64
D_FF = 2816
CONV_W = 3
D_IN = D_SSM + 3 * D_ATTN + 2 * D_MODEL
RMS_EPS = 1e-6
MASK_VALUE = -1e30

V7X_VMEM_LIMIT_BYTES = 56 * 1024 * 1024
LANES = 128
FF_CHUNKS = ((0, 768), (768, 1536), (1536, 2304), (2304, 2816))


def _params(*sem):
    return pltpu.CompilerParams(dimension_semantics=sem, vmem_limit_bytes=V7X_VMEM_LIMIT_BYTES)


def _const_spec(shape):
    zeros = (0,) * len(shape)
    return pl.BlockSpec(shape, lambda *_: zeros, pipeline_mode=pl.Buffered(1))


def _rms(x, g):
    return x * lax.rsqrt(jnp.mean(x * x, axis=-1, keepdims=True) + RMS_EPS) * g


def _gelu(x):
    k = math.sqrt(2.0 / math.pi)
    return (0.5 * x) * (1.0 + jnp.tanh(x * (k + (k * 0.044715) * (x * x))))


def _sigmoid(x):
    return 1.0 / (1.0 + jnp.exp(-x))


def _dot(a, b):
    return jnp.dot(a, b, preferred_element_type=F32)


S5_CHUNK = 16
SUPER = D_SSM // LANES
SUPER_GROUPS = N_GROUPS // SUPER
SUPER_STATE = 2 * SUPER_GROUPS * SSM_STATE
S5_ROW = S5_CHUNK * LANES
S5_PROMPT_SEQS = 4


def _inproj_body(x_ref, g_ref, w_ref, u_ref, qkv_ref, gate_ref, k32_ref, v32_ref, u_scr):
    xn = _rms(x_ref[0], g_ref[...]).astype(BF16)

    def seg(lo, hi):
        return _dot(xn, w_ref[:, lo:hi])

    u = seg(0, D_SSM)
    n_rows = u_scr.shape[1] // S5_CHUNK
    for s in range(SUPER):
        u_scr[s] = u[:, s * LANES:(s + 1) * LANES]
    for s in range(SUPER):
        for j in range(S5_CHUNK):
            frames = u_scr[s, pl.ds(j, n_rows, stride=S5_CHUNK), :]
            u_ref[s, :, j * LANES:(j + 1) * LANES] = frames.astype(BF16)
    q0 = D_SSM
    qkv_ref[0, :, 0:D_ATTN] = seg(q0, q0 + D_ATTN).astype(BF16)
    k = seg(q0 + D_ATTN, q0 + 2 * D_ATTN)
    v = seg(q0 + 2 * D_ATTN, q0 + 3 * D_ATTN)
    qkv_ref[0, :, D_ATTN:2 * D_ATTN] = k.astype(BF16)
    qkv_ref[0, :, 2 * D_ATTN:3 * D_ATTN] = v.astype(BF16)
    g0 = q0 + 3 * D_ATTN
    gate_ref[0, :, 0:D_MODEL] = seg(g0, g0 + D_MODEL).astype(BF16)
    gate_ref[0, :, D_MODEL:2 * D_MODEL] = seg(g0 + D_MODEL, g0 + 2 * D_MODEL).astype(BF16)

    @pl.when(pl.program_id(1) == pl.num_programs(1) - 1)
    def _():
        k32_ref[0] = k
        v32_ref[0] = v


def _inproj(x, g_mix, w_in_b, tm):
    n, t, _ = x.shape
    tiles = t // tm
    n_rows = tm // S5_CHUNK
    tok = lambda w: pl.BlockSpec((1, tm, w), lambda i, j: (i, j, 0))
    keep = pl.BlockSpec((1, tm, D_ATTN), lambda i, j: (i, 0, 0))
    u_spec = pl.BlockSpec((SUPER, n_rows, S5_ROW), lambda i, j: (0, i * tiles + j, 0))
    return pl.pallas_call(
        _inproj_body,
        grid=(n, tiles),
        in_specs=[tok(D_MODEL), _const_spec((1, D_MODEL)), _const_spec((D_MODEL, D_IN))],
        out_specs=[u_spec, tok(3 * D_ATTN), tok(2 * D_MODEL), keep, keep],
        out_shape=[
            jax.ShapeDtypeStruct((SUPER, n * tiles * n_rows, S5_ROW), BF16),
            jax.ShapeDtypeStruct((n, t, 3 * D_ATTN), BF16),
            jax.ShapeDtypeStruct((n, t, 2 * D_MODEL), BF16),
            jax.ShapeDtypeStruct((n, tm, D_ATTN), F32),
            jax.ShapeDtypeStruct((n, tm, D_ATTN), F32),
        ],
        scratch_shapes=[pltpu.VMEM((SUPER, tm, LANES), F32)],
        compiler_params=_params("arbitrary", "arbitrary"),
        name="inproj",
    )(x, g_mix, w_in_b)


def _s5_weights(lam_re, lam_im, log_dt, b_re, b_im, c_re, c_im, d_skip):
    L = S5_CHUNK
    hi = lax.Precision.HIGHEST
    f = lambda a: a.astype(F32)
    cmul = lambda ar, ai, br, bi: (ar * br - ai * bi, ar * bi + ai * br)
    a_re, a_im = f(lam_re), f(lam_im)
    dt = jnp.exp(f(log_dt))[:, None]
    tau = jnp.arange(L + 1, dtype=F32)[None, :, None]
    mag = jnp.exp((a_re * dt)[:, None, :] * tau)
    ang = (a_im * dt)[:, None, :] * tau
    pw_re, pw_im = mag * jnp.cos(ang), mag * jnp.sin(ang)
    n_re, n_im = pw_re[:, 1] - 1.0, pw_im[:, 1]
    den = a_re * a_re + a_im * a_im
    f_re, f_im = (n_re * a_re + n_im * a_im) / den, (n_im * a_re - n_re * a_im) / den
    bb_re, bb_im = cmul(f_re[..., None], f_im[..., None], f(b_re), f(b_im))
    cp_re, cp_im = cmul(f(c_re)[:, :, None, :], f(c_im)[:, :, None, :], pw_re[:, None], pw_im[:, None])
    kt = (jnp.einsum('gatp,gph->ghta', cp_re[:, :, :L], bb_re, precision=hi)
          - jnp.einsum('gatp,gph->ghta', cp_im[:, :, :L], bb_im, precision=hi))
    kt = kt.at[:, :, 0, :].add(f(d_skip)[:, :, None] * jnp.eye(SSM_GROUP, dtype=F32)[None])
    rev = L - 1 - np.arange(L)
    bt_re, bt_im = jnp.swapaxes(bb_re, 1, 2)[:, None], jnp.swapaxes(bb_im, 1, 2)[:, None]
    wst_re, wst_im = cmul(pw_re[:, rev][:, :, None, :], pw_im[:, rev][:, :, None, :], bt_re, bt_im)

    sg = lambda a: a.reshape((SUPER, SUPER_GROUPS) + a.shape[1:])
    kc = jnp.transpose(sg(kt), (0, 2, 3, 1, 4)).reshape(SUPER, SSM_GROUP, S5_ROW)
    even = (np.arange(SUPER_GROUPS) % 2 == 0)[None, None, :, None, None]

    def in_form(w):
        w = jnp.transpose(sg(w), (0, 2, 1, 3, 4))
        w = jnp.concatenate([jnp.where(even, w, 0.0), jnp.where(even, 0.0, w)], axis=-1)
        return w.reshape(SUPER, S5_ROW, 2 * SSM_STATE)

    def out_form(c):
        return jnp.transpose(sg(c), (0, 4, 3, 1, 2)).reshape(SUPER, SSM_STATE, S5_ROW)

    wc = jnp.stack([in_form(wst_re), in_form(wst_im)], axis=1)
    ic = jnp.stack([out_form(cp_re[:, :, 1:L + 1]), out_form(-cp_im[:, :, 1:L + 1])], axis=1)
    half = SUPER_STATE // 2
    lam = jnp.concatenate([pw_re[:, L].reshape(SUPER, half), pw_im[:, L].reshape(SUPER, half)], axis=-1)
    return kc.astype(BF16), wc.astype(BF16), ic.astype(BF16), lam[:, None, :]


def _s5_body(u_ref, kc_ref, wc_ref, ic_ref, lam_ref, s0_ref, y_ref, sfin_ref, toep, wst, wint, x_scr, prev_scr,
             *, ns, nc):
    half = SUPER_STATE // 2

    @pl.when(pl.program_id(1) == 0)
    def _():
        zero = jnp.zeros((), BF16)
        group_of = lambda shape, dim: (lax.broadcasted_iota(jnp.int32, shape, dim) % LANES) // SSM_GROUP
        col_group = group_of((SSM_GROUP, S5_ROW), 1)
        kc = kc_ref[0]
        bblk = jnp.concatenate([jnp.where(col_group == g, kc, zero) for g in range(SUPER_GROUPS)], axis=0)
        for j in range(S5_CHUNK):
            rows = slice(j * LANES, (j + 1) * LANES)
            if j:
                toep[rows, 0:j * LANES] = jnp.zeros((LANES, j * LANES), BF16)
            toep[rows, j * LANES:S5_ROW] = bblk[:, 0:S5_ROW - j * LANES]
        row_pair = group_of((S5_ROW, LANES), 0) // 2
        for part in range(2):
            w = wc_ref[0, part]
            for k in range(SUPER_GROUPS // 2):
                c0 = part * half + k * LANES
                wst[:, c0:c0 + LANES] = jnp.where(row_pair == k, w, zero)
        col_group = group_of((SSM_STATE, S5_ROW), 1)
        for part in range(2):
            c = ic_ref[0, part]
            for g in range(SUPER_GROUPS):
                r0 = part * half + g * SSM_STATE
                wint[r0:r0 + SSM_STATE, :] = jnp.where(col_group == g, c, zero)

    u = u_ref[0]
    x = _dot(u, wst[...])
    nblk = SUPER_STATE // LANES
    hblk = nblk // 2
    for k in range(nblk):
        x_scr[k] = x[:, k * LANES:(k + 1) * LANES]
    lam = [lam_ref[0, :, k * LANES:(k + 1) * LANES] for k in range(nblk)]

    def step(c, s):
        rows = pl.ds(c, ns, stride=nc)
        nxt = []
        for k in range(nblk):
            prev_scr[k, rows, :] = s[k]
        for k in range(hblk):
            l_re, l_im, s_re, s_im = lam[k], lam[hblk + k], s[k], s[hblk + k]
            nxt.append((l_re * s_re - l_im * s_im + x_scr[k, rows, :],
                        l_re * s_im + l_im * s_re + x_scr[hblk + k, rows, :]))
        return tuple(p[0] for p in nxt) + tuple(p[1] for p in nxt)

    s0 = s0_ref[0, 0]
    s_fin = lax.fori_loop(0, nc, step, tuple(s0[:, k * LANES:(k + 1) * LANES] for k in range(nblk)))
    for k in range(nblk):
        sfin_ref[0, 0, :, k * LANES:(k + 1) * LANES] = s_fin[k]

    prev = jnp.concatenate([prev_scr[k] for k in range(nblk)], axis=-1).astype(BF16)
    blk = 2 * LANES
    for tb in range(S5_ROW // blk):
        cols = slice(tb * blk, (tb + 1) * blk)
        depth = (tb + 1) * blk
        y = _dot(u[:, 0:depth], toep[0:depth, cols]) + _dot(prev, wint[:, cols])
        y_ref[0, :, cols] = _gelu(y).astype(BF16)


def _s5(u2, weights, s0, ns, nc):
    kc, wc, ic, lam = weights
    tiles = s0.shape[1]
    r = ns * nc
    per_sg = lambda *dims: pl.BlockSpec((1,) + dims, lambda s, i: (s,) + (0,) * len(dims))
    rows = pl.BlockSpec((1, r, S5_ROW), lambda s, i: (s, i, 0))
    state = pl.BlockSpec((1, 1, ns, SUPER_STATE), lambda s, i: (s, i, 0, 0))
    return pl.pallas_call(
        functools.partial(_s5_body, ns=ns, nc=nc),
        grid=(SUPER, tiles),
        in_specs=[rows, per_sg(SSM_GROUP, S5_ROW), per_sg(2, S5_ROW, LANES), per_sg(2, SSM_STATE, S5_ROW),
                  per_sg(1, SUPER_STATE), state],
        out_specs=[rows, state],
        out_shape=[jax.ShapeDtypeStruct((SUPER, tiles * r, S5_ROW), BF16),
                   jax.ShapeDtypeStruct((SUPER, tiles, ns, SUPER_STATE), F32)],
        scratch_shapes=[pltpu.VMEM((S5_ROW, S5_ROW), BF16), pltpu.VMEM((S5_ROW, SUPER_STATE), BF16),
                        pltpu.VMEM((SUPER_STATE, S5_ROW), BF16),
                        pltpu.VMEM((SUPER_STATE // LANES, r, LANES), F32),
                        pltpu.VMEM((SUPER_STATE // LANES, r, LANES), F32)],
        compiler_params=_params("arbitrary", "arbitrary"),
        name="s5",
    )(u2, kc, wc, ic, lam, s0)


def _state_to_super(s_re, s_im, tiles, ns):
    n = s_re.shape[0]
    half = SUPER_STATE // 2
    s = jnp.concatenate([s_re.astype(F32).reshape(n, SUPER, half), s_im.astype(F32).reshape(n, SUPER, half)], -1)
    s = jnp.pad(jnp.transpose(s, (1, 0, 2)), ((0, 0), (0, tiles * ns - n), (0, 0)))
    return s.reshape(SUPER, tiles, ns, SUPER_STATE)


def _state_from_super(s, n):
    half = SUPER_STATE // 2
    s = jnp.transpose(s.reshape(SUPER, -1, SUPER_STATE)[:, :n], (1, 0, 2))
    return (s[:, :, :half].reshape(n, N_GROUPS, SSM_STATE), s[:, :, half:].reshape(n, N_GROUPS, SSM_STATE))


def _attend(qc, kb, vb, bias_ref, valid):
    tq = qc.shape[0]
    left = lax.broadcasted_iota(jnp.int32, (tq, LANES), 1) < HEAD_DIM
    zero = jnp.zeros((), BF16)
    pairs = range(N_HEADS // 2)
    cols = [slice(hp * LANES, (hp + 1) * LANES) for hp in pairs]
    scores = []
    for hp in pairs:
        qp = qc[:, cols[hp]]
        q2 = jnp.concatenate([jnp.where(left, qp, zero), jnp.where(left, zero, qp)], axis=0)
        sc = lax.dot_general(q2, kb[:, cols[hp]], (((1,), (1,)), ((), ())), preferred_element_type=F32)
        sc = sc + bias_ref[hp]
        if valid is not None:
            sc = jnp.where(valid, sc, MASK_VALUE)
        scores.append(sc)
    probs, dens = [], []
    for sc in scores:
        e = jnp.exp(sc - jnp.max(sc, axis=-1, keepdims=True))
        dens.append(jnp.sum(e, axis=-1, keepdims=True))
        probs.append(e.astype(BF16))
    outs = []
    for hp in pairs:
        o2 = _dot(probs[hp], vb[:, cols[hp]]) / dens[hp]
        outs.append(jnp.where(left, o2[:tq], o2[tq:]))
    return jnp.concatenate(outs, axis=-1)


def _attn_prompt_body(q_ref, k_ref, v_ref, bias_ref, o_ref, kpad, vpad):
    t = q_ref.shape[1]
    zeros = jnp.zeros((ATTN_REACH, D_ATTN), BF16)
    kpad[0:ATTN_REACH, :] = zeros
    vpad[0:ATTN_REACH, :] = zeros
    kpad[ATTN_REACH:ATTN_REACH + t, :] = k_ref[0]
    vpad[ATTN_REACH:ATTN_REACH + t, :] = v_ref[0]
    col = lax.broadcasted_iota(jnp.int32, (2 * CHUNK, BAND), 1)

    def chunk(c, masked):
        r0 = pl.multiple_of(c * CHUNK, CHUNK)
        qc = q_ref[0, pl.ds(r0, CHUNK), :]
        kb = kpad[pl.ds(r0, BAND), :]
        vb = vpad[pl.ds(r0, BAND), :]
        valid = (col >= ATTN_REACH - r0) if masked else None
        o_ref[0, pl.ds(r0, CHUNK), :] = _attend(qc, kb, vb, bias_ref, valid).astype(BF16)

    n_chunks = t // CHUNK
    lax.fori_loop(0, min(LEFT_CHUNKS, n_chunks), lambda c, _: chunk(c, True), None, unroll=2)
    lax.fori_loop(LEFT_CHUNKS, n_chunks, lambda c, _: chunk(c, False), None, unroll=2)


def _attn_prompt(qkv, bias):
    n, t, _ = qkv.shape
    part = lambda j: pl.BlockSpec((1, t, D_ATTN), lambda i: (i, 0, j))
    return pl.pallas_call(
        _attn_prompt_body,
        grid=(n,),
        in_specs=[part(0), part(1), part(2), _const_spec((N_HEADS // 2, 2 * CHUNK, BAND))],
        out_specs=pl.BlockSpec((1, t, D_ATTN), lambda i: (i, 0, 0)),
        out_shape=jax.ShapeDtypeStruct((n, t, D_ATTN), BF16),
        scratch_shapes=[pltpu.VMEM((ATTN_REACH + t, D_ATTN), BF16), pltpu.VMEM((ATTN_REACH + t, D_ATTN), BF16)],
        compiler_params=_params("arbitrary"),
        name="attn_prompt",
    )(qkv, qkv, qkv, bias)


def _attn_sample_body(q_ref, k_ref, v_ref, bias_ref, o_ref):
    o_ref[0] = _attend(q_ref[0], k_ref[0], v_ref[0], bias_ref, None).astype(BF16)


def _attn_sample(q, kk, vv, bias):
    n, tq, _ = q.shape
    tk = kk.shape[1]
    return pl.pallas_call(
        _attn_sample_body,
        grid=(n,),
        in_specs=[pl.BlockSpec((1, tq, D_ATTN), lambda i: (i, 0, 0)),
                  pl.BlockSpec((1, tk, D_ATTN), lambda i: (i, 0, 0)),
                  pl.BlockSpec((1, tk, D_ATTN), lambda i: (i, 0, 0)),
                  _const_spec((N_HEADS // 2, 2 * tq, tk))],
        out_specs=pl.BlockSpec((1, tq, D_ATTN), lambda i: (i, 0, 0)),
        out_shape=jax.ShapeDtypeStruct((n, tq, D_ATTN), BF16),
        compiler_params=_params("arbitrary"),
        name="attn_sample",
    )(q, kk, vv, bias)


def _rel_bias(table, tq, tk):
    width = tq + tk - 1
    z = np.arange(width)
    idx = np.clip(ATTN_REACH - np.where(z < tk, z, z - width), -MAX_REL, MAX_REL) + MAX_REL
    ext = table.astype(F32)[:, idx]
    flat = jnp.tile(ext, (1, tq))[:, :tq * (width - 1)]
    bias = flat.reshape(N_HEADS, tq, width - 1)[:, :, :tk]
    return bias.reshape(N_HEADS // 2, 2 * tq, tk)


def _ffn_body(*refs, multi_seq):
    if multi_seq:
        (x_ref, ys_ref, att_ref, gate_ref, h1_ref, h2_ref, wglu, watt, wo, gffn, wup, cw, cb, wdown, gfin,
         y_ref, conv_ref, a_scr, ys_scr) = refs
    else:
        (x_ref, ys_ref, att_ref, gate_ref, hist_ref, wglu, watt, wo, gffn, wup, cw, cb, wdown, gfin,
         y_ref, conv_ref, a_scr, ys_scr, carry) = refs
    tm = x_ref.shape[1]
    n_rows = tm // S5_CHUNK
    for s in range(SUPER):
        for j in range(S5_CHUNK):
            ys_scr[s, pl.ds(j, n_rows, stride=S5_CHUNK), :] = ys_ref[s, :, j * LANES:(j + 1) * LANES].astype(F32)
    ys = jnp.concatenate([ys_scr[s] for s in range(SUPER)], axis=-1).astype(BF16)
    gl = _dot(ys, wglu[...])
    br_ssm = gl[:, :D_MODEL] * _sigmoid(gl[:, D_MODEL:])
    br_att = _dot(att_ref[0], watt[...])
    mix = (_sigmoid(gate_ref[0, :, :D_MODEL].astype(F32)) * br_ssm
           + _sigmoid(gate_ref[0, :, D_MODEL:].astype(F32)) * br_att)
    h = x_ref[0] + _dot(mix.astype(BF16), wo[...])
    hn = _rms(h, gffn[...]).astype(BF16)

    if multi_seq:
        pos = lax.broadcasted_iota(jnp.int32, (tm, 1), 0) % multi_seq
    else:
        @pl.when(pl.program_id(1) == 0)
        def _():
            carry[...] = hist_ref[0]

    acc = jnp.zeros((tm, D_MODEL), F32)
    for lo, hi in FF_CHUNKS:
        a = _dot(hn, wup[:, lo:hi])
        b = _dot(hn, wup[:, D_FF + lo:D_FF + hi])
        a_scr[8:8 + tm, 0:hi - lo] = a
        if multi_seq:
            a_scr[6:8, 0:hi - lo] = jnp.zeros((2, hi - lo), F32)
            conv_ref[0, :, lo:hi] = a
        else:
            a_scr[6:8, 0:hi - lo] = carry[:, lo:hi]
            carry[:, lo:hi] = a[tm - 2:tm, :]
        a1 = a_scr[7:7 + tm, 0:hi - lo]
        a2 = a_scr[6:6 + tm, 0:hi - lo]
        if multi_seq:
            a1 = jnp.where(pos < 1, h1_ref[:, lo:hi], a1)
            a2 = jnp.where(pos < 2, h2_ref[:, lo:hi], a2)
        c = cb[:, lo:hi] + cw[0:1, lo:hi] * a2 + cw[1:2, lo:hi] * a1 + cw[2:3, lo:hi] * a
        act = (_gelu(c) * b).astype(BF16)
        acc = acc + _dot(act, wdown[lo:hi, :])
    y_ref[0] = _rms(h + acc, gfin[...])

    if not multi_seq:
        @pl.when(pl.program_id(1) == pl.num_programs(1) - 1)
        def _():
            conv_ref[0] = carry[...]


def _ffn_weight_specs():
    return [_const_spec((D_SSM, 2 * D_MODEL)), _const_spec((D_ATTN, D_MODEL)), _const_spec((D_MODEL, D_MODEL)),
            _const_spec((1, D_MODEL)), _const_spec((D_MODEL, 2 * D_FF)), _const_spec((CONV_W, D_FF)),
            _const_spec((1, D_FF)), _const_spec((D_FF, D_MODEL)), _const_spec((1, D_MODEL))]


def _ffn_prompt(x, ys, att, gates, hist, wts, tm):
    n, t, _ = x.shape
    tiles = t // tm
    tok = lambda w: pl.BlockSpec((1, tm, w), lambda i, j: (i, j, 0))
    ys_spec = pl.BlockSpec((SUPER, tm // S5_CHUNK, S5_ROW), lambda i, j: (0, i * tiles + j, 0))
    seq = pl.BlockSpec((1, CONV_W - 1, D_FF), lambda i, j: (i, 0, 0))
    fc = max(hi - lo for lo, hi in FF_CHUNKS)
    return pl.pallas_call(
        functools.partial(_ffn_body, multi_seq=0),
        grid=(n, tiles),
        in_specs=[tok(D_MODEL), ys_spec, tok(D_ATTN), tok(2 * D_MODEL), seq] + _ffn_weight_specs(),
        out_specs=[tok(D_MODEL), seq],
        out_shape=[jax.ShapeDtypeStruct((n, t, D_MODEL), F32),
                   jax.ShapeDtypeStruct((n, CONV_W - 1, D_FF), F32)],
        scratch_shapes=[pltpu.VMEM((tm + 8, fc), F32), pltpu.VMEM((SUPER, tm, LANES), F32),
                        pltpu.VMEM((CONV_W - 1, D_FF), F32)],
        compiler_params=_params("arbitrary", "arbitrary"),
        name="merge_ffn_prompt",
    )(x, ys, att, gates, hist, *wts)


def _ffn_sample(x, ys, att, gates, h1, h2, wts, seq_len):
    rows = x.shape[1]
    tok = lambda w: pl.BlockSpec((1, rows, w), lambda i: (0, 0, 0))
    ys_spec = pl.BlockSpec((SUPER, rows // S5_CHUNK, S5_ROW), lambda i: (0, 0, 0))
    full = pl.BlockSpec((rows, D_FF), lambda i: (0, 0))
    fc = max(hi - lo for lo, hi in FF_CHUNKS)
    return pl.pallas_call(
        functools.partial(_ffn_body, multi_seq=seq_len),
        grid=(1,),
        in_specs=[tok(D_MODEL), ys_spec, tok(D_ATTN), tok(2 * D_MODEL), full, full] + _ffn_weight_specs(),
        out_specs=[tok(D_MODEL), tok(D_FF)],
        out_shape=[jax.ShapeDtypeStruct((1, rows, D_MODEL), F32),
                   jax.ShapeDtypeStruct((1, rows, D_FF), F32)],
        scratch_shapes=[pltpu.VMEM((rows + 8, fc), F32), pltpu.VMEM((SUPER, rows, LANES), F32)],
        compiler_params=_params("arbitrary"),
        name="merge_ffn_sample",
    )(x, ys, att, gates, h1, h2, *wts)


PROMPT_TOKEN_TILE = 512
FFN_TOKEN_TILE = 512
SAMPLE_STREAM_PAD = 16


def kernel(x_prompt, x_sample, state_ssm_re, state_ssm_im, cache_attn_k, cache_attn_v, cache_conv,
           g_mix, w_in, ssm_lambda_re, ssm_lambda_im, ssm_log_dt, ssm_b_re, ssm_b_im,
           ssm_c_re, ssm_c_im, ssm_d, w_ssm_glu, attn_rel_bias, w_attn_up, w_o,
           g_ffn, w_up, conv_w, conv_b, w_down, g_final):
    depth = w_in.shape[0]
    nb, seq, _ = x_prompt.shape
    db, dseq, _ = x_sample.shape
    assert depth == 1 and seq % PROMPT_TOKEN_TILE == 0 and min(ATTN_REACH, seq) == PROMPT_TOKEN_TILE
    assert seq % S5_CHUNK == 0 and dseq == S5_CHUNK and nb % S5_PROMPT_SEQS == 0 and db <= SAMPLE_STREAM_PAD
    l = 0

    col_scale = jnp.ones((D_IN,), F32).at[D_SSM:D_SSM + D_ATTN].set(HEAD_DIM ** -0.5)
    w_in_b = (w_in[l] * col_scale[None, :]).astype(BF16)
    wts = (w_ssm_glu[l].astype(BF16), w_attn_up[l].astype(BF16), w_o[l].astype(BF16), g_ffn[l][None, :],
           w_up[l].astype(BF16), conv_w[l], conv_b[l][None, :], w_down[l].astype(BF16), g_final[None, :])
    g_mix_l = g_mix[l][None, :]
    w_s5 = _s5_weights(ssm_lambda_re[l], ssm_lambda_im[l], ssm_log_dt[l], ssm_b_re[l], ssm_b_im[l],
                       ssm_c_re[l], ssm_c_im[l], ssm_d[l])
    table = attn_rel_bias[l]
    bias_p = _rel_bias(table, CHUNK, BAND)
    w_cache = cache_attn_k.shape[2]
    assert w_cache == ATTN_REACH
    bias_s = _rel_bias(table, dseq, w_cache + dseq)

    u, qkv, gates, k32, v32 = _inproj(x_prompt, g_mix_l, w_in_b, PROMPT_TOKEN_TILE)
    n_chunks = seq // S5_CHUNK
    tiles = nb // S5_PROMPT_SEQS
    zero_state = jnp.zeros((SUPER, tiles, S5_PROMPT_SEQS, SUPER_STATE), F32)
    ys, p_state = _s5(u, w_s5, zero_state, S5_PROMPT_SEQS, n_chunks)
    p_re, p_im = _state_from_super(p_state, nb)
    att = _attn_prompt(qkv, bias_p)
    zero_conv = jnp.zeros((nb, CONV_W - 1, D_FF), F32)
    y_prompt, p_conv = _ffn_prompt(x_prompt, ys, att, gates, zero_conv, wts, FFN_TOKEN_TILE)

    rows = db * dseq
    xs = x_sample.reshape(1, rows, D_MODEL)
    us, qkvs, gates_s, ks32, vs32 = _inproj(xs, g_mix_l, w_in_b, rows)
    us2 = jnp.pad(us, ((0, 0), (0, SAMPLE_STREAM_PAD - db), (0, 0)))
    s0 = _state_to_super(state_ssm_re[l], state_ssm_im[l], 1, SAMPLE_STREAM_PAD)
    yss, s_state = _s5(us2, w_s5, s0, SAMPLE_STREAM_PAD, 1)
    s_re, s_im = _state_from_super(s_state, db)
    yss = yss[:, :db]
    qkvs3 = qkvs.reshape(db, dseq, 3 * D_ATTN)
    kk = jnp.concatenate([cache_attn_k[l].reshape(db, w_cache, D_ATTN).astype(BF16),
                          qkvs3[:, :, D_ATTN:2 * D_ATTN]], axis=1)
    vv = jnp.concatenate([cache_attn_v[l].reshape(db, w_cache, D_ATTN).astype(BF16),
                          qkvs3[:, :, 2 * D_ATTN:]], axis=1)
    att_s = _attn_sample(qkvs3[:, :, :D_ATTN], kk, vv, bias_s)
    hist = cache_conv[l].astype(F32)
    pad_rows = ((0, 0), (0, dseq - 1), (0, 0))
    h1 = jnp.pad(hist[:, 1:2], pad_rows).reshape(rows, D_FF)
    h2 = jnp.pad(hist, ((0, 0), (0, dseq - 2), (0, 0))).reshape(rows, D_FF)
    y_s, a_s = _ffn_sample(xs, yss, att_s.reshape(1, rows, D_ATTN), gates_s, h1, h2, wts, dseq)
    s_conv = a_s.reshape(db, dseq, D_FF)[:, dseq - (CONV_W - 1):]

    heads = lambda a, n, t: a.reshape(1, n, t, N_HEADS, HEAD_DIM)
    return (y_prompt, y_s.reshape(db, dseq, D_MODEL),
            p_re[None], p_im[None],
            heads(k32, nb, PROMPT_TOKEN_TILE), heads(v32, nb, PROMPT_TOKEN_TILE), p_conv[None],
            s_re[None], s_im[None],
            heads(ks32, db, dseq), heads(vs32, db, dseq), s_conv[None])
```

```python
import functools
import math

import jax
import jax.numpy as jnp
import numpy as np
from jax import lax
from jax.experimental import pallas as pl
from jax.experimental.pallas import tpu as pltpu

F32 = jnp.float32
BF16 = jnp.bfloat16

D_MODEL = 1024
CHUNK = 64
LEFT_CHUNKS = 8
ATTN_REACH = LEFT_CHUNKS * CHUNK
BAND = ATTN_REACH + CHUNK
D_SSM = D_MODEL // 2
D_ATTN = D_MODEL // 2
HEAD_DIM = 64
N_HEADS = D_ATTN // HEAD_DIM
MAX_REL = 128
SSM_GROUP = 16
N_GROUPS = D_SSM // SSM_GROUP
SSM_STATE = 64
D_FF = 2816
CONV_W = 3
D_IN = D_SSM + 3 * D_ATTN + 2 * D_MODEL
RMS_EPS = 1e-6
MASK_VALUE = -1e30

V7X_VMEM_LIMIT_BYTES = 56 * 1024 * 1024
LANES = 128
FF_CHUNKS = ((0, 2816),)


def _params(*sem):
    return pltpu.CompilerParams(dimension_semantics=sem, vmem_limit_bytes=V7X_VMEM_LIMIT_BYTES)


def _const_spec(shape):
    zeros = (0,) * len(shape)
    return pl.BlockSpec(shape, lambda *_: zeros, pipeline_mode=pl.Buffered(1))


def _rms(x, g):
    return x * lax.rsqrt(jnp.mean(x * x, axis=-1, keepdims=True) + RMS_EPS) * g


def _gelu(x):
    k = math.sqrt(2.0 / math.pi)
    return (0.5 * x) * (1.0 + jnp.tanh(x * (k + (k * 0.044715) * (x * x))))


def _sigmoid(x):
    return 1.0 / (1.0 + jnp.exp(-x))


def _dot(a, b):
    return jnp.dot(a, b, preferred_element_type=F32)


S5_CHUNK = 16
SUPER = D_SSM // LANES
SUPER_GROUPS = N_GROUPS // SUPER
SUPER_STATE = 2 * SUPER_GROUPS * SSM_STATE
S5_ROW = S5_CHUNK * LANES
S5_PROMPT_SEQS = 4


def _inproj_body(x_ref, g_ref, w_ref, u_ref, qkv_ref, gate_ref, k32_ref, v32_ref, u_scr):
    xn = _rms(x_ref[0], g_ref[...]).astype(BF16)

    def seg(lo, hi):
        return _dot(xn, w_ref[:, lo:hi])

    u = seg(0, D_SSM)
    n_rows = u_scr.shape[1] // S5_CHUNK
    for s in range(SUPER):
        u_scr[s] = u[:, s * LANES:(s + 1) * LANES]
    for s in range(SUPER):
        for j in range(S5_CHUNK):
            frames = u_scr[s, pl.ds(j, n_rows, stride=S5_CHUNK), :]
            u_ref[s, :, j * LANES:(j + 1) * LANES] = frames.astype(BF16)
    q0 = D_SSM
    qkv_ref[0, :, 0:D_ATTN] = seg(q0, q0 + D_ATTN).astype(BF16)
    k = seg(q0 + D_ATTN, q0 + 2 * D_ATTN)
    v = seg(q0 + 2 * D_ATTN, q0 + 3 * D_ATTN)
    qkv_ref[0, :, D_ATTN:2 * D_ATTN] = k.astype(BF16)
    qkv_ref[0, :, 2 * D_ATTN:3 * D_ATTN] = v.astype(BF16)
    g0 = q0 + 3 * D_ATTN
    gate_ref[0, :, 0:D_MODEL] = seg(g0, g0 + D_MODEL).astype(BF16)
    gate_ref[0, :, D_MODEL:2 * D_MODEL] = seg(g0 + D_MODEL, g0 + 2 * D_MODEL).astype(BF16)

    @pl.when(pl.program_id(1) == pl.num_programs(1) - 1)
    def _():
        k32_ref[0] = k
        v32_ref[0] = v


def _inproj(x, g_mix, w_in_b, tm):
    n, t, _ = x.shape
    tiles = t // tm
    n_rows = tm // S5_CHUNK
    tok = lambda w: pl.BlockSpec((1, tm, w), lambda i, j: (i, j, 0))
    keep = pl.BlockSpec((1, tm, D_ATTN), lambda i, j: (i, 0, 0))
    u_spec = pl.BlockSpec((SUPER, n_rows, S5_ROW), lambda i, j: (0, i * tiles + j, 0))
    return pl.pallas_call(
        _inproj_body,
        grid=(n, tiles),
        in_specs=[tok(D_MODEL), _const_spec((1, D_MODEL)), _const_spec((D_MODEL, D_IN))],
        out_specs=[u_spec, tok(3 * D_ATTN), tok(2 * D_MODEL), keep, keep],
        out_shape=[
            jax.ShapeDtypeStruct((SUPER, n * tiles * n_rows, S5_ROW), BF16),
            jax.ShapeDtypeStruct((n, t, 3 * D_ATTN), BF16),
            jax.ShapeDtypeStruct((n, t, 2 * D_MODEL), BF16),
            jax.ShapeDtypeStruct((n, tm, D_ATTN), F32),
            jax.ShapeDtypeStruct((n, tm, D_ATTN), F32),
        ],
        scratch_shapes=[pltpu.VMEM((SUPER, tm, LANES), F32)],
        compiler_params=_params("arbitrary", "arbitrary"),
        name="inproj",
    )(x, g_mix, w_in_b)


def _s5_weights(lam_re, lam_im, log_dt, b_re, b_im, c_re, c_im, d_skip):
    L = S5_CHUNK
    hi = lax.Precision.HIGHEST
    f = lambda a: a.astype(F32)
    cmul = lambda ar, ai, br, bi: (ar * br - ai * bi, ar * bi + ai * br)
    a_re, a_im = f(lam_re), f(lam_im)
    dt = jnp.exp(f(log_dt))[:, None]
    tau = jnp.arange(L + 1, dtype=F32)[None, :, None]
    mag = jnp.exp((a_re * dt)[:, None, :] * tau)
    ang = (a_im * dt)[:, None, :] * tau
    pw_re, pw_im = mag * jnp.cos(ang), mag * jnp.sin(ang)
    n_re, n_im = pw_re[:, 1] - 1.0, pw_im[:, 1]
    den = a_re * a_re + a_im * a_im
    f_re, f_im = (n_re * a_re + n_im * a_im) / den, (n_im * a_re - n_re * a_im) / den
    bb_re, bb_im = cmul(f_re[..., None], f_im[..., None], f(b_re), f(b_im))
    cp_re, cp_im = cmul(f(c_re)[:, :, None, :], f(c_im)[:, :, None, :], pw_re[:, None], pw_im[:, None])
    kt = (jnp.einsum('gatp,gph->ghta', cp_re[:, :, :L], bb_re, precision=hi)
          - jnp.einsum('gatp,gph->ghta', cp_im[:, :, :L], bb_im, precision=hi))
    kt = kt.at[:, :, 0, :].add(f(d_skip)[:, :, None] * jnp.eye(SSM_GROUP, dtype=F32)[None])
    rev = L - 1 - np.arange(L)
    bt_re, bt_im = jnp.swapaxes(bb_re, 1, 2)[:, None], jnp.swapaxes(bb_im, 1, 2)[:, None]
    wst_re, wst_im = cmul(pw_re[:, rev][:, :, None, :], pw_im[:, rev][:, :, None, :], bt_re, bt_im)

    sg = lambda a: a.reshape((SUPER, SUPER_GROUPS) + a.shape[1:])
    kc = jnp.transpose(sg(kt), (0, 2, 3, 1, 4)).reshape(SUPER, SSM_GROUP, S5_ROW)
    even = (np.arange(SUPER_GROUPS) % 2 == 0)[None, None, :, None, None]

    def in_form(w):
        w = jnp.transpose(sg(w), (0, 2, 1, 3, 4))
        w = jnp.concatenate([jnp.where(even, w, 0.0), jnp.where(even, 0.0, w)], axis=-1)
        return w.reshape(SUPER, S5_ROW, 2 * SSM_STATE)

    def out_form(c):
        return jnp.transpose(sg(c), (0, 4, 3, 1, 2)).reshape(SUPER, SSM_STATE, S5_ROW)

    wc = jnp.stack([in_form(wst_re), in_form(wst_im)], axis=1)
    ic = jnp.stack([out_form(cp_re[:, :, 1:L + 1]), out_form(-cp_im[:, :, 1:L + 1])], axis=1)
    half = SUPER_STATE // 2
    lam = jnp.concatenate([pw_re[:, L].reshape(SUPER, half), pw_im[:, L].reshape(SUPER, half)], axis=-1)
    return kc.astype(BF16), wc.astype(BF16), ic.astype(BF16), lam[:, None, :]


def _s5_body(u_ref, kc_ref, wc_ref, ic_ref, lam_ref, s0_ref, y_ref, sfin_ref, toep, wst, wint, x_scr, prev_scr,
             *, ns, nc):
    half = SUPER_STATE // 2

    @pl.when(pl.program_id(1) == 0)
    def _():
        zero = jnp.zeros((), BF16)
        group_of = lambda shape, dim: (lax.broadcasted_iota(jnp.int32, shape, dim) % LANES) // SSM_GROUP
        col_group = group_of((SSM_GROUP, S5_ROW), 1)
        kc = kc_ref[0]
        bblk = jnp.concatenate([jnp.where(col_group == g, kc, zero) for g in range(SUPER_GROUPS)], axis=0)
        for j in range(S5_CHUNK):
            rows = slice(j * LANES, (j + 1) * LANES)
            if j:
                toep[rows, 0:j * LANES] = jnp.zeros((LANES, j * LANES), BF16)
            toep[rows, j * LANES:S5_ROW] = bblk[:, 0:S5_ROW - j * LANES]
        row_pair = group_of((S5_ROW, LANES), 0) // 2
        for part in range(2):
            w = wc_ref[0, part]
            for k in range(SUPER_GROUPS // 2):
                c0 = part * half + k * LANES
                wst[:, c0:c0 + LANES] = jnp.where(row_pair == k, w, zero)
        col_group = group_of((SSM_STATE, S5_ROW), 1)
        for part in range(2):
            c = ic_ref[0, part]
            for g in range(SUPER_GROUPS):
                r0 = part * half + g * SSM_STATE
                wint[r0:r0 + SSM_STATE, :] = jnp.where(col_group == g, c, zero)

    u = u_ref[0]
    x = _dot(u, wst[...])
    nblk = SUPER_STATE // LANES
    hblk = nblk // 2
    for k in range(nblk):
        x_scr[k] = x[:, k * LANES:(k + 1) * LANES]
    lam = [lam_ref[0, :, k * LANES:(k + 1) * LANES] for k in range(nblk)]

    def step(c, s):
        rows = pl.ds(c, ns, stride=nc)
        nxt = []
        for k in range(nblk):
            prev_scr[k, rows, :] = s[k]
        for k in range(hblk):
            l_re, l_im, s_re, s_im = lam[k], lam[hblk + k], s[k], s[hblk + k]
            nxt.append((l_re * s_re - l_im * s_im + x_scr[k, rows, :],
                        l_re * s_im + l_im * s_re + x_scr[hblk + k, rows, :]))
        return tuple(p[0] for p in nxt) + tuple(p[1] for p in nxt)

    s0 = s0_ref[0, 0]
    s_fin = lax.fori_loop(0, nc, step, tuple(s0[:, k * LANES:(k + 1) * LANES] for k in range(nblk)),
                          unroll=min(nc, 4))
    for k in range(nblk):
        sfin_ref[0, 0, :, k * LANES:(k + 1) * LANES] = s_fin[k]

    prev = jnp.concatenate([prev_scr[k] for k in range(nblk)], axis=-1).astype(BF16)
    blk = 2 * LANES
    for tb in range(S5_ROW // blk):
        cols = slice(tb * blk, (tb + 1) * blk)
        depth = (tb + 1) * blk
        y = _dot(u[:, 0:depth], toep[0:depth, cols]) + _dot(prev, wint[:, cols])
        y_ref[0, :, cols] = _gelu(y).astype(BF16)


def _s5(u2, weights, s0, ns, nc):
    kc, wc, ic, lam = weights
    tiles = s0.shape[1]
    r = ns * nc
    per_sg = lambda *dims: pl.BlockSpec((1,) + dims, lambda s, i: (s,) + (0,) * len(dims))
    rows = pl.BlockSpec((1, r, S5_ROW), lambda s, i: (s, i, 0))
    state = pl.BlockSpec((1, 1, ns, SUPER_STATE), lambda s, i: (s, i, 0, 0))
    return pl.pallas_call(
        functools.partial(_s5_body, ns=ns, nc=nc),
        grid=(SUPER, tiles),
        in_specs=[rows, per_sg(SSM_GROUP, S5_ROW), per_sg(2, S5_ROW, LANES), per_sg(2, SSM_STATE, S5_ROW),
                  per_sg(1, SUPER_STATE), state],
        out_specs=[rows, state],
        out_shape=[jax.ShapeDtypeStruct((SUPER, tiles * r, S5_ROW), BF16),
                   jax.ShapeDtypeStruct((SUPER, tiles, ns, SUPER_STATE), F32)],
        scratch_shapes=[pltpu.VMEM((S5_ROW, S5_ROW), BF16), pltpu.VMEM((S5_ROW, SUPER_STATE), BF16),
                        pltpu.VMEM((SUPER_STATE, S5_ROW), BF16),
                        pltpu.VMEM((SUPER_STATE // LANES, r, LANES), F32),
                        pltpu.VMEM((SUPER_STATE // LANES, r, LANES), F32)],
        compiler_params=_params("arbitrary", "arbitrary"),
        name="s5",
    )(u2, kc, wc, ic, lam, s0)


def _state_to_super(s_re, s_im, tiles, ns):
    n = s_re.shape[0]
    half = SUPER_STATE // 2
    s = jnp.concatenate([s_re.astype(F32).reshape(n, SUPER, half), s_im.astype(F32).reshape(n, SUPER, half)], -1)
    s = jnp.pad(jnp.transpose(s, (1, 0, 2)), ((0, 0), (0, tiles * ns - n), (0, 0)))
    return s.reshape(SUPER, tiles, ns, SUPER_STATE)


def _state_from_super(s, n):
    half = SUPER_STATE // 2
    s = jnp.transpose(s.reshape(SUPER, -1, SUPER_STATE)[:, :n], (1, 0, 2))
    return (s[:, :, :half].reshape(n, N_GROUPS, SSM_STATE), s[:, :, half:].reshape(n, N_GROUPS, SSM_STATE))


def _attend(qc, kb, vb, bias_ref, valid):
    tq = qc.shape[0]
    left = lax.broadcasted_iota(jnp.int32, (tq, LANES), 1) < HEAD_DIM
    zero = jnp.zeros((), BF16)
    pairs = range(N_HEADS // 2)
    cols = [slice(hp * LANES, (hp + 1) * LANES) for hp in pairs]
    scores = []
    for hp in pairs:
        qp = qc[:, cols[hp]]
        q2 = jnp.concatenate([jnp.where(left, qp, zero), jnp.where(left, zero, qp)], axis=0)
        sc = lax.dot_general(q2, kb[:, cols[hp]], (((1,), (1,)), ((), ())), preferred_element_type=F32)
        sc = sc + bias_ref[hp]
        if valid is not None:
            sc = jnp.where(valid, sc, MASK_VALUE)
        scores.append(sc)
    probs, dens = [], []
    for sc in scores:
        e = jnp.exp2(sc - jnp.max(sc, axis=-1, keepdims=True))
        dens.append(jnp.sum(e, axis=-1, keepdims=True))
        probs.append(e.astype(BF16))
    outs = []
    for hp in pairs:
        o2 = _dot(probs[hp], vb[:, cols[hp]]) / dens[hp]
        outs.append(jnp.where(left, o2[:tq], o2[tq:]))
    return jnp.concatenate(outs, axis=-1)


def _attn_prompt_body(q_ref, k_ref, v_ref, bias_ref, o_ref, kpad, vpad):
    t = q_ref.shape[1]
    zeros = jnp.zeros((ATTN_REACH, D_ATTN), BF16)
    kpad[0:ATTN_REACH, :] = zeros
    vpad[0:ATTN_REACH, :] = zeros
    kpad[ATTN_REACH:ATTN_REACH + t, :] = k_ref[0]
    vpad[ATTN_REACH:ATTN_REACH + t, :] = v_ref[0]
    col = lax.broadcasted_iota(jnp.int32, (2 * CHUNK, BAND), 1)

    def chunk(c, masked):
        r0 = pl.multiple_of(c * CHUNK, CHUNK)
        qc = q_ref[0, pl.ds(r0, CHUNK), :]
        kb = kpad[pl.ds(r0, BAND), :]
        vb = vpad[pl.ds(r0, BAND), :]
        valid = (col >= ATTN_REACH - r0) if masked else None
        o_ref[0, pl.ds(r0, CHUNK), :] = _attend(qc, kb, vb, bias_ref, valid).astype(BF16)

    n_chunks = t // CHUNK
    lax.fori_loop(0, min(LEFT_CHUNKS, n_chunks), lambda c, _: chunk(c, True), None, unroll=2)
    lax.fori_loop(LEFT_CHUNKS, n_chunks, lambda c, _: chunk(c, False), None, unroll=2)


def _attn_prompt(qkv, bias):
    n, t, _ = qkv.shape
    part = lambda j: pl.BlockSpec((1, t, D_ATTN), lambda i: (i, 0, j))
    return pl.pallas_call(
        _attn_prompt_body,
        grid=(n,),
        in_specs=[part(0), part(1), part(2), _const_spec((N_HEADS // 2, 2 * CHUNK, BAND))],
        out_specs=pl.BlockSpec((1, t, D_ATTN), lambda i: (i, 0, 0)),
        out_shape=jax.ShapeDtypeStruct((n, t, D_ATTN), BF16),
        scratch_shapes=[pltpu.VMEM((ATTN_REACH + t, D_ATTN), BF16), pltpu.VMEM((ATTN_REACH + t, D_ATTN), BF16)],
        compiler_params=_params("arbitrary"),
        name="attn_prompt",
    )(qkv, qkv, qkv, bias)


def _attn_sample_body(q_ref, k_ref, v_ref, bias_ref, o_ref):
    o_ref[0] = _attend(q_ref[0], k_ref[0], v_ref[0], bias_ref, None).astype(BF16)


def _attn_sample(q, kk, vv, bias):
    n, tq, _ = q.shape
    tk = kk.shape[1]
    return pl.pallas_call(
        _attn_sample_body,
        grid=(n,),
        in_specs=[pl.BlockSpec((1, tq, D_ATTN), lambda i: (i, 0, 0)),
                  pl.BlockSpec((1, tk, D_ATTN), lambda i: (i, 0, 0)),
                  pl.BlockSpec((1, tk, D_ATTN), lambda i: (i, 0, 0)),
                  _const_spec((N_HEADS // 2, 2 * tq, tk))],
        out_specs=pl.BlockSpec((1, tq, D_ATTN), lambda i: (i, 0, 0)),
        out_shape=jax.ShapeDtypeStruct((n, tq, D_ATTN), BF16),
        compiler_params=_params("arbitrary"),
        name="attn_sample",
    )(q, kk, vv, bias)


def _rel_bias(table, tq, tk):
    width = tq + tk - 1
    z = np.arange(width)
    idx = np.clip(ATTN_REACH - np.where(z < tk, z, z - width), -MAX_REL, MAX_REL) + MAX_REL
    ext = table.astype(F32)[:, idx]
    flat = jnp.tile(ext, (1, tq))[:, :tq * (width - 1)]
    bias = flat.reshape(N_HEADS, tq, width - 1)[:, :, :tk]
    return bias.reshape(N_HEADS // 2, 2 * tq, tk)


def _ffn_body(*refs, multi_seq):
    if multi_seq:
        (x_ref, ys_ref, att_ref, gate_ref, h1_ref, h2_ref, wglu, watt, wo, gffn, wup, cw, cb, wdown, gfin,
         y_ref, conv_ref, a_scr, ys_scr) = refs
    else:
        (x_ref, ys_ref, att_ref, gate_ref, hist_ref, wglu, watt, wo, gffn, wup, cw, cb, wdown, gfin,
         y_ref, conv_ref, a_scr, ys_scr, carry) = refs
    tm = x_ref.shape[1]
    n_rows = tm // S5_CHUNK
    for s in range(SUPER):
        for j in range(S5_CHUNK):
            ys_scr[s, pl.ds(j, n_rows, stride=S5_CHUNK), :] = ys_ref[s, :, j * LANES:(j + 1) * LANES].astype(F32)
    if not multi_seq:
        @pl.when(pl.program_id(1) == 0)
        def _():
            carry[...] = hist_ref[0]

    n_sub = a_scr.shape[0]
    sm = tm // n_sub
    for sub in range(n_sub):
        rows = slice(sub * sm, (sub + 1) * sm)
        ys = jnp.concatenate([ys_scr[s, rows, :] for s in range(SUPER)], axis=-1).astype(BF16)
        gl = _dot(ys, wglu[...])
        br_ssm = gl[:, :D_MODEL] * _sigmoid(gl[:, D_MODEL:])
        br_att = _dot(att_ref[0, rows, :], watt[...])
        mix = (_sigmoid(gate_ref[0, rows, :D_MODEL].astype(F32)) * br_ssm
               + _sigmoid(gate_ref[0, rows, D_MODEL:].astype(F32)) * br_att)
        h = x_ref[0, rows, :] + _dot(mix.astype(BF16), wo[...])
        hn = _rms(h, gffn[...]).astype(BF16)
        if multi_seq:
            pos = (lax.broadcasted_iota(jnp.int32, (sm, 1), 0) + sub * sm) % multi_seq

        acc = jnp.zeros((sm, D_MODEL), F32)
        for ci, (lo, hi) in enumerate(FF_CHUNKS):
            w = hi - lo
            scr = a_scr.at[sub, ci]
            a = _dot(hn, wup[:, lo:hi])
            b = _dot(hn, wup[:, D_FF + lo:D_FF + hi])
            scr[8:8 + sm, 0:w] = a
            if multi_seq:
                scr[6:8, 0:w] = jnp.zeros((2, w), F32)
                conv_ref[0, rows, lo:hi] = a
            else:
                scr[6:8, 0:w] = carry[:, lo:hi]
                carry[:, lo:hi] = a[sm - 2:sm, :]
            a1 = scr[7:7 + sm, 0:w]
            a2 = scr[6:6 + sm, 0:w]
            if multi_seq:
                a1 = jnp.where(pos < 1, h1_ref[rows, lo:hi], a1)
                a2 = jnp.where(pos < 2, h2_ref[rows, lo:hi], a2)
            c = cb[:, lo:hi] + cw[0:1, lo:hi] * a2 + cw[1:2, lo:hi] * a1 + cw[2:3, lo:hi] * a
            act = (_gelu(c) * b).astype(BF16)
            acc = acc + _dot(act, wdown[lo:hi, :])
        y_ref[0, rows, :] = _rms(h + acc, gfin[...])

    if not multi_seq:
        @pl.when(pl.program_id(1) == pl.num_programs(1) - 1)
        def _():
            conv_ref[0] = carry[...]


def _ffn_weight_specs():
    return [_const_spec((D_SSM, 2 * D_MODEL)), _const_spec((D_ATTN, D_MODEL)), _const_spec((D_MODEL, D_MODEL)),
            _const_spec((1, D_MODEL)), _const_spec((D_MODEL, 2 * D_FF)), _const_spec((CONV_W, D_FF)),
            _const_spec((1, D_FF)), _const_spec((D_FF, D_MODEL)), _const_spec((1, D_MODEL))]


def _ffn_prompt(x, ys, att, gates, hist, wts, tm):
    n, t, _ = x.shape
    tiles = t // tm
    tok = lambda w: pl.BlockSpec((1, tm, w), lambda i, j: (i, j, 0))
    ys_spec = pl.BlockSpec((SUPER, tm // S5_CHUNK, S5_ROW), lambda i, j: (0, i * tiles + j, 0))
    seq = pl.BlockSpec((1, CONV_W - 1, D_FF), lambda i, j: (i, 0, 0))
    fc = max(hi - lo for lo, hi in FF_CHUNKS)
    return pl.pallas_call(
        functools.partial(_ffn_body, multi_seq=0),
        grid=(n, tiles),
        in_specs=[tok(D_MODEL), ys_spec, tok(D_ATTN), tok(2 * D_MODEL), seq] + _ffn_weight_specs(),
        out_specs=[tok(D_MODEL), seq],
        out_shape=[jax.ShapeDtypeStruct((n, t, D_MODEL), F32),
                   jax.ShapeDtypeStruct((n, CONV_W - 1, D_FF), F32)],
        scratch_shapes=[pltpu.VMEM((FFN_SUB_TILES, len(FF_CHUNKS), tm // FFN_SUB_TILES + 8, fc), F32),
                        pltpu.VMEM((SUPER, tm, LANES), F32), pltpu.VMEM((CONV_W - 1, D_FF), F32)],
        compiler_params=_params("arbitrary", "arbitrary"),
        name="merge_ffn_prompt",
    )(x, ys, att, gates, hist, *wts)


def _ffn_sample(x, ys, att, gates, h1, h2, wts, seq_len):
    rows = x.shape[1]
    tok = lambda w: pl.BlockSpec((1, rows, w), lambda i: (0, 0, 0))
    ys_spec = pl.BlockSpec((SUPER, rows // S5_CHUNK, S5_ROW), lambda i: (0, 0, 0))
    full = pl.BlockSpec((rows, D_FF), lambda i: (0, 0))
    fc = max(hi - lo for lo, hi in FF_CHUNKS)
    return pl.pallas_call(
        functools.partial(_ffn_body, multi_seq=seq_len),
        grid=(1,),
        in_specs=[tok(D_MODEL), ys_spec, tok(D_ATTN), tok(2 * D_MODEL), full, full] + _ffn_weight_specs(),
        out_specs=[tok(D_MODEL), tok(D_FF)],
        out_shape=[jax.ShapeDtypeStruct((1, rows, D_MODEL), F32),
                   jax.ShapeDtypeStruct((1, rows, D_FF), F32)],
        scratch_shapes=[pltpu.VMEM((1, len(FF_CHUNKS), rows + 8, fc), F32), pltpu.VMEM((SUPER, rows, LANES), F32)],
        compiler_params=_params("arbitrary"),
        name="merge_ffn_sample",
    )(x, ys, att, gates, h1, h2, *wts)


PROMPT_TOKEN_TILE = 512
FFN_TOKEN_TILE = 512
FFN_SUB_TILES = 1
SAMPLE_STREAM_PAD = 16


def kernel(x_prompt, x_sample, state_ssm_re, state_ssm_im, cache_attn_k, cache_attn_v, cache_conv,
           g_mix, w_in, ssm_lambda_re, ssm_lambda_im, ssm_log_dt, ssm_b_re, ssm_b_im,
           ssm_c_re, ssm_c_im, ssm_d, w_ssm_glu, attn_rel_bias, w_attn_up, w_o,
           g_ffn, w_up, conv_w, conv_b, w_down, g_final):
    depth = w_in.shape[0]
    nb, seq, _ = x_prompt.shape
    db, dseq, _ = x_sample.shape
    assert depth == 1 and seq % PROMPT_TOKEN_TILE == 0 and min(ATTN_REACH, seq) == PROMPT_TOKEN_TILE
    assert seq % S5_CHUNK == 0 and dseq == S5_CHUNK and nb % S5_PROMPT_SEQS == 0 and db <= SAMPLE_STREAM_PAD
    l = 0

    log2e = math.log2(math.e)
    col_scale = jnp.ones((D_IN,), F32).at[D_SSM:D_SSM + D_ATTN].set(HEAD_DIM ** -0.5 * log2e)
    w_in_b = (w_in[l] * col_scale[None, :]).astype(BF16)
    wts = (w_ssm_glu[l].astype(BF16), w_attn_up[l].astype(BF16), w_o[l].astype(BF16), g_ffn[l][None, :],
           w_up[l].astype(BF16), conv_w[l], conv_b[l][None, :], w_down[l].astype(BF16), g_final[None, :])
    g_mix_l = g_mix[l][None, :]
    w_s5 = _s5_weights(ssm_lambda_re[l], ssm_lambda_im[l], ssm_log_dt[l], ssm_b_re[l], ssm_b_im[l],
                       ssm_c_re[l], ssm_c_im[l], ssm_d[l])
    table = attn_rel_bias[l].astype(F32) * log2e
    bias_p = _rel_bias(table, CHUNK, BAND)
    w_cache = cache_attn_k.shape[2]
    assert w_cache == ATTN_REACH
    bias_s = _rel_bias(table, dseq, w_cache + dseq)

    u, qkv, gates, k32, v32 = _inproj(x_prompt, g_mix_l, w_in_b, PROMPT_TOKEN_TILE)
    n_chunks = seq // S5_CHUNK
    tiles = nb // S5_PROMPT_SEQS
    zero_state = jnp.zeros((SUPER, tiles, S5_PROMPT_SEQS, SUPER_STATE), F32)
    ys, p_state = _s5(u, w_s5, zero_state, S5_PROMPT_SEQS, n_chunks)
    p_re, p_im = _state_from_super(p_state, nb)
    att = _attn_prompt(qkv, bias_p)
    zero_conv = jnp.zeros((nb, CONV_W - 1, D_FF), F32)
    y_prompt, p_conv = _ffn_prompt(x_prompt, ys, att, gates, zero_conv, wts, FFN_TOKEN_TILE)

    rows = db * dseq
    xs = x_sample.reshape(1, rows, D_MODEL)
    us, qkvs, gates_s, ks32, vs32 = _inproj(xs, g_mix_l, w_in_b, rows)
    us2 = jnp.pad(us, ((0, 0), (0, SAMPLE_STREAM_PAD - db), (0, 0)))
    s0 = _state_to_super(state_ssm_re[l], state_ssm_im[l], 1, SAMPLE_STREAM_PAD)
    yss, s_state = _s5(us2, w_s5, s0, SAMPLE_STREAM_PAD, 1)
    s_re, s_im = _state_from_super(s_state, db)
    yss = yss[:, :db]
    qkvs3 = qkvs.reshape(db, dseq, 3 * D_ATTN)
    kk = jnp.concatenate([cache_attn_k[l].reshape(db, w_cache, D_ATTN).astype(BF16),
                          qkvs3[:, :, D_ATTN:2 * D_ATTN]], axis=1)
    vv = jnp.concatenate([cache_attn_v[l].reshape(db, w_cache, D_ATTN).astype(BF16),
                          qkvs3[:, :, 2 * D_ATTN:]], axis=1)
    att_s = _attn_sample(qkvs3[:, :, :D_ATTN], kk, vv, bias_s)
    hist = cache_conv[l].astype(F32)
    pad_rows = ((0, 0), (0, dseq - 1), (0, 0))
    h1 = jnp.pad(hist[:, 1:2], pad_rows).reshape(rows, D_FF)
    h2 = jnp.pad(hist, ((0, 0), (0, dseq - 2), (0, 0))).reshape(rows, D_FF)
    y_s, a_s = _ffn_sample(xs, yss, att_s.reshape(1, rows, D_ATTN), gates_s, h1, h2, wts, dseq)
    s_conv = a_s.reshape(db, dseq, D_FF)[:, dseq - (CONV_W - 1):]

    heads = lambda a, n, t: a.reshape(1, n, t, N_HEADS, HEAD_DIM)
    return (y_prompt, y_s.reshape(db, dseq, D_MODEL),
            p_re[None], p_im[None],
            heads(k32, nb, PROMPT_TOKEN_TILE), heads(v32, nb, PROMPT_TOKEN_TILE), p_conv[None],
            s_re[None], s_im[None],
            heads(ks32, db, dseq), heads(vs32, db, dseq), s_conv[None])
```

```python
import functools
import math

import jax
import jax.numpy as jnp
import numpy as np
from jax import lax
from jax.experimental import pallas as pl
from jax.experimental.pallas import tpu as pltpu

F32 = jnp.float32
BF16 = jnp.bfloat16

D_MODEL = 1024
CHUNK = 64
LEFT_CHUNKS = 8
ATTN_REACH = LEFT_CHUNKS * CHUNK
BAND = ATTN_REACH + CHUNK
D_SSM = D_MODEL // 2
D_ATTN = D_MODEL // 2
HEAD_DIM = 64
N_HEADS = D_ATTN // HEAD_DIM
MAX_REL = 128
SSM_GROUP = 16
N_GROUPS = D_SSM // SSM_GROUP
SSM_STATE = 64
D_FF = 2816
CONV_W = 3
D_IN = D_SSM + 3 * D_ATTN + 2 * D_MODEL
RMS_EPS = 1e-6
MASK_VALUE = -1e30

V7X_VMEM_LIMIT_BYTES = 56 * 1024 * 1024
LANES = 128
FF_CHUNKS = ((0, 2816),)


def _params(*sem):
    return pltpu.CompilerParams(dimension_semantics=sem, vmem_limit_bytes=V7X_VMEM_LIMIT_BYTES)


def _const_spec(shape):
    zeros = (0,) * len(shape)
    return pl.BlockSpec(shape, lambda *_: zeros, pipeline_mode=pl.Buffered(1))


def _rms(x, g):
    return x * lax.rsqrt(jnp.mean(x * x, axis=-1, keepdims=True) + RMS_EPS) * g


def _gelu(x):
    k = math.sqrt(2.0 / math.pi)
    return (0.5 * x) * (1.0 + jnp.tanh(x * (k + (k * 0.044715) * (x * x))))


def _sigmoid(x):
    return 1.0 / (1.0 + jnp.exp(-x))


def _dot(a, b):
    return jnp.dot(a, b, preferred_element_type=F32)


S5_CHUNK = 16
SUPER = D_SSM // LANES
SUPER_GROUPS = N_GROUPS // SUPER
SUPER_STATE = 2 * SUPER_GROUPS * SSM_STATE
S5_ROW = S5_CHUNK * LANES
S5_PROMPT_SEQS = 4


def _inproj_body(x_ref, g_ref, w_ref, u_ref, qkv_ref, gate_ref, k32_ref, v32_ref, u_scr):
    xn = _rms(x_ref[0], g_ref[...]).astype(BF16)

    def seg(lo, hi):
        return _dot(xn, w_ref[:, lo:hi])

    u = seg(0, D_SSM)
    n_rows = u_scr.shape[1] // S5_CHUNK
    for s in range(SUPER):
        u_scr[s] = u[:, s * LANES:(s + 1) * LANES]
    for s in range(SUPER):
        for j in range(S5_CHUNK):
            frames = u_scr[s, pl.ds(j, n_rows, stride=S5_CHUNK), :]
            u_ref[s, :, j * LANES:(j + 1) * LANES] = frames.astype(BF16)
    q0 = D_SSM
    qkv_ref[0, :, 0:D_ATTN] = seg(q0, q0 + D_ATTN).astype(BF16)
    k = seg(q0 + D_ATTN, q0 + 2 * D_ATTN)
    v = seg(q0 + 2 * D_ATTN, q0 + 3 * D_ATTN)
    qkv_ref[0, :, D_ATTN:2 * D_ATTN] = k.astype(BF16)
    qkv_ref[0, :, 2 * D_ATTN:3 * D_ATTN] = v.astype(BF16)
    g0 = q0 + 3 * D_ATTN
    gate_ref[0, :, 0:D_MODEL] = seg(g0, g0 + D_MODEL).astype(BF16)
    gate_ref[0, :, D_MODEL:2 * D_MODEL] = seg(g0 + D_MODEL, g0 + 2 * D_MODEL).astype(BF16)

    @pl.when(pl.program_id(1) == pl.num_programs(1) - 1)
    def _():
        tm, kept = k.shape[0], k32_ref.shape[1]
        k32_ref[0] = k[tm - kept:, :]
        v32_ref[0] = v[tm - kept:, :]


def _inproj(x, g_mix, w_in_b, tm, kept):
    n, t, _ = x.shape
    assert kept <= tm
    tiles = t // tm
    n_rows = tm // S5_CHUNK
    tok = lambda w: pl.BlockSpec((1, tm, w), lambda i, j: (i, j, 0))
    keep = pl.BlockSpec((1, kept, D_ATTN), lambda i, j: (i, 0, 0))
    u_spec = pl.BlockSpec((SUPER, n_rows, S5_ROW), lambda i, j: (0, i * tiles + j, 0))
    return pl.pallas_call(
        _inproj_body,
        grid=(n, tiles),
        in_specs=[tok(D_MODEL), _const_spec((1, D_MODEL)), _const_spec((D_MODEL, D_IN))],
        out_specs=[u_spec, tok(3 * D_ATTN), tok(2 * D_MODEL), keep, keep],
        out_shape=[
            jax.ShapeDtypeStruct((SUPER, n * tiles * n_rows, S5_ROW), BF16),
            jax.ShapeDtypeStruct((n, t, 3 * D_ATTN), BF16),
            jax.ShapeDtypeStruct((n, t, 2 * D_MODEL), BF16),
            jax.ShapeDtypeStruct((n, kept, D_ATTN), F32),
            jax.ShapeDtypeStruct((n, kept, D_ATTN), F32),
        ],
        scratch_shapes=[pltpu.VMEM((SUPER, tm, LANES), F32)],
        compiler_params=_params("arbitrary", "arbitrary"),
        name="inproj",
    )(x, g_mix, w_in_b)


def _s5_weights(lam_re, lam_im, log_dt, b_re, b_im, c_re, c_im, d_skip):
    L = S5_CHUNK
    hi = lax.Precision.HIGHEST
    f = lambda a: a.astype(F32)
    cmul = lambda ar, ai, br, bi: (ar * br - ai * bi, ar * bi + ai * br)
    a_re, a_im = f(lam_re), f(lam_im)
    dt = jnp.exp(f(log_dt))[:, None]
    tau = jnp.arange(L + 1, dtype=F32)[None, :, None]
    mag = jnp.exp((a_re * dt)[:, None, :] * tau)
    ang = (a_im * dt)[:, None, :] * tau
    pw_re, pw_im = mag * jnp.cos(ang), mag * jnp.sin(ang)
    n_re, n_im = pw_re[:, 1] - 1.0, pw_im[:, 1]
    den = a_re * a_re + a_im * a_im
    f_re, f_im = (n_re * a_re + n_im * a_im) / den, (n_im * a_re - n_re * a_im) / den
    bb_re, bb_im = cmul(f_re[..., None], f_im[..., None], f(b_re), f(b_im))
    cp_re, cp_im = cmul(f(c_re)[:, :, None, :], f(c_im)[:, :, None, :], pw_re[:, None], pw_im[:, None])
    kt = (jnp.einsum('gatp,gph->ghta', cp_re[:, :, :L], bb_re, precision=hi)
          - jnp.einsum('gatp,gph->ghta', cp_im[:, :, :L], bb_im, precision=hi))
    kt = kt.at[:, :, 0, :].add(f(d_skip)[:, :, None] * jnp.eye(SSM_GROUP, dtype=F32)[None])
    rev = L - 1 - np.arange(L)
    bt_re, bt_im = jnp.swapaxes(bb_re, 1, 2)[:, None], jnp.swapaxes(bb_im, 1, 2)[:, None]
    wst_re, wst_im = cmul(pw_re[:, rev][:, :, None, :], pw_im[:, rev][:, :, None, :], bt_re, bt_im)

    sg = lambda a: a.reshape((SUPER, SUPER_GROUPS) + a.shape[1:])
    kc = jnp.transpose(sg(kt), (0, 2, 3, 1, 4)).reshape(SUPER, SSM_GROUP, S5_ROW)
    even = (np.arange(SUPER_GROUPS) % 2 == 0)[None, None, :, None, None]

    def in_form(w):
        w = jnp.transpose(sg(w), (0, 2, 1, 3, 4))
        w = jnp.concatenate([jnp.where(even, w, 0.0), jnp.where(even, 0.0, w)], axis=-1)
        return w.reshape(SUPER, S5_ROW, 2 * SSM_STATE)

    def out_form(c):
        return jnp.transpose(sg(c), (0, 4, 3, 1, 2)).reshape(SUPER, SSM_STATE, S5_ROW)

    wc = jnp.stack([in_form(wst_re), in_form(wst_im)], axis=1)
    ic = jnp.stack([out_form(cp_re[:, :, 1:L + 1]), out_form(-cp_im[:, :, 1:L + 1])], axis=1)
    half = SUPER_STATE // 2
    lam = jnp.concatenate([pw_re[:, L].reshape(SUPER, half), pw_im[:, L].reshape(SUPER, half)], axis=-1)
    return kc.astype(BF16), wc.astype(BF16), ic.astype(BF16), lam[:, None, :]


def _s5_body(u_ref, kc_ref, wc_ref, ic_ref, lam_ref, s0_ref, y_ref, sfin_ref, toep, wst, wint, x_scr, prev_scr,
             *, ns, nc):
    half = SUPER_STATE // 2

    @pl.when(pl.program_id(1) == 0)
    def _():
        zero = jnp.zeros((), BF16)
        group_of = lambda shape, dim: (lax.broadcasted_iota(jnp.int32, shape, dim) % LANES) // SSM_GROUP
        col_group = group_of((SSM_GROUP, S5_ROW), 1)
        kc = kc_ref[0]
        bblk = jnp.concatenate([jnp.where(col_group == g, kc, zero) for g in range(SUPER_GROUPS)], axis=0)
        for j in range(S5_CHUNK):
            rows = slice(j * LANES, (j + 1) * LANES)
            if j:
                toep[rows, 0:j * LANES] = jnp.zeros((LANES, j * LANES), BF16)
            toep[rows, j * LANES:S5_ROW] = bblk[:, 0:S5_ROW - j * LANES]
        row_pair = group_of((S5_ROW, LANES), 0) // 2
        for part in range(2):
            w = wc_ref[0, part]
            for k in range(SUPER_GROUPS // 2):
                c0 = part * half + k * LANES
                wst[:, c0:c0 + LANES] = jnp.where(row_pair == k, w, zero)
        col_group = group_of((SSM_STATE, S5_ROW), 1)
        for part in range(2):
            c = ic_ref[0, part]
            for g in range(SUPER_GROUPS):
                r0 = part * half + g * SSM_STATE
                wint[r0:r0 + SSM_STATE, :] = jnp.where(col_group == g, c, zero)

    u = u_ref[0]
    x = _dot(u, wst[...])
    nblk = SUPER_STATE // LANES
    hblk = nblk // 2
    for k in range(nblk):
        x_scr[k] = x[:, k * LANES:(k + 1) * LANES]
    lam = [lam_ref[0, :, k * LANES:(k + 1) * LANES] for k in range(nblk)]

    def step(c, s):
        rows = pl.ds(c, ns, stride=nc)
        nxt = []
        for k in range(nblk):
            prev_scr[k, rows, :] = s[k]
        for k in range(hblk):
            l_re, l_im, s_re, s_im = lam[k], lam[hblk + k], s[k], s[hblk + k]
            nxt.append((l_re * s_re - l_im * s_im + x_scr[k, rows, :],
                        l_re * s_im + l_im * s_re + x_scr[hblk + k, rows, :]))
        return tuple(p[0] for p in nxt) + tuple(p[1] for p in nxt)

    s0 = s0_ref[0, 0]
    s_fin = lax.fori_loop(0, nc, step, tuple(s0[:, k * LANES:(k + 1) * LANES] for k in range(nblk)),
                          unroll=min(nc, 4))
    for k in range(nblk):
        sfin_ref[0, 0, :, k * LANES:(k + 1) * LANES] = s_fin[k]

    prev = jnp.concatenate([prev_scr[k] for k in range(nblk)], axis=-1).astype(BF16)
    blk = 2 * LANES
    for tb in range(S5_ROW // blk):
        cols = slice(tb * blk, (tb + 1) * blk)
        depth = (tb + 1) * blk
        y = _dot(u[:, 0:depth], toep[0:depth, cols]) + _dot(prev, wint[:, cols])
        y_ref[0, :, cols] = _gelu(y).astype(BF16)


def _s5(u2, weights, s0, ns, nc):
    kc, wc, ic, lam = weights
    tiles = s0.shape[1]
    r = ns * nc
    per_sg = lambda *dims: pl.BlockSpec((1,) + dims, lambda s, i: (s,) + (0,) * len(dims))
    rows = pl.BlockSpec((1, r, S5_ROW), lambda s, i: (s, i, 0))
    state = pl.BlockSpec((1, 1, ns, SUPER_STATE), lambda s, i: (s, i, 0, 0))
    return pl.pallas_call(
        functools.partial(_s5_body, ns=ns, nc=nc),
        grid=(SUPER, tiles),
        in_specs=[rows, per_sg(SSM_GROUP, S5_ROW), per_sg(2, S5_ROW, LANES), per_sg(2, SSM_STATE, S5_ROW),
                  per_sg(1, SUPER_STATE), state],
        out_specs=[rows, state],
        out_shape=[jax.ShapeDtypeStruct((SUPER, tiles * r, S5_ROW), BF16),
                   jax.ShapeDtypeStruct((SUPER, tiles, ns, SUPER_STATE), F32)],
        scratch_shapes=[pltpu.VMEM((S5_ROW, S5_ROW), BF16), pltpu.VMEM((S5_ROW, SUPER_STATE), BF16),
                        pltpu.VMEM((SUPER_STATE, S5_ROW), BF16),
                        pltpu.VMEM((SUPER_STATE // LANES, r, LANES), F32),
                        pltpu.VMEM((SUPER_STATE // LANES, r, LANES), F32)],
        compiler_params=_params("arbitrary", "arbitrary"),
        name="s5",
    )(u2, kc, wc, ic, lam, s0)


def _state_to_super(s_re, s_im, tiles, ns):
    n = s_re.shape[0]
    half = SUPER_STATE // 2
    s = jnp.concatenate([s_re.astype(F32).reshape(n, SUPER, half), s_im.astype(F32).reshape(n, SUPER, half)], -1)
    s = jnp.pad(jnp.transpose(s, (1, 0, 2)), ((0, 0), (0, tiles * ns - n), (0, 0)))
    return s.reshape(SUPER, tiles, ns, SUPER_STATE)


def _state_from_super(s, n):
    half = SUPER_STATE // 2
    s = jnp.transpose(s.reshape(SUPER, -1, SUPER_STATE)[:, :n], (1, 0, 2))
    return (s[:, :, :half].reshape(n, N_GROUPS, SSM_STATE), s[:, :, half:].reshape(n, N_GROUPS, SSM_STATE))


def _attend(qc, kb, vb, bias_ref, valid, skip=0):
    tq = qc.shape[0]
    left = lax.broadcasted_iota(jnp.int32, (tq, LANES), 1) < HEAD_DIM
    zero = jnp.zeros((), BF16)
    pairs = range(N_HEADS // 2)
    cols = [slice(hp * LANES, (hp + 1) * LANES) for hp in pairs]
    scores = []
    for hp in pairs:
        qp = qc[:, cols[hp]]
        q2 = jnp.concatenate([jnp.where(left, qp, zero), jnp.where(left, zero, qp)], axis=0)
        sc = lax.dot_general(q2, kb[:, cols[hp]], (((1,), (1,)), ((), ())), preferred_element_type=F32)
        sc = sc + bias_ref[hp, :, skip:]
        if valid is not None:
            sc = jnp.where(valid, sc, MASK_VALUE)
        scores.append(sc)
    probs, dens = [], []
    for sc in scores:
        e = jnp.exp2(sc - jnp.max(sc, axis=-1, keepdims=True))
        dens.append(jnp.sum(e, axis=-1, keepdims=True))
        probs.append(e.astype(BF16))
    outs = []
    for hp in pairs:
        o2 = _dot(probs[hp], vb[:, cols[hp]]) / dens[hp]
        outs.append(jnp.where(left, o2[:tq], o2[tq:]))
    return jnp.concatenate(outs, axis=-1)


def _attn_prompt_body(q_ref, k_ref, v_ref, bias_ref, o_ref, kpad, vpad):
    t = q_ref.shape[1]
    zeros = jnp.zeros((ATTN_REACH, D_ATTN), BF16)
    kpad[0:ATTN_REACH, :] = zeros
    vpad[0:ATTN_REACH, :] = zeros
    kpad[ATTN_REACH:ATTN_REACH + t, :] = k_ref[0]
    vpad[ATTN_REACH:ATTN_REACH + t, :] = v_ref[0]
    def chunk(c, skip, masked):
        r0 = c * CHUNK if isinstance(c, int) else pl.multiple_of(c * CHUNK, CHUNK)
        qc = q_ref[0, pl.ds(r0, CHUNK), :]
        kb = kpad[pl.ds(r0 + skip, BAND - skip), :]
        vb = vpad[pl.ds(r0 + skip, BAND - skip), :]
        valid = None
        if masked:
            col = lax.broadcasted_iota(jnp.int32, (2 * CHUNK, BAND - skip), 1) + skip
            valid = col >= ATTN_REACH - r0
        o_ref[0, pl.ds(r0, CHUNK), :] = _attend(qc, kb, vb, bias_ref, valid, skip).astype(BF16)

    n_chunks = t // CHUNK
    for c in range(min(LEFT_CHUNKS, n_chunks)):
        seen = (c + 1) * CHUNK
        skip = (BAND - seen) // LANES * LANES
        chunk(c, skip, BAND - skip != seen)
    lax.fori_loop(LEFT_CHUNKS, n_chunks, lambda c, _: chunk(c, 0, False), None, unroll=2)


def _attn_prompt(qkv, bias):
    n, t, _ = qkv.shape
    part = lambda j: pl.BlockSpec((1, t, D_ATTN), lambda i: (i, 0, j))
    return pl.pallas_call(
        _attn_prompt_body,
        grid=(n,),
        in_specs=[part(0), part(1), part(2), _const_spec((N_HEADS // 2, 2 * CHUNK, BAND))],
        out_specs=pl.BlockSpec((1, t, D_ATTN), lambda i: (i, 0, 0)),
        out_shape=jax.ShapeDtypeStruct((n, t, D_ATTN), BF16),
        scratch_shapes=[pltpu.VMEM((ATTN_REACH + t, D_ATTN), BF16), pltpu.VMEM((ATTN_REACH + t, D_ATTN), BF16)],
        compiler_params=_params("arbitrary"),
        name="attn_prompt",
    )(qkv, qkv, qkv, bias)


def _attn_sample_body(q_ref, k_ref, v_ref, bias_ref, o_ref):
    o_ref[0] = _attend(q_ref[0], k_ref[0], v_ref[0], bias_ref, None).astype(BF16)


def _attn_sample(q, kk, vv, bias):
    n, tq, _ = q.shape
    tk = kk.shape[1]
    return pl.pallas_call(
        _attn_sample_body,
        grid=(n,),
        in_specs=[pl.BlockSpec((1, tq, D_ATTN), lambda i: (i, 0, 0)),
                  pl.BlockSpec((1, tk, D_ATTN), lambda i: (i, 0, 0)),
                  pl.BlockSpec((1, tk, D_ATTN), lambda i: (i, 0, 0)),
                  _const_spec((N_HEADS // 2, 2 * tq, tk))],
        out_specs=pl.BlockSpec((1, tq, D_ATTN), lambda i: (i, 0, 0)),
        out_shape=jax.ShapeDtypeStruct((n, tq, D_ATTN), BF16),
        compiler_params=_params("arbitrary"),
        name="attn_sample",
    )(q, kk, vv, bias)


def _rel_bias(table, tq, tk):
    width = tq + tk - 1
    z = np.arange(width)
    idx = np.clip(ATTN_REACH - np.where(z < tk, z, z - width), -MAX_REL, MAX_REL) + MAX_REL
    ext = table.astype(F32)[:, idx]
    flat = jnp.tile(ext, (1, tq))[:, :tq * (width - 1)]
    bias = flat.reshape(N_HEADS, tq, width - 1)[:, :, :tk]
    return bias.reshape(N_HEADS // 2, 2 * tq, tk)


def _merge_stage(x_ref, ys_ref, att_ref, gate_ref, wglu, watt, wo, ys_scr):
    tm = x_ref.shape[1]
    gated_att = _sigmoid(gate_ref[0, :, D_MODEL:].astype(F32)) * _dot(att_ref[0], watt[...])
    n_rows = tm // S5_CHUNK
    for s in range(SUPER):
        for j in range(S5_CHUNK):
            ys_scr[s, pl.ds(j, n_rows, stride=S5_CHUNK), :] = ys_ref[s, :, j * LANES:(j + 1) * LANES].astype(F32)
    ys = jnp.concatenate([ys_scr[s] for s in range(SUPER)], axis=-1).astype(BF16)
    gl = _dot(ys, wglu[...])
    br_ssm = gl[:, :D_MODEL] * _sigmoid(gl[:, D_MODEL:])
    mix = _sigmoid(gate_ref[0, :, :D_MODEL].astype(F32)) * br_ssm + gated_att
    return x_ref[0] + _dot(mix.astype(BF16), wo[...])


def _convglu_stage(h, hn, a_scr, wup, cw, cb, wdown, gfin, history):
    tm = h.shape[0]
    acc = jnp.zeros((tm, D_MODEL), F32)
    for lo, hi in FF_CHUNKS:
        w = hi - lo
        a = _dot(hn, wup[:, lo:hi])
        b = _dot(hn, wup[:, D_FF + lo:D_FF + hi])
        a_scr[8:8 + tm, 0:w] = a
        a1, a2 = history(a, lo, hi)
        c = cb[:, lo:hi] + cw[0:1, lo:hi] * a2 + cw[1:2, lo:hi] * a1 + cw[2:3, lo:hi] * a
        act = (_gelu(c) * b).astype(BF16)
        acc = acc + _dot(act, wdown[lo:hi, :])
    return _rms(h + acc, gfin[...])


def _ffn_sample_body(x_ref, ys_ref, att_ref, gate_ref, h1_ref, h2_ref, wglu, watt, wo, gffn, wup, cw, cb, wdown,
                     gfin, y_ref, conv_ref, a_scr, ys_scr, *, seq_len):
    tm = x_ref.shape[1]
    h = _merge_stage(x_ref, ys_ref, att_ref, gate_ref, wglu, watt, wo, ys_scr)
    hn = _rms(h, gffn[...]).astype(BF16)
    pos = lax.broadcasted_iota(jnp.int32, (tm, 1), 0) % seq_len

    def history(a, lo, hi):
        w = hi - lo
        a_scr[6:8, 0:w] = jnp.zeros((2, w), F32)
        conv_ref[0, :, lo:hi] = a
        a1 = jnp.where(pos < 1, h1_ref[:, lo:hi], a_scr[7:7 + tm, 0:w])
        a2 = jnp.where(pos < 2, h2_ref[:, lo:hi], a_scr[6:6 + tm, 0:w])
        return a1, a2

    y_ref[0] = _convglu_stage(h, hn, a_scr, wup, cw, cb, wdown, gfin, history)


def _ffn_prompt_body(x_ref, ys_ref, att_ref, gate_ref, wglu, watt, wo, gffn, wup, cw, cb, wdown, gfin,
                     y_ref, conv_ref, a_scr, ys_scr, carry):
    tm = x_ref.shape[1]

    @pl.when(pl.program_id(1) == 0)
    def _():
        carry[...] = jnp.zeros(carry.shape, F32)

    h = _merge_stage(x_ref, ys_ref, att_ref, gate_ref, wglu, watt, wo, ys_scr)
    hn = _rms(h, gffn[...]).astype(BF16)

    def history(a, lo, hi):
        w = hi - lo
        a_scr[6:8, 0:w] = carry[:, lo:hi]
        carry[:, lo:hi] = a[tm - 2:tm, :]
        return a_scr[7:7 + tm, 0:w], a_scr[6:6 + tm, 0:w]

    y_ref[0] = _convglu_stage(h, hn, a_scr, wup, cw, cb, wdown, gfin, history)

    @pl.when(pl.program_id(1) == pl.num_programs(1) - 1)
    def _():
        conv_ref[0] = carry[...]


def _ffn_weight_specs():
    return [_const_spec((D_SSM, 2 * D_MODEL)), _const_spec((D_ATTN, D_MODEL)), _const_spec((D_MODEL, D_MODEL)),
            _const_spec((1, D_MODEL)), _const_spec((D_MODEL, 2 * D_FF)), _const_spec((CONV_W, D_FF)),
            _const_spec((1, D_FF)), _const_spec((D_FF, D_MODEL)), _const_spec((1, D_MODEL))]


def _ffn_prompt(x, ys, att, gates, wts, tm):
    n, t, _ = x.shape
    tiles = t // tm
    tok = lambda w: pl.BlockSpec((1, tm, w), lambda i, j: (i, j, 0))
    ys_spec = pl.BlockSpec((SUPER, tm // S5_CHUNK, S5_ROW), lambda i, j: (0, i * tiles + j, 0))
    seq = pl.BlockSpec((1, CONV_W - 1, D_FF), lambda i, j: (i, 0, 0))
    fc = max(hi - lo for lo, hi in FF_CHUNKS)
    return pl.pallas_call(
        _ffn_prompt_body,
        grid=(n, tiles),
        in_specs=[tok(D_MODEL), ys_spec, tok(D_ATTN), tok(2 * D_MODEL)] + _ffn_weight_specs(),
        out_specs=[tok(D_MODEL), seq],
        out_shape=[jax.ShapeDtypeStruct((n, t, D_MODEL), F32),
                   jax.ShapeDtypeStruct((n, CONV_W - 1, D_FF), F32)],
        scratch_shapes=[pltpu.VMEM((tm + 8, fc), F32), pltpu.VMEM((SUPER, tm, LANES), F32),
                        pltpu.VMEM((CONV_W - 1, D_FF), F32)],
        compiler_params=_params("arbitrary", "arbitrary"),
        name="merge_ffn_prompt",
    )(x, ys, att, gates, *wts)


def _ffn_sample(x, ys, att, gates, h1, h2, wts, seq_len):
    rows = x.shape[1]
    tok = lambda w: pl.BlockSpec((1, rows, w), lambda i: (0, 0, 0))
    ys_spec = pl.BlockSpec((SUPER, rows // S5_CHUNK, S5_ROW), lambda i: (0, 0, 0))
    full = pl.BlockSpec((rows, D_FF), lambda i: (0, 0))
    fc = max(hi - lo for lo, hi in FF_CHUNKS)
    return pl.pallas_call(
        functools.partial(_ffn_sample_body, seq_len=seq_len),
        grid=(1,),
        in_specs=[tok(D_MODEL), ys_spec, tok(D_ATTN), tok(2 * D_MODEL), full, full] + _ffn_weight_specs(),
        out_specs=[tok(D_MODEL), tok(D_FF)],
        out_shape=[jax.ShapeDtypeStruct((1, rows, D_MODEL), F32),
                   jax.ShapeDtypeStruct((1, rows, D_FF), F32)],
        scratch_shapes=[pltpu.VMEM((rows + 8, fc), F32), pltpu.VMEM((SUPER, rows, LANES), F32)],
        compiler_params=_params("arbitrary"),
        name="merge_ffn_sample",
    )(x, ys, att, gates, h1, h2, *wts)


PROMPT_TOKEN_TILE = 1024
FFN_TOKEN_TILE = 512
SAMPLE_STREAM_PAD = 16


def kernel(x_prompt, x_sample, state_ssm_re, state_ssm_im, cache_attn_k, cache_attn_v, cache_conv,
           g_mix, w_in, ssm_lambda_re, ssm_lambda_im, ssm_log_dt, ssm_b_re, ssm_b_im,
           ssm_c_re, ssm_c_im, ssm_d, w_ssm_glu, attn_rel_bias, w_attn_up, w_o,
           g_ffn, w_up, conv_w, conv_b, w_down, g_final):
    depth = w_in.shape[0]
    nb, seq, _ = x_prompt.shape
    db, dseq, _ = x_sample.shape
    kept = min(ATTN_REACH, seq)
    assert depth == 1 and seq % PROMPT_TOKEN_TILE == 0 and seq % FFN_TOKEN_TILE == 0
    assert seq % S5_CHUNK == 0 and dseq == S5_CHUNK and nb % S5_PROMPT_SEQS == 0 and db <= SAMPLE_STREAM_PAD
    l = 0

    log2e = math.log2(math.e)
    col_scale = jnp.ones((D_IN,), F32).at[D_SSM:D_SSM + D_ATTN].set(HEAD_DIM ** -0.5 * log2e)
    w_in_b = (w_in[l] * col_scale[None, :]).astype(BF16)
    wts = (w_ssm_glu[l].astype(BF16), w_attn_up[l].astype(BF16), w_o[l].astype(BF16), g_ffn[l][None, :],
           w_up[l].astype(BF16), conv_w[l], conv_b[l][None, :], w_down[l].astype(BF16), g_final[None, :])
    g_mix_l = g_mix[l][None, :]
    w_s5 = _s5_weights(ssm_lambda_re[l], ssm_lambda_im[l], ssm_log_dt[l], ssm_b_re[l], ssm_b_im[l],
                       ssm_c_re[l], ssm_c_im[l], ssm_d[l])
    table = attn_rel_bias[l].astype(F32) * log2e
    bias_p = _rel_bias(table, CHUNK, BAND)
    w_cache = cache_attn_k.shape[2]
    assert w_cache == ATTN_REACH
    bias_s = _rel_bias(table, dseq, w_cache + dseq)

    u, qkv, gates, k32, v32 = _inproj(x_prompt, g_mix_l, w_in_b, PROMPT_TOKEN_TILE, kept)
    n_chunks = seq // S5_CHUNK
    tiles = nb // S5_PROMPT_SEQS
    zero_state = jnp.zeros((SUPER, tiles, S5_PROMPT_SEQS, SUPER_STATE), F32)
    ys, p_state = _s5(u, w_s5, zero_state, S5_PROMPT_SEQS, n_chunks)
    p_re, p_im = _state_from_super(p_state, nb)
    att = _attn_prompt(qkv, bias_p)
    y_prompt, p_conv = _ffn_prompt(x_prompt, ys, att, gates, wts, FFN_TOKEN_TILE)

    rows = db * dseq
    xs = x_sample.reshape(1, rows, D_MODEL)
    us, qkvs, gates_s, ks32, vs32 = _inproj(xs, g_mix_l, w_in_b, rows, rows)
    us2 = jnp.pad(us, ((0, 0), (0, SAMPLE_STREAM_PAD - db), (0, 0)))
    s0 = _state_to_super(state_ssm_re[l], state_ssm_im[l], 1, SAMPLE_STREAM_PAD)
    yss, s_state = _s5(us2, w_s5, s0, SAMPLE_STREAM_PAD, 1)
    s_re, s_im = _state_from_super(s_state, db)
    yss = yss[:, :db]
    qkvs3 = qkvs.reshape(db, dseq, 3 * D_ATTN)
    kk = jnp.concatenate([cache_attn_k[l].reshape(db, w_cache, D_ATTN).astype(BF16),
                          qkvs3[:, :, D_ATTN:2 * D_ATTN]], axis=1)
    vv = jnp.concatenate([cache_attn_v[l].reshape(db, w_cache, D_ATTN).astype(BF16),
                          qkvs3[:, :, 2 * D_ATTN:]], axis=1)
    att_s = _attn_sample(qkvs3[:, :, :D_ATTN], kk, vv, bias_s)
    hist = cache_conv[l].astype(F32)
    pad_rows = ((0, 0), (0, dseq - 1), (0, 0))
    h1 = jnp.pad(hist[:, 1:2], pad_rows).reshape(rows, D_FF)
    h2 = jnp.pad(hist, ((0, 0), (0, dseq - 2), (0, 0))).reshape(rows, D_FF)
    y_s, a_s = _ffn_sample(xs, yss, att_s.reshape(1, rows, D_ATTN), gates_s, h1, h2, wts, dseq)
    s_conv = a_s.reshape(db, dseq, D_FF)[:, dseq - (CONV_W - 1):]

    heads = lambda a, n, t: a.reshape(1, n, t, N_HEADS, HEAD_DIM)
    return (y_prompt, y_s.reshape(db, dseq, D_MODEL),
            p_re[None], p_im[None],
            heads(k32, nb, kept), heads(v32, nb, kept), p_conv[None],
            s_re[None], s_im[None],
            heads(ks32, db, dseq), heads(vs32, db, dseq), s_conv[None])
```

```python
import functools
import math

import jax
import jax.numpy as jnp
import numpy as np
from jax import lax
from jax.experimental import pallas as pl
from jax.experimental.pallas import tpu as pltpu

F32 = jnp.float32
BF16 = jnp.bfloat16

D_MODEL = 1024
CHUNK = 64
LEFT_CHUNKS = 8
ATTN_REACH = LEFT_CHUNKS * CHUNK
BAND = ATTN_REACH + CHUNK
D_SSM = D_MODEL // 2
D_ATTN = D_MODEL // 2
HEAD_DIM = 64
N_HEADS = D_ATTN // HEAD_DIM
MAX_REL = 128
SSM_GROUP = 16
N_GROUPS = D_SSM // SSM_GROUP
SSM_STATE = 64
D_FF = 2816
CONV_W = 3
D_IN = D_SSM + 3 * D_ATTN + 2 * D_MODEL
RMS_EPS = 1e-6
MASK_VALUE = -1e30

V7X_VMEM_LIMIT_BYTES = 56 * 1024 * 1024
LANES = 128
MERGE_BLOCK = 256
FF_CHUNKS = ((0, 2816),)


def _params(*sem):
    return pltpu.CompilerParams(dimension_semantics=sem, vmem_limit_bytes=V7X_VMEM_LIMIT_BYTES)


def _const_spec(shape):
    zeros = (0,) * len(shape)
    return pl.BlockSpec(shape, lambda *_: zeros, pipeline_mode=pl.Buffered(1))


def _rms(x, g):
    return x * lax.rsqrt(jnp.mean(x * x, axis=-1, keepdims=True) + RMS_EPS) * g


def _gelu(x):
    k = math.sqrt(2.0 / math.pi)
    return (0.5 * x) * (1.0 + jnp.tanh(x * (k + (k * 0.044715) * (x * x))))


def _sigmoid(x):
    return 1.0 / (1.0 + jnp.exp(-x))


def _dot(a, b):
    return jnp.dot(a, b, preferred_element_type=F32)


S5_CHUNK = 16
SUPER = D_SSM // LANES
SUPER_GROUPS = N_GROUPS // SUPER
SUPER_STATE = 2 * SUPER_GROUPS * SSM_STATE
S5_ROW = S5_CHUNK * LANES
S5_PROMPT_SEQS = 8


def _inproj_body(x_ref, g_ref, w_ref, u_ref, qkv_ref, gate_ref, k32_ref, v32_ref):
    xn = _rms(x_ref[0], g_ref[...]).astype(BF16)

    def seg(lo, hi):
        return _dot(xn, w_ref[:, lo:hi])

    u = seg(0, D_SSM)
    n_rows = u.shape[0] // S5_CHUNK
    for s in range(SUPER):
        frames = u[:, s * LANES:(s + 1) * LANES].reshape(n_rows, S5_CHUNK, LANES)
        frames = pltpu.einshape("cjl->jcl", frames)
        for j in range(S5_CHUNK):
            u_ref[s, :, j * LANES:(j + 1) * LANES] = frames[j].astype(BF16)
    q0 = D_SSM
    qkv_ref[0, :, 0:D_ATTN] = seg(q0, q0 + D_ATTN).astype(BF16)
    k = seg(q0 + D_ATTN, q0 + 2 * D_ATTN)
    v = seg(q0 + 2 * D_ATTN, q0 + 3 * D_ATTN)
    qkv_ref[0, :, D_ATTN:2 * D_ATTN] = k.astype(BF16)
    qkv_ref[0, :, 2 * D_ATTN:3 * D_ATTN] = v.astype(BF16)
    g0 = q0 + 3 * D_ATTN
    gate_ref[0, :, 0:D_MODEL] = seg(g0, g0 + D_MODEL).astype(BF16)
    gate_ref[0, :, D_MODEL:2 * D_MODEL] = seg(g0 + D_MODEL, g0 + 2 * D_MODEL).astype(BF16)

    @pl.when(pl.program_id(1) == pl.num_programs(1) - 1)
    def _():
        tm, kept = k.shape[0], k32_ref.shape[1]
        k32_ref[0] = k[tm - kept:, :]
        v32_ref[0] = v[tm - kept:, :]


def _inproj(x, g_mix, w_in_b, tm, kept):
    n, t, _ = x.shape
    assert kept <= tm
    tiles = t // tm
    n_rows = tm // S5_CHUNK
    tok = lambda w: pl.BlockSpec((1, tm, w), lambda i, j: (i, j, 0))
    keep = pl.BlockSpec((1, kept, D_ATTN), lambda i, j: (i, 0, 0))
    u_spec = pl.BlockSpec((SUPER, n_rows, S5_ROW), lambda i, j: (0, i * tiles + j, 0))
    return pl.pallas_call(
        _inproj_body,
        grid=(n, tiles),
        in_specs=[tok(D_MODEL), _const_spec((1, D_MODEL)), _const_spec((D_MODEL, D_IN))],
        out_specs=[u_spec, tok(3 * D_ATTN), tok(2 * D_MODEL), keep, keep],
        out_shape=[
            jax.ShapeDtypeStruct((SUPER, n * tiles * n_rows, S5_ROW), BF16),
            jax.ShapeDtypeStruct((n, t, 3 * D_ATTN), BF16),
            jax.ShapeDtypeStruct((n, t, 2 * D_MODEL), BF16),
            jax.ShapeDtypeStruct((n, kept, D_ATTN), F32),
            jax.ShapeDtypeStruct((n, kept, D_ATTN), F32),
        ],
        compiler_params=_params("arbitrary", "arbitrary"),
        name="inproj",
    )(x, g_mix, w_in_b)


def _s5_weights(lam_re, lam_im, log_dt, b_re, b_im, c_re, c_im, d_skip):
    L = S5_CHUNK
    hi = lax.Precision.HIGHEST
    f = lambda a: a.astype(F32)
    cmul = lambda ar, ai, br, bi: (ar * br - ai * bi, ar * bi + ai * br)
    a_re, a_im = f(lam_re), f(lam_im)
    dt = jnp.exp(f(log_dt))[:, None]
    tau = jnp.arange(L + 1, dtype=F32)[None, :, None]
    mag = jnp.exp((a_re * dt)[:, None, :] * tau)
    ang = (a_im * dt)[:, None, :] * tau
    pw_re, pw_im = mag * jnp.cos(ang), mag * jnp.sin(ang)
    n_re, n_im = pw_re[:, 1] - 1.0, pw_im[:, 1]
    den = a_re * a_re + a_im * a_im
    f_re, f_im = (n_re * a_re + n_im * a_im) / den, (n_im * a_re - n_re * a_im) / den
    bb_re, bb_im = cmul(f_re[..., None], f_im[..., None], f(b_re), f(b_im))
    cp_re, cp_im = cmul(f(c_re)[:, :, None, :], f(c_im)[:, :, None, :], pw_re[:, None], pw_im[:, None])
    kt = (jnp.einsum('gatp,gph->ghta', cp_re[:, :, :L], bb_re, precision=hi)
          - jnp.einsum('gatp,gph->ghta', cp_im[:, :, :L], bb_im, precision=hi))
    kt = kt.at[:, :, 0, :].add(f(d_skip)[:, :, None] * jnp.eye(SSM_GROUP, dtype=F32)[None])
    rev = L - 1 - np.arange(L)
    bt_re, bt_im = jnp.swapaxes(bb_re, 1, 2)[:, None], jnp.swapaxes(bb_im, 1, 2)[:, None]
    wst_re, wst_im = cmul(pw_re[:, rev][:, :, None, :], pw_im[:, rev][:, :, None, :], bt_re, bt_im)

    sg = lambda a: a.reshape((SUPER, SUPER_GROUPS) + a.shape[1:])
    kc = jnp.transpose(sg(kt), (0, 2, 3, 1, 4)).reshape(SUPER, SSM_GROUP, S5_ROW)
    even = (np.arange(SUPER_GROUPS) % 2 == 0)[None, None, :, None, None]

    def in_form(w):
        w = jnp.transpose(sg(w), (0, 2, 1, 3, 4))
        w = jnp.concatenate([jnp.where(even, w, 0.0), jnp.where(even, 0.0, w)], axis=-1)
        return w.reshape(SUPER, S5_ROW, 2 * SSM_STATE)

    def out_form(c):
        return jnp.transpose(sg(c), (0, 4, 3, 1, 2)).reshape(SUPER, SSM_STATE, S5_ROW)

    wc = jnp.stack([in_form(wst_re), in_form(wst_im)], axis=1)
    ic = jnp.stack([out_form(cp_re[:, :, 1:L + 1]), out_form(-cp_im[:, :, 1:L + 1])], axis=1)
    half = SUPER_STATE // 2
    lam = jnp.concatenate([pw_re[:, L].reshape(SUPER, half), pw_im[:, L].reshape(SUPER, half)], axis=-1)
    return kc.astype(BF16), wc.astype(BF16), ic.astype(BF16), lam[:, None, :]


def _s5_body(u_ref, kc_ref, wc_ref, ic_ref, lam_ref, s0_ref, y_ref, sfin_ref, toep, wst, wint, x_scr, prev_scr,
             *, ns, nc):
    half = SUPER_STATE // 2

    @pl.when(pl.program_id(1) == 0)
    def _():
        zero = jnp.zeros((), BF16)
        group_of = lambda shape, dim: (lax.broadcasted_iota(jnp.int32, shape, dim) % LANES) // SSM_GROUP
        col_group = group_of((SSM_GROUP, S5_ROW), 1)
        kc = kc_ref[0]
        bblk = jnp.concatenate([jnp.where(col_group == g, kc, zero) for g in range(SUPER_GROUPS)], axis=0)
        for j in range(S5_CHUNK):
            rows = slice(j * LANES, (j + 1) * LANES)
            if j:
                toep[rows, 0:j * LANES] = jnp.zeros((LANES, j * LANES), BF16)
            toep[rows, j * LANES:S5_ROW] = bblk[:, 0:S5_ROW - j * LANES]
        row_pair = group_of((S5_ROW, LANES), 0) // 2
        for part in range(2):
            w = wc_ref[0, part]
            for k in range(SUPER_GROUPS // 2):
                c0 = part * half + k * LANES
                wst[:, c0:c0 + LANES] = jnp.where(row_pair == k, w, zero)
        col_group = group_of((SSM_STATE, S5_ROW), 1)
        for part in range(2):
            c = ic_ref[0, part]
            for g in range(SUPER_GROUPS):
                r0 = part * half + g * SSM_STATE
                wint[r0:r0 + SSM_STATE, :] = jnp.where(col_group == g, c, zero)

    u = u_ref[0]
    x = _dot(u, wst[...])
    regroup = nc > 1
    if regroup:
        x = pltpu.einshape("ncl->cnl", x.reshape(ns, nc, SUPER_STATE)).reshape(ns * nc, SUPER_STATE)
    nblk = SUPER_STATE // LANES
    hblk = nblk // 2
    for k in range(nblk):
        x_scr[k] = x[:, k * LANES:(k + 1) * LANES]
    lam = [lam_ref[0, :, k * LANES:(k + 1) * LANES] for k in range(nblk)]

    def step(c, s):
        rows = pl.ds(pl.multiple_of(c * ns, ns), ns)
        nxt = []
        for k in range(nblk):
            prev_scr[k, rows, :] = s[k]
        for k in range(hblk):
            l_re, l_im, s_re, s_im = lam[k], lam[hblk + k], s[k], s[hblk + k]
            nxt.append((l_re * s_re - l_im * s_im + x_scr[k, rows, :],
                        l_re * s_im + l_im * s_re + x_scr[hblk + k, rows, :]))
        return tuple(p[0] for p in nxt) + tuple(p[1] for p in nxt)

    s0 = s0_ref[0, 0]
    s_fin = lax.fori_loop(0, nc, step, tuple(s0[:, k * LANES:(k + 1) * LANES] for k in range(nblk)),
                          unroll=min(nc, 4))
    for k in range(nblk):
        sfin_ref[0, 0, :, k * LANES:(k + 1) * LANES] = s_fin[k]

    prev = jnp.concatenate([prev_scr[k] for k in range(nblk)], axis=-1)
    if regroup:
        prev = pltpu.einshape("cnl->ncl", prev.reshape(nc, ns, SUPER_STATE)).reshape(ns * nc, SUPER_STATE)
    prev = prev.astype(BF16)
    blk = 2 * LANES
    for tb in range(S5_ROW // blk):
        cols = slice(tb * blk, (tb + 1) * blk)
        depth = (tb + 1) * blk
        y = _dot(u[:, 0:depth], toep[0:depth, cols]) + _dot(prev, wint[:, cols])
        y_ref[0, :, cols] = _gelu(y).astype(BF16)


def _s5(u2, weights, s0, ns, nc):
    kc, wc, ic, lam = weights
    tiles = s0.shape[1]
    r = ns * nc
    per_sg = lambda *dims: pl.BlockSpec((1,) + dims, lambda s, i: (s,) + (0,) * len(dims))
    rows = pl.BlockSpec((1, r, S5_ROW), lambda s, i: (s, i, 0))
    state = pl.BlockSpec((1, 1, ns, SUPER_STATE), lambda s, i: (s, i, 0, 0))
    return pl.pallas_call(
        functools.partial(_s5_body, ns=ns, nc=nc),
        grid=(SUPER, tiles),
        in_specs=[rows, per_sg(SSM_GROUP, S5_ROW), per_sg(2, S5_ROW, LANES), per_sg(2, SSM_STATE, S5_ROW),
                  per_sg(1, SUPER_STATE), state],
        out_specs=[rows, state],
        out_shape=[jax.ShapeDtypeStruct((SUPER, tiles * r, S5_ROW), BF16),
                   jax.ShapeDtypeStruct((SUPER, tiles, ns, SUPER_STATE), F32)],
        scratch_shapes=[pltpu.VMEM((S5_ROW, S5_ROW), BF16), pltpu.VMEM((S5_ROW, SUPER_STATE), BF16),
                        pltpu.VMEM((SUPER_STATE, S5_ROW), BF16),
                        pltpu.VMEM((SUPER_STATE // LANES, r, LANES), F32),
                        pltpu.VMEM((SUPER_STATE // LANES, r, LANES), F32)],
        compiler_params=_params("arbitrary", "arbitrary"),
        name="s5",
    )(u2, kc, wc, ic, lam, s0)


def _state_to_super(s_re, s_im, tiles, ns):
    n = s_re.shape[0]
    half = SUPER_STATE // 2
    s = jnp.concatenate([s_re.astype(F32).reshape(n, SUPER, half), s_im.astype(F32).reshape(n, SUPER, half)], -1)
    s = jnp.pad(jnp.transpose(s, (1, 0, 2)), ((0, 0), (0, tiles * ns - n), (0, 0)))
    return s.reshape(SUPER, tiles, ns, SUPER_STATE)


def _state_from_super(s, n):
    half = SUPER_STATE // 2
    s = jnp.transpose(s.reshape(SUPER, -1, SUPER_STATE)[:, :n], (1, 0, 2))
    return (s[:, :, :half].reshape(n, N_GROUPS, SSM_STATE), s[:, :, half:].reshape(n, N_GROUPS, SSM_STATE))


def _attend(qc, kb, vb, bias_ref, valid, skip=0):
    tq = qc.shape[0]
    left = lax.broadcasted_iota(jnp.int32, (tq, LANES), 1) < HEAD_DIM
    zero = jnp.zeros((), BF16)
    pairs = range(N_HEADS // 2)
    cols = [slice(hp * LANES, (hp + 1) * LANES) for hp in pairs]
    scores = []
    for hp in pairs:
        qp = qc[:, cols[hp]]
        q2 = jnp.concatenate([jnp.where(left, qp, zero), jnp.where(left, zero, qp)], axis=0)
        sc = lax.dot_general(q2, kb[:, cols[hp]], (((1,), (1,)), ((), ())), preferred_element_type=F32)
        sc = sc + bias_ref[hp, :, skip:]
        if valid is not None:
            sc = jnp.where(valid, sc, MASK_VALUE)
        scores.append(sc)
    probs, dens = [], []
    for sc in scores:
        e = jnp.exp2(sc - jnp.max(sc, axis=-1, keepdims=True))
        dens.append(jnp.sum(e, axis=-1, keepdims=True))
        probs.append(e.astype(BF16))
    outs = []
    for hp in pairs:
        o2 = _dot(probs[hp], vb[:, cols[hp]]) / dens[hp]
        outs.append(jnp.where(left, o2[:tq], o2[tq:]))
    return jnp.concatenate(outs, axis=-1)


def _attn_prompt_body(q_ref, k_ref, v_ref, bias_ref, o_ref, kpad, vpad):
    t = q_ref.shape[1]
    zeros = jnp.zeros((ATTN_REACH, D_ATTN), BF16)
    kpad[0:ATTN_REACH, :] = zeros
    vpad[0:ATTN_REACH, :] = zeros
    kpad[ATTN_REACH:ATTN_REACH + t, :] = k_ref[0]
    vpad[ATTN_REACH:ATTN_REACH + t, :] = v_ref[0]
    def chunk(c, skip, masked):
        r0 = c * CHUNK if isinstance(c, int) else pl.multiple_of(c * CHUNK, CHUNK)
        qc = q_ref[0, pl.ds(r0, CHUNK), :]
        kb = kpad[pl.ds(r0 + skip, BAND - skip), :]
        vb = vpad[pl.ds(r0 + skip, BAND - skip), :]
        valid = None
        if masked:
            col = lax.broadcasted_iota(jnp.int32, (2 * CHUNK, BAND - skip), 1) + skip
            valid = col >= ATTN_REACH - r0
        o_ref[0, pl.ds(r0, CHUNK), :] = _attend(qc, kb, vb, bias_ref, valid, skip).astype(BF16)

    n_chunks = t // CHUNK
    for c in range(min(LEFT_CHUNKS, n_chunks)):
        seen = (c + 1) * CHUNK
        skip = (BAND - seen) // LANES * LANES
        chunk(c, skip, BAND - skip != seen)
    lax.fori_loop(LEFT_CHUNKS, n_chunks, lambda c, _: chunk(c, 0, False), None, unroll=2)


def _attn_prompt(qkv, bias):
    n, t, _ = qkv.shape
    part = lambda j: pl.BlockSpec((1, t, D_ATTN), lambda i: (i, 0, j))
    return pl.pallas_call(
        _attn_prompt_body,
        grid=(n,),
        in_specs=[part(0), part(1), part(2), _const_spec((N_HEADS // 2, 2 * CHUNK, BAND))],
        out_specs=pl.BlockSpec((1, t, D_ATTN), lambda i: (i, 0, 0)),
        out_shape=jax.ShapeDtypeStruct((n, t, D_ATTN), BF16),
        scratch_shapes=[pltpu.VMEM((ATTN_REACH + t, D_ATTN), BF16), pltpu.VMEM((ATTN_REACH + t, D_ATTN), BF16)],
        compiler_params=_params("arbitrary"),
        name="attn_prompt",
    )(qkv, qkv, qkv, bias)


def _attn_sample_body(q_ref, k_ref, v_ref, bias_ref, o_ref):
    o_ref[0] = _attend(q_ref[0], k_ref[0], v_ref[0], bias_ref, None).astype(BF16)


def _attn_sample(q, kk, vv, bias):
    n, tq, _ = q.shape
    tk = kk.shape[1]
    return pl.pallas_call(
        _attn_sample_body,
        grid=(n,),
        in_specs=[pl.BlockSpec((1, tq, D_ATTN), lambda i: (i, 0, 0)),
                  pl.BlockSpec((1, tk, D_ATTN), lambda i: (i, 0, 0)),
                  pl.BlockSpec((1, tk, D_ATTN), lambda i: (i, 0, 0)),
                  _const_spec((N_HEADS // 2, 2 * tq, tk))],
        out_specs=pl.BlockSpec((1, tq, D_ATTN), lambda i: (i, 0, 0)),
        out_shape=jax.ShapeDtypeStruct((n, tq, D_ATTN), BF16),
        compiler_params=_params("arbitrary"),
        name="attn_sample",
    )(q, kk, vv, bias)


def _rel_bias(table, tq, tk):
    width = tq + tk - 1
    z = np.arange(width)
    idx = np.clip(ATTN_REACH - np.where(z < tk, z, z - width), -MAX_REL, MAX_REL) + MAX_REL
    ext = table.astype(F32)[:, idx]
    flat = jnp.tile(ext, (1, tq))[:, :tq * (width - 1)]
    bias = flat.reshape(N_HEADS, tq, width - 1)[:, :, :tk]
    return bias.reshape(N_HEADS // 2, 2 * tq, tk)


def _merge_stage(x_ref, ys_ref, att_ref, gate_ref, wglu, watt, wo):
    tm = x_ref.shape[1]
    br_att = _dot(att_ref[0], watt[...])
    blocks = []
    for s in range(SUPER):
        frames = jnp.stack([ys_ref[s, :, j * LANES:(j + 1) * LANES].astype(F32) for j in range(S5_CHUNK)], axis=0)
        blocks.append(pltpu.einshape("jcl->cjl", frames).reshape(tm, LANES))
    ys = jnp.concatenate(blocks, axis=-1).astype(BF16)

    def mix_block(k):
        cols = slice(k * MERGE_BLOCK, (k + 1) * MERGE_BLOCK)
        gate_cols = slice(D_MODEL + k * MERGE_BLOCK, D_MODEL + (k + 1) * MERGE_BLOCK)
        br_ssm = _dot(ys, wglu[:, cols]) * _sigmoid(_dot(ys, wglu[:, gate_cols]))
        mix = (_sigmoid(gate_ref[0, :, cols].astype(F32)) * br_ssm
               + _sigmoid(gate_ref[0, :, gate_cols].astype(F32)) * br_att[:, cols])
        return mix.astype(BF16)

    n_blocks = D_MODEL // MERGE_BLOCK
    h = x_ref[0]
    pending = mix_block(0)
    for k in range(n_blocks):
        nxt = mix_block(k + 1) if k + 1 < n_blocks else None
        h = h + _dot(pending, wo[k * MERGE_BLOCK:(k + 1) * MERGE_BLOCK, :])
        pending = nxt
    return h


def _convglu_stage(h, hn, a_scr, wup, cw, cb, wdown, gfin, history):
    tm = h.shape[0]
    acc = jnp.zeros((tm, D_MODEL), F32)
    for lo, hi in FF_CHUNKS:
        w = hi - lo
        a = _dot(hn, wup[:, lo:hi])
        b = _dot(hn, wup[:, D_FF + lo:D_FF + hi])
        a_scr[8:8 + tm, 0:w] = a
        a1, a2 = history(a, lo, hi)
        c = cb[:, lo:hi] + cw[0:1, lo:hi] * a2 + cw[1:2, lo:hi] * a1 + cw[2:3, lo:hi] * a
        act = (_gelu(c) * b).astype(BF16)
        acc = acc + _dot(act, wdown[lo:hi, :])
    return _rms(h + acc, gfin[...])


def _ffn_sample_body(x_ref, ys_ref, att_ref, gate_ref, h1_ref, h2_ref, wglu, watt, wo, gffn, wup, cw, cb, wdown,
                     gfin, y_ref, conv_ref, a_scr, *, seq_len):
    tm = x_ref.shape[1]
    h = _merge_stage(x_ref, ys_ref, att_ref, gate_ref, wglu, watt, wo)
    hn = _rms(h, gffn[...]).astype(BF16)
    pos = lax.broadcasted_iota(jnp.int32, (tm, 1), 0) % seq_len

    def history(a, lo, hi):
        w = hi - lo
        a_scr[6:8, 0:w] = jnp.zeros((2, w), F32)
        conv_ref[0, :, lo:hi] = a
        a1 = jnp.where(pos < 1, h1_ref[:, lo:hi], a_scr[7:7 + tm, 0:w])
        a2 = jnp.where(pos < 2, h2_ref[:, lo:hi], a_scr[6:6 + tm, 0:w])
        return a1, a2

    y_ref[0] = _convglu_stage(h, hn, a_scr, wup, cw, cb, wdown, gfin, history)


def _ffn_prompt_body(x_ref, ys_ref, att_ref, gate_ref, wglu, watt, wo, gffn, wup, cw, cb, wdown, gfin,
                     y_ref, conv_ref, a_scr, carry):
    tm = x_ref.shape[1]

    @pl.when(pl.program_id(1) == 0)
    def _():
        carry[...] = jnp.zeros(carry.shape, F32)

    h = _merge_stage(x_ref, ys_ref, att_ref, gate_ref, wglu, watt, wo)
    hn = _rms(h, gffn[...]).astype(BF16)

    def history(a, lo, hi):
        w = hi - lo
        a_scr[6:8, 0:w] = carry[:, lo:hi]
        carry[:, lo:hi] = a[tm - 2:tm, :]
        return a_scr[7:7 + tm, 0:w], a_scr[6:6 + tm, 0:w]

    y_ref[0] = _convglu_stage(h, hn, a_scr, wup, cw, cb, wdown, gfin, history)

    @pl.when(pl.program_id(1) == pl.num_programs(1) - 1)
    def _():
        conv_ref[0] = carry[...]


def _ffn_weight_specs():
    return [_const_spec((D_SSM, 2 * D_MODEL)), _const_spec((D_ATTN, D_MODEL)), _const_spec((D_MODEL, D_MODEL)),
            _const_spec((1, D_MODEL)), _const_spec((D_MODEL, 2 * D_FF)), _const_spec((CONV_W, D_FF)),
            _const_spec((1, D_FF)), _const_spec((D_FF, D_MODEL)), _const_spec((1, D_MODEL))]


def _ffn_prompt(x, ys, att, gates, wts, tm):
    n, t, _ = x.shape
    tiles = t // tm
    tok = lambda w: pl.BlockSpec((1, tm, w), lambda i, j: (i, j, 0))
    ys_spec = pl.BlockSpec((SUPER, tm // S5_CHUNK, S5_ROW), lambda i, j: (0, i * tiles + j, 0))
    seq = pl.BlockSpec((1, CONV_W - 1, D_FF), lambda i, j: (i, 0, 0))
    fc = max(hi - lo for lo, hi in FF_CHUNKS)
    return pl.pallas_call(
        _ffn_prompt_body,
        grid=(n, tiles),
        in_specs=[tok(D_MODEL), ys_spec, tok(D_ATTN), tok(2 * D_MODEL)] + _ffn_weight_specs(),
        out_specs=[tok(D_MODEL), seq],
        out_shape=[jax.ShapeDtypeStruct((n, t, D_MODEL), F32),
                   jax.ShapeDtypeStruct((n, CONV_W - 1, D_FF), F32)],
        scratch_shapes=[pltpu.VMEM((tm + 8, fc), F32), pltpu.VMEM((CONV_W - 1, D_FF), F32)],
        compiler_params=_params("arbitrary", "arbitrary"),
        name="merge_ffn_prompt",
    )(x, ys, att, gates, *wts)


def _ffn_sample(x, ys, att, gates, h1, h2, wts, seq_len):
    rows = x.shape[1]
    tok = lambda w: pl.BlockSpec((1, rows, w), lambda i: (0, 0, 0))
    ys_spec = pl.BlockSpec((SUPER, rows // S5_CHUNK, S5_ROW), lambda i: (0, 0, 0))
    full = pl.BlockSpec((rows, D_FF), lambda i: (0, 0))
    fc = max(hi - lo for lo, hi in FF_CHUNKS)
    return pl.pallas_call(
        functools.partial(_ffn_sample_body, seq_len=seq_len),
        grid=(1,),
        in_specs=[tok(D_MODEL), ys_spec, tok(D_ATTN), tok(2 * D_MODEL), full, full] + _ffn_weight_specs(),
        out_specs=[tok(D_MODEL), tok(D_FF)],
        out_shape=[jax.ShapeDtypeStruct((1, rows, D_MODEL), F32),
                   jax.ShapeDtypeStruct((1, rows, D_FF), F32)],
        scratch_shapes=[pltpu.VMEM((rows + 8, fc), F32)],
        compiler_params=_params("arbitrary"),
        name="merge_ffn_sample",
    )(x, ys, att, gates, h1, h2, *wts)


PROMPT_TOKEN_TILE = 1024
FFN_TOKEN_TILE = 512
SAMPLE_STREAM_PAD = 16


def kernel(x_prompt, x_sample, state_ssm_re, state_ssm_im, cache_attn_k, cache_attn_v, cache_conv,
           g_mix, w_in, ssm_lambda_re, ssm_lambda_im, ssm_log_dt, ssm_b_re, ssm_b_im,
           ssm_c_re, ssm_c_im, ssm_d, w_ssm_glu, attn_rel_bias, w_attn_up, w_o,
           g_ffn, w_up, conv_w, conv_b, w_down, g_final):
    depth = w_in.shape[0]
    nb, seq, _ = x_prompt.shape
    db, dseq, _ = x_sample.shape
    kept = min(ATTN_REACH, seq)
    assert depth == 1 and seq % PROMPT_TOKEN_TILE == 0 and seq % FFN_TOKEN_TILE == 0
    assert seq % S5_CHUNK == 0 and dseq == S5_CHUNK and nb % S5_PROMPT_SEQS == 0 and db <= SAMPLE_STREAM_PAD
    l = 0

    log2e = math.log2(math.e)
    col_scale = jnp.ones((D_IN,), F32).at[D_SSM:D_SSM + D_ATTN].set(HEAD_DIM ** -0.5 * log2e)
    w_in_b = (w_in[l] * col_scale[None, :]).astype(BF16)
    wts = (w_ssm_glu[l].astype(BF16), w_attn_up[l].astype(BF16), w_o[l].astype(BF16), g_ffn[l][None, :],
           w_up[l].astype(BF16), conv_w[l], conv_b[l][None, :], w_down[l].astype(BF16), g_final[None, :])
    g_mix_l = g_mix[l][None, :]
    w_s5 = _s5_weights(ssm_lambda_re[l], ssm_lambda_im[l], ssm_log_dt[l], ssm_b_re[l], ssm_b_im[l],
                       ssm_c_re[l], ssm_c_im[l], ssm_d[l])
    table = attn_rel_bias[l].astype(F32) * log2e
    bias_p = _rel_bias(table, CHUNK, BAND)
    w_cache = cache_attn_k.shape[2]
    assert w_cache == ATTN_REACH
    bias_s = _rel_bias(table, dseq, w_cache + dseq)

    u, qkv, gates, k32, v32 = _inproj(x_prompt, g_mix_l, w_in_b, PROMPT_TOKEN_TILE, kept)
    n_chunks = seq // S5_CHUNK
    tiles = nb // S5_PROMPT_SEQS
    zero_state = jnp.zeros((SUPER, tiles, S5_PROMPT_SEQS, SUPER_STATE), F32)
    ys, p_state = _s5(u, w_s5, zero_state, S5_PROMPT_SEQS, n_chunks)
    p_re, p_im = _state_from_super(p_state, nb)
    att = _attn_prompt(qkv, bias_p)
    y_prompt, p_conv = _ffn_prompt(x_prompt, ys, att, gates, wts, FFN_TOKEN_TILE)

    rows = db * dseq
    xs = x_sample.reshape(1, rows, D_MODEL)
    us, qkvs, gates_s, ks32, vs32 = _inproj(xs, g_mix_l, w_in_b, rows, rows)
    us2 = jnp.pad(us, ((0, 0), (0, SAMPLE_STREAM_PAD - db), (0, 0)))
    s0 = _state_to_super(state_ssm_re[l], state_ssm_im[l], 1, SAMPLE_STREAM_PAD)
    yss, s_state = _s5(us2, w_s5, s0, SAMPLE_STREAM_PAD, 1)
    s_re, s_im = _state_from_super(s_state, db)
    yss = yss[:, :db]
    qkvs3 = qkvs.reshape(db, dseq, 3 * D_ATTN)
    kk = jnp.concatenate([cache_attn_k[l].reshape(db, w_cache, D_ATTN).astype(BF16),
                          qkvs3[:, :, D_ATTN:2 * D_ATTN]], axis=1)
    vv = jnp.concatenate([cache_attn_v[l].reshape(db, w_cache, D_ATTN).astype(BF16),
                          qkvs3[:, :, 2 * D_ATTN:]], axis=1)
    att_s = _attn_sample(qkvs3[:, :, :D_ATTN], kk, vv, bias_s)
    hist = cache_conv[l].astype(F32)
    pad_rows = ((0, 0), (0, dseq - 1), (0, 0))
    h1 = jnp.pad(hist[:, 1:2], pad_rows).reshape(rows, D_FF)
    h2 = jnp.pad(hist, ((0, 0), (0, dseq - 2), (0, 0))).reshape(rows, D_FF)
    y_s, a_s = _ffn_sample(xs, yss, att_s.reshape(1, rows, D_ATTN), gates_s, h1, h2, wts, dseq)
    s_conv = a_s.reshape(db, dseq, D_FF)[:, dseq - (CONV_W - 1):]

    heads = lambda a, n, t: a.reshape(1, n, t, N_HEADS, HEAD_DIM)
    return (y_prompt, y_s.reshape(db, dseq, D_MODEL),
            p_re[None], p_im[None],
            heads(k32, nb, kept), heads(v32, nb, kept), p_conv[None],
            s_re[None], s_im[None],
            heads(ks32, db, dseq), heads(vs32, db, dseq), s_conv[None])
```

```python
import functools
import math

import jax
import jax.numpy as jnp
import numpy as np
from jax import lax
from jax.experimental import pallas as pl
from jax.experimental.pallas import tpu as pltpu

F32 = jnp.float32
BF16 = jnp.bfloat16

D_MODEL = 1024
CHUNK = 64
LEFT_CHUNKS = 8
ATTN_REACH = LEFT_CHUNKS * CHUNK
BAND = ATTN_REACH + CHUNK
D_SSM = D_MODEL // 2
D_ATTN = D_MODEL // 2
HEAD_DIM = 64
N_HEADS = D_ATTN // HEAD_DIM
MAX_REL = 128
SSM_GROUP = 16
N_GROUPS = D_SSM // SSM_GROUP
SSM_STATE = 64
D_FF = 2816
CONV_W = 3
D_IN = D_SSM + 3 * D_ATTN + 2 * D_MODEL
RMS_EPS = 1e-6
MASK_VALUE = -1e30

V7X_VMEM_LIMIT_BYTES = 56 * 1024 * 1024
LANES = 128
MERGE_BLOCK = 256
FF_CHUNKS = ((0, 2816),)


def _params(*sem):
    return pltpu.CompilerParams(dimension_semantics=sem, vmem_limit_bytes=V7X_VMEM_LIMIT_BYTES)


def _const_spec(shape):
    zeros = (0,) * len(shape)
    return pl.BlockSpec(shape, lambda *_: zeros, pipeline_mode=pl.Buffered(1))


def _rms(x, g):
    return x * lax.rsqrt(jnp.mean(x * x, axis=-1, keepdims=True) + RMS_EPS) * g


def _gelu(x):
    k = math.sqrt(2.0 / math.pi)
    return (0.5 * x) * (1.0 + jnp.tanh(x * (k + (k * 0.044715) * (x * x))))


def _sigmoid(x):
    return 1.0 / (1.0 + jnp.exp(-x))


def _dot(a, b):
    return jnp.dot(a, b, preferred_element_type=F32)


S5_CHUNK = 16
SUPER = D_SSM // LANES
SUPER_GROUPS = N_GROUPS // SUPER
SUPER_STATE = 2 * SUPER_GROUPS * SSM_STATE
S5_ROW = S5_CHUNK * LANES
S5_PROMPT_SEQS = 8


def _inproj_body(x_ref, g_ref, w_ref, u_ref, qkv_ref, gate_ref, k32_ref, v32_ref):
    xn = _rms(x_ref[0], g_ref[...]).astype(BF16)

    def seg(lo, hi):
        return _dot(xn, w_ref[:, lo:hi])

    u = seg(0, D_SSM)
    n_rows = u.shape[0] // S5_CHUNK
    for s in range(SUPER):
        frames = u[:, s * LANES:(s + 1) * LANES].reshape(n_rows, S5_CHUNK, LANES)
        frames = jnp.swapaxes(frames, 0, 1)
        for j in range(S5_CHUNK):
            u_ref[s, :, j * LANES:(j + 1) * LANES] = frames[j].astype(BF16)
    q0 = D_SSM
    qkv_ref[0, :, 0:D_ATTN] = seg(q0, q0 + D_ATTN).astype(BF16)
    k = seg(q0 + D_ATTN, q0 + 2 * D_ATTN)
    v = seg(q0 + 2 * D_ATTN, q0 + 3 * D_ATTN)
    qkv_ref[0, :, D_ATTN:2 * D_ATTN] = k.astype(BF16)
    qkv_ref[0, :, 2 * D_ATTN:3 * D_ATTN] = v.astype(BF16)
    g0 = q0 + 3 * D_ATTN
    gate_ref[0, :, 0:D_MODEL] = seg(g0, g0 + D_MODEL).astype(BF16)
    gate_ref[0, :, D_MODEL:2 * D_MODEL] = seg(g0 + D_MODEL, g0 + 2 * D_MODEL).astype(BF16)

    @pl.when(pl.program_id(1) == pl.num_programs(1) - 1)
    def _():
        tm, kept = k.shape[0], k32_ref.shape[1]
        k32_ref[0] = k[tm - kept:, :]
        v32_ref[0] = v[tm - kept:, :]


def _inproj(x, g_mix, w_in_b, tm, kept):
    n, t, _ = x.shape
    assert kept <= tm
    tiles = t // tm
    n_rows = tm // S5_CHUNK
    tok = lambda w: pl.BlockSpec((1, tm, w), lambda i, j: (i, j, 0))
    keep = pl.BlockSpec((1, kept, D_ATTN), lambda i, j: (i, 0, 0))
    u_spec = pl.BlockSpec((SUPER, n_rows, S5_ROW), lambda i, j: (0, i * tiles + j, 0))
    return pl.pallas_call(
        _inproj_body,
        grid=(n, tiles),
        in_specs=[tok(D_MODEL), _const_spec((1, D_MODEL)), _const_spec((D_MODEL, D_IN))],
        out_specs=[u_spec, tok(3 * D_ATTN), tok(2 * D_MODEL), keep, keep],
        out_shape=[
            jax.ShapeDtypeStruct((SUPER, n * tiles * n_rows, S5_ROW), BF16),
            jax.ShapeDtypeStruct((n, t, 3 * D_ATTN), BF16),
            jax.ShapeDtypeStruct((n, t, 2 * D_MODEL), BF16),
            jax.ShapeDtypeStruct((n, kept, D_ATTN), F32),
            jax.ShapeDtypeStruct((n, kept, D_ATTN), F32),
        ],
        compiler_params=_params("arbitrary", "arbitrary"),
        name="inproj",
    )(x, g_mix, w_in_b)


def _s5_weights(lam_re, lam_im, log_dt, b_re, b_im, c_re, c_im, d_skip):
    L = S5_CHUNK
    hi = lax.Precision.HIGHEST
    f = lambda a: a.astype(F32)
    cmul = lambda ar, ai, br, bi: (ar * br - ai * bi, ar * bi + ai * br)
    a_re, a_im = f(lam_re), f(lam_im)
    dt = jnp.exp(f(log_dt))[:, None]
    tau = jnp.arange(L + 1, dtype=F32)[None, :, None]
    mag = jnp.exp((a_re * dt)[:, None, :] * tau)
    ang = (a_im * dt)[:, None, :] * tau
    pw_re, pw_im = mag * jnp.cos(ang), mag * jnp.sin(ang)
    n_re, n_im = pw_re[:, 1] - 1.0, pw_im[:, 1]
    den = a_re * a_re + a_im * a_im
    f_re, f_im = (n_re * a_re + n_im * a_im) / den, (n_im * a_re - n_re * a_im) / den
    bb_re, bb_im = cmul(f_re[..., None], f_im[..., None], f(b_re), f(b_im))
    cp_re, cp_im = cmul(f(c_re)[:, :, None, :], f(c_im)[:, :, None, :], pw_re[:, None], pw_im[:, None])
    kt = (jnp.einsum('gatp,gph->ghta', cp_re[:, :, :L], bb_re, precision=hi)
          - jnp.einsum('gatp,gph->ghta', cp_im[:, :, :L], bb_im, precision=hi))
    first = (np.arange(L) == 0).astype(np.float32)[None, None, :, None]
    kt = kt + first * (f(d_skip)[:, :, None] * jnp.eye(SSM_GROUP, dtype=F32)[None])[:, :, None, :]
    rev = L - 1 - np.arange(L)
    bt_re, bt_im = jnp.swapaxes(bb_re, 1, 2)[:, None], jnp.swapaxes(bb_im, 1, 2)[:, None]
    wst_re, wst_im = cmul(pw_re[:, rev][:, :, None, :], pw_im[:, rev][:, :, None, :], bt_re, bt_im)

    sg = lambda a: a.reshape((SUPER, SUPER_GROUPS) + a.shape[1:])
    kc = jnp.transpose(sg(kt), (0, 2, 3, 1, 4)).reshape(SUPER, SSM_GROUP, S5_ROW)
    even = (np.arange(SUPER_GROUPS) % 2 == 0)[None, None, :, None, None]

    def in_form(w):
        w = jnp.transpose(sg(w), (0, 2, 1, 3, 4))
        w = jnp.concatenate([jnp.where(even, w, 0.0), jnp.where(even, 0.0, w)], axis=-1)
        return w.reshape(SUPER, S5_ROW, 2 * SSM_STATE)

    def out_form(c):
        return jnp.transpose(sg(c), (0, 4, 3, 1, 2)).reshape(SUPER, SSM_STATE, S5_ROW)

    wc = jnp.stack([in_form(wst_re), in_form(wst_im)], axis=1)
    ic = jnp.stack([out_form(cp_re[:, :, 1:L + 1]), out_form(-cp_im[:, :, 1:L + 1])], axis=1)
    half = SUPER_STATE // 2
    lam = jnp.concatenate([pw_re[:, L].reshape(SUPER, half), pw_im[:, L].reshape(SUPER, half)], axis=-1)
    return kc.astype(BF16), wc.astype(BF16), ic.astype(BF16), lam[:, None, :]


def _s5_body(u_ref, kc_ref, wc_ref, ic_ref, lam_ref, s0_ref, y_ref, sfin_ref, toep, wst, wint, x_scr, prev_scr,
             *, ns, nc):
    half = SUPER_STATE // 2

    @pl.when(pl.program_id(1) == 0)
    def _():
        zero = jnp.zeros((), BF16)
        group_of = lambda shape, dim: (lax.broadcasted_iota(jnp.int32, shape, dim) % LANES) // SSM_GROUP
        col_group = group_of((SSM_GROUP, S5_ROW), 1)
        kc = kc_ref[0]
        bblk = jnp.concatenate([jnp.where(col_group == g, kc, zero) for g in range(SUPER_GROUPS)], axis=0)
        for j in range(S5_CHUNK):
            rows = slice(j * LANES, (j + 1) * LANES)
            if j:
                toep[rows, 0:j * LANES] = jnp.zeros((LANES, j * LANES), BF16)
            toep[rows, j * LANES:S5_ROW] = bblk[:, 0:S5_ROW - j * LANES]
        row_pair = group_of((S5_ROW, LANES), 0) // 2
        for part in range(2):
            w = wc_ref[0, part]
            for k in range(SUPER_GROUPS // 2):
                c0 = part * half + k * LANES
                wst[:, c0:c0 + LANES] = jnp.where(row_pair == k, w, zero)
        col_group = group_of((SSM_STATE, S5_ROW), 1)
        for part in range(2):
            c = ic_ref[0, part]
            for g in range(SUPER_GROUPS):
                r0 = part * half + g * SSM_STATE
                wint[r0:r0 + SSM_STATE, :] = jnp.where(col_group == g, c, zero)

    u = u_ref[0]
    x = _dot(u, wst[...])
    regroup = nc > 1
    if regroup:
        x = jnp.swapaxes(x.reshape(ns, nc, SUPER_STATE), 0, 1).reshape(ns * nc, SUPER_STATE)
    nblk = SUPER_STATE // LANES
    hblk = nblk // 2
    for k in range(nblk):
        x_scr[k] = x[:, k * LANES:(k + 1) * LANES]
    lam = [lam_ref[0, :, k * LANES:(k + 1) * LANES] for k in range(nblk)]

    def step(c, s):
        rows = pl.ds(pl.multiple_of(c * ns, ns), ns)
        nxt = []
        for k in range(nblk):
            prev_scr[k, rows, :] = s[k]
        for k in range(hblk):
            l_re, l_im, s_re, s_im = lam[k], lam[hblk + k], s[k], s[hblk + k]
            nxt.append((l_re * s_re - l_im * s_im + x_scr[k, rows, :],
                        l_re * s_im + l_im * s_re + x_scr[hblk + k, rows, :]))
        return tuple(p[0] for p in nxt) + tuple(p[1] for p in nxt)

    s0 = s0_ref[0, 0]
    s_fin = lax.fori_loop(0, nc, step, tuple(s0[:, k * LANES:(k + 1) * LANES] for k in range(nblk)),
                          unroll=min(nc, 4))
    for k in range(nblk):
        sfin_ref[0, 0, :, k * LANES:(k + 1) * LANES] = s_fin[k]

    prev = jnp.concatenate([prev_scr[k] for k in range(nblk)], axis=-1)
    if regroup:
        prev = jnp.swapaxes(prev.reshape(nc, ns, SUPER_STATE), 0, 1).reshape(ns * nc, SUPER_STATE)
    prev = prev.astype(BF16)
    blk = 2 * LANES
    for tb in range(S5_ROW // blk):
        cols = slice(tb * blk, (tb + 1) * blk)
        depth = (tb + 1) * blk
        y = _dot(u[:, 0:depth], toep[0:depth, cols]) + _dot(prev, wint[:, cols])
        y_ref[0, :, cols] = _gelu(y).astype(BF16)


def _s5(u2, weights, s0, ns, nc):
    kc, wc, ic, lam = weights
    tiles = s0.shape[1]
    r = ns * nc
    per_sg = lambda *dims: pl.BlockSpec((1,) + dims, lambda s, i: (s,) + (0,) * len(dims))
    rows = pl.BlockSpec((1, r, S5_ROW), lambda s, i: (s, i, 0))
    state = pl.BlockSpec((1, 1, ns, SUPER_STATE), lambda s, i: (s, i, 0, 0))
    return pl.pallas_call(
        functools.partial(_s5_body, ns=ns, nc=nc),
        grid=(SUPER, tiles),
        in_specs=[rows, per_sg(SSM_GROUP, S5_ROW), per_sg(2, S5_ROW, LANES), per_sg(2, SSM_STATE, S5_ROW),
                  per_sg(1, SUPER_STATE), state],
        out_specs=[rows, state],
        out_shape=[jax.ShapeDtypeStruct((SUPER, tiles * r, S5_ROW), BF16),
                   jax.ShapeDtypeStruct((SUPER, tiles, ns, SUPER_STATE), F32)],
        scratch_shapes=[pltpu.VMEM((S5_ROW, S5_ROW), BF16), pltpu.VMEM((S5_ROW, SUPER_STATE), BF16),
                        pltpu.VMEM((SUPER_STATE, S5_ROW), BF16),
                        pltpu.VMEM((SUPER_STATE // LANES, r, LANES), F32),
                        pltpu.VMEM((SUPER_STATE // LANES, r, LANES), F32)],
        compiler_params=_params("arbitrary", "arbitrary"),
        name="s5",
    )(u2, kc, wc, ic, lam, s0)


def _state_to_super(s_re, s_im, tiles, ns):
    n = s_re.shape[0]
    half = SUPER_STATE // 2
    s = jnp.concatenate([s_re.astype(F32).reshape(n, SUPER, half), s_im.astype(F32).reshape(n, SUPER, half)], -1)
    s = jnp.pad(jnp.transpose(s, (1, 0, 2)), ((0, 0), (0, tiles * ns - n), (0, 0)))
    return s.reshape(SUPER, tiles, ns, SUPER_STATE)


def _state_from_super(s, n):
    half = SUPER_STATE // 2
    s = jnp.transpose(s.reshape(SUPER, -1, SUPER_STATE)[:, :n], (1, 0, 2))
    return (s[:, :, :half].reshape(n, N_GROUPS, SSM_STATE), s[:, :, half:].reshape(n, N_GROUPS, SSM_STATE))


def _attend(qc, kb, vb, bias_ref, valid, skip=0):
    tq = qc.shape[0]
    left = lax.broadcasted_iota(jnp.int32, (tq, LANES), 1) < HEAD_DIM
    zero = jnp.zeros((), BF16)
    pairs = range(N_HEADS // 2)
    cols = [slice(hp * LANES, (hp + 1) * LANES) for hp in pairs]
    scores = []
    for hp in pairs:
        qp = qc[:, cols[hp]]
        q2 = jnp.concatenate([jnp.where(left, qp, zero), jnp.where(left, zero, qp)], axis=0)
        sc = lax.dot_general(q2, kb[:, cols[hp]], (((1,), (1,)), ((), ())), preferred_element_type=F32)
        sc = sc + bias_ref[hp, :, skip:]
        if valid is not None:
            sc = jnp.where(valid, sc, MASK_VALUE)
        scores.append(sc)
    probs, dens = [], []
    for sc in scores:
        e = jnp.exp2(sc - jnp.max(sc, axis=-1, keepdims=True))
        dens.append(jnp.sum(e, axis=-1, keepdims=True))
        probs.append(e.astype(BF16))
    outs = []
    for hp in pairs:
        o2 = _dot(probs[hp], vb[:, cols[hp]]) / dens[hp]
        outs.append(jnp.where(left, o2[:tq], o2[tq:]))
    return jnp.concatenate(outs, axis=-1)


def _attn_prompt_body(q_ref, k_ref, v_ref, bias_ref, o_ref, kpad, vpad):
    t = q_ref.shape[1]
    zeros = jnp.zeros((ATTN_REACH, D_ATTN), BF16)
    kpad[0:ATTN_REACH, :] = zeros
    vpad[0:ATTN_REACH, :] = zeros
    kpad[ATTN_REACH:ATTN_REACH + t, :] = k_ref[0]
    vpad[ATTN_REACH:ATTN_REACH + t, :] = v_ref[0]
    def chunk(c, skip, masked):
        r0 = c * CHUNK if isinstance(c, int) else pl.multiple_of(c * CHUNK, CHUNK)
        qc = q_ref[0, pl.ds(r0, CHUNK), :]
        kb = kpad[pl.ds(r0 + skip, BAND - skip), :]
        vb = vpad[pl.ds(r0 + skip, BAND - skip), :]
        valid = None
        if masked:
            col = lax.broadcasted_iota(jnp.int32, (2 * CHUNK, BAND - skip), 1) + skip
            valid = col >= ATTN_REACH - r0
        o_ref[0, pl.ds(r0, CHUNK), :] = _attend(qc, kb, vb, bias_ref, valid, skip).astype(BF16)

    n_chunks = t // CHUNK
    for c in range(min(LEFT_CHUNKS, n_chunks)):
        seen = (c + 1) * CHUNK
        skip = (BAND - seen) // LANES * LANES
        chunk(c, skip, BAND - skip != seen)
    lax.fori_loop(LEFT_CHUNKS, n_chunks, lambda c, _: chunk(c, 0, False), None, unroll=4)


def _attn_prompt(qkv, bias):
    n, t, _ = qkv.shape
    part = lambda j: pl.BlockSpec((1, t, D_ATTN), lambda i: (i, 0, j))
    return pl.pallas_call(
        _attn_prompt_body,
        grid=(n,),
        in_specs=[part(0), part(1), part(2), _const_spec((N_HEADS // 2, 2 * CHUNK, BAND))],
        out_specs=pl.BlockSpec((1, t, D_ATTN), lambda i: (i, 0, 0)),
        out_shape=jax.ShapeDtypeStruct((n, t, D_ATTN), BF16),
        scratch_shapes=[pltpu.VMEM((ATTN_REACH + t, D_ATTN), BF16), pltpu.VMEM((ATTN_REACH + t, D_ATTN), BF16)],
        compiler_params=_params("arbitrary"),
        name="attn_prompt",
    )(qkv, qkv, qkv, bias)


def _attn_sample_body(q_ref, k_ref, v_ref, bias_ref, o_ref):
    o_ref[0] = _attend(q_ref[0], k_ref[0], v_ref[0], bias_ref, None).astype(BF16)


def _attn_sample(q, kk, vv, bias):
    n, tq, _ = q.shape
    tk = kk.shape[1]
    return pl.pallas_call(
        _attn_sample_body,
        grid=(n,),
        in_specs=[pl.BlockSpec((1, tq, D_ATTN), lambda i: (i, 0, 0)),
                  pl.BlockSpec((1, tk, D_ATTN), lambda i: (i, 0, 0)),
                  pl.BlockSpec((1, tk, D_ATTN), lambda i: (i, 0, 0)),
                  _const_spec((N_HEADS // 2, 2 * tq, tk))],
        out_specs=pl.BlockSpec((1, tq, D_ATTN), lambda i: (i, 0, 0)),
        out_shape=jax.ShapeDtypeStruct((n, tq, D_ATTN), BF16),
        compiler_params=_params("arbitrary"),
        name="attn_sample",
    )(q, kk, vv, bias)


def _rel_bias(table, tq, tk):
    width = tq + tk - 1
    z = np.arange(width)
    idx = np.clip(ATTN_REACH - np.where(z < tk, z, z - width), -MAX_REL, MAX_REL) + MAX_REL
    ext = table.astype(F32)[:, idx]
    flat = jnp.tile(ext, (1, tq))[:, :tq * (width - 1)]
    bias = flat.reshape(N_HEADS, tq, width - 1)[:, :, :tk]
    return bias.reshape(N_HEADS // 2, 2 * tq, tk)


def _merge_stage(x_ref, ys_ref, att_ref, gate_ref, wglu, watt, wo):
    tm = x_ref.shape[1]
    br_att = _dot(att_ref[0], watt[...])
    blocks = []
    for s in range(SUPER):
        frames = jnp.stack([ys_ref[s, :, j * LANES:(j + 1) * LANES].astype(F32) for j in range(S5_CHUNK)], axis=0)
        blocks.append(jnp.swapaxes(frames, 0, 1).reshape(tm, LANES))
    ys = jnp.concatenate(blocks, axis=-1).astype(BF16)

    def mix_block(k):
        cols = slice(k * MERGE_BLOCK, (k + 1) * MERGE_BLOCK)
        gate_cols = slice(D_MODEL + k * MERGE_BLOCK, D_MODEL + (k + 1) * MERGE_BLOCK)
        br_ssm = _dot(ys, wglu[:, cols]) * _sigmoid(_dot(ys, wglu[:, gate_cols]))
        mix = (_sigmoid(gate_ref[0, :, cols].astype(F32)) * br_ssm
               + _sigmoid(gate_ref[0, :, gate_cols].astype(F32)) * br_att[:, cols])
        return mix.astype(BF16)

    n_blocks = D_MODEL // MERGE_BLOCK
    h = x_ref[0]
    pending = mix_block(0)
    for k in range(n_blocks):
        nxt = mix_block(k + 1) if k + 1 < n_blocks else None
        h = h + _dot(pending, wo[k * MERGE_BLOCK:(k + 1) * MERGE_BLOCK, :])
        pending = nxt
    return h


def _convglu_stage(h, hn, a_scr, wup, cw, cb, wdown, gfin, history):
    tm = h.shape[0]
    acc = jnp.zeros((tm, D_MODEL), F32)
    for lo, hi in FF_CHUNKS:
        w = hi - lo
        a = _dot(hn, wup[:, lo:hi])
        b = _dot(hn, wup[:, D_FF + lo:D_FF + hi])
        a_scr[8:8 + tm, 0:w] = a
        a1, a2 = history(a, lo, hi)
        c = cb[:, lo:hi] + cw[0:1, lo:hi] * a2 + cw[1:2, lo:hi] * a1 + cw[2:3, lo:hi] * a
        act = (_gelu(c) * b).astype(BF16)
        acc = acc + _dot(act, wdown[lo:hi, :])
    return _rms(h + acc, gfin[...])


def _ffn_sample_body(x_ref, ys_ref, att_ref, gate_ref, h1_ref, h2_ref, wglu, watt, wo, gffn, wup, cw, cb, wdown,
                     gfin, y_ref, conv_ref, a_scr, *, seq_len):
    tm = x_ref.shape[1]
    h = _merge_stage(x_ref, ys_ref, att_ref, gate_ref, wglu, watt, wo)
    hn = _rms(h, gffn[...]).astype(BF16)
    pos = lax.broadcasted_iota(jnp.int32, (tm, 1), 0) % seq_len

    def history(a, lo, hi):
        w = hi - lo
        a_scr[6:8, 0:w] = jnp.zeros((2, w), F32)
        conv_ref[0, :, lo:hi] = a
        a1 = jnp.where(pos < 1, h1_ref[:, lo:hi], a_scr[7:7 + tm, 0:w])
        a2 = jnp.where(pos < 2, h2_ref[:, lo:hi], a_scr[6:6 + tm, 0:w])
        return a1, a2

    y_ref[0] = _convglu_stage(h, hn, a_scr, wup, cw, cb, wdown, gfin, history)


def _ffn_prompt_body(x_ref, ys_ref, att_ref, gate_ref, wglu, watt, wo, gffn, wup, cw, cb, wdown, gfin,
                     y_ref, conv_ref, a_scr, carry):
    tm = x_ref.shape[1]

    @pl.when(pl.program_id(1) == 0)
    def _():
        carry[...] = jnp.zeros(carry.shape, F32)

    h = _merge_stage(x_ref, ys_ref, att_ref, gate_ref, wglu, watt, wo)
    hn = _rms(h, gffn[...]).astype(BF16)

    def history(a, lo, hi):
        w = hi - lo
        a_scr[6:8, 0:w] = carry[:, lo:hi]
        carry[:, lo:hi] = a[tm - 2:tm, :]
        return a_scr[7:7 + tm, 0:w], a_scr[6:6 + tm, 0:w]

    y_ref[0] = _convglu_stage(h, hn, a_scr, wup, cw, cb, wdown, gfin, history)

    @pl.when(pl.program_id(1) == pl.num_programs(1) - 1)
    def _():
        conv_ref[0] = carry[...]


def _ffn_weight_specs():
    return [_const_spec((D_SSM, 2 * D_MODEL)), _const_spec((D_ATTN, D_MODEL)), _const_spec((D_MODEL, D_MODEL)),
            _const_spec((1, D_MODEL)), _const_spec((D_MODEL, 2 * D_FF)), _const_spec((CONV_W, D_FF)),
            _const_spec((1, D_FF)), _const_spec((D_FF, D_MODEL)), _const_spec((1, D_MODEL))]


def _ffn_prompt(x, ys, att, gates, wts, tm):
    n, t, _ = x.shape
    tiles = t // tm
    tok = lambda w: pl.BlockSpec((1, tm, w), lambda i, j: (i, j, 0))
    ys_spec = pl.BlockSpec((SUPER, tm // S5_CHUNK, S5_ROW), lambda i, j: (0, i * tiles + j, 0))
    seq = pl.BlockSpec((1, CONV_W - 1, D_FF), lambda i, j: (i, 0, 0))
    fc = max(hi - lo for lo, hi in FF_CHUNKS)
    return pl.pallas_call(
        _ffn_prompt_body,
        grid=(n, tiles),
        in_specs=[tok(D_MODEL), ys_spec, tok(D_ATTN), tok(2 * D_MODEL)] + _ffn_weight_specs(),
        out_specs=[tok(D_MODEL), seq],
        out_shape=[jax.ShapeDtypeStruct((n, t, D_MODEL), F32),
                   jax.ShapeDtypeStruct((n, CONV_W - 1, D_FF), F32)],
        scratch_shapes=[pltpu.VMEM((tm + 8, fc), F32), pltpu.VMEM((CONV_W - 1, D_FF), F32)],
        compiler_params=_params("arbitrary", "arbitrary"),
        name="merge_ffn_prompt",
    )(x, ys, att, gates, *wts)


def _ffn_sample(x, ys, att, gates, h1, h2, wts, seq_len):
    rows = x.shape[1]
    tok = lambda w: pl.BlockSpec((1, rows, w), lambda i: (0, 0, 0))
    ys_spec = pl.BlockSpec((SUPER, rows // S5_CHUNK, S5_ROW), lambda i: (0, 0, 0))
    full = pl.BlockSpec((rows, D_FF), lambda i: (0, 0))
    fc = max(hi - lo for lo, hi in FF_CHUNKS)
    return pl.pallas_call(
        functools.partial(_ffn_sample_body, seq_len=seq_len),
        grid=(1,),
        in_specs=[tok(D_MODEL), ys_spec, tok(D_ATTN), tok(2 * D_MODEL), full, full] + _ffn_weight_specs(),
        out_specs=[tok(D_MODEL), tok(D_FF)],
        out_shape=[jax.ShapeDtypeStruct((1, rows, D_MODEL), F32),
                   jax.ShapeDtypeStruct((1, rows, D_FF), F32)],
        scratch_shapes=[pltpu.VMEM((rows + 8, fc), F32)],
        compiler_params=_params("arbitrary"),
        name="merge_ffn_sample",
    )(x, ys, att, gates, h1, h2, *wts)


PROMPT_TOKEN_TILE = 1024
FFN_TOKEN_TILE = 512
SAMPLE_STREAM_PAD = 16


def kernel(x_prompt, x_sample, state_ssm_re, state_ssm_im, cache_attn_k, cache_attn_v, cache_conv,
           g_mix, w_in, ssm_lambda_re, ssm_lambda_im, ssm_log_dt, ssm_b_re, ssm_b_im,
           ssm_c_re, ssm_c_im, ssm_d, w_ssm_glu, attn_rel_bias, w_attn_up, w_o,
           g_ffn, w_up, conv_w, conv_b, w_down, g_final):
    depth = w_in.shape[0]
    nb, seq, _ = x_prompt.shape
    db, dseq, _ = x_sample.shape
    kept = min(ATTN_REACH, seq)
    assert depth == 1 and seq % PROMPT_TOKEN_TILE == 0 and seq % FFN_TOKEN_TILE == 0
    assert seq % S5_CHUNK == 0 and dseq == S5_CHUNK and nb % S5_PROMPT_SEQS == 0 and db <= SAMPLE_STREAM_PAD
    l = 0

    log2e = math.log2(math.e)
    col_scale = jnp.ones((D_IN,), F32).at[D_SSM:D_SSM + D_ATTN].set(HEAD_DIM ** -0.5 * log2e)
    w_in_b = (w_in[l] * col_scale[None, :]).astype(BF16)
    wts = (w_ssm_glu[l].astype(BF16), w_attn_up[l].astype(BF16), w_o[l].astype(BF16), g_ffn[l][None, :],
           w_up[l].astype(BF16), conv_w[l], conv_b[l][None, :], w_down[l].astype(BF16), g_final[None, :])
    g_mix_l = g_mix[l][None, :]
    w_s5 = _s5_weights(ssm_lambda_re[l], ssm_lambda_im[l], ssm_log_dt[l], ssm_b_re[l], ssm_b_im[l],
                       ssm_c_re[l], ssm_c_im[l], ssm_d[l])
    table = attn_rel_bias[l].astype(F32) * log2e
    bias_p = _rel_bias(table, CHUNK, BAND)
    w_cache = cache_attn_k.shape[2]
    assert w_cache == ATTN_REACH
    bias_s = _rel_bias(table, dseq, w_cache + dseq)

    u, qkv, gates, k32, v32 = _inproj(x_prompt, g_mix_l, w_in_b, PROMPT_TOKEN_TILE, kept)
    n_chunks = seq // S5_CHUNK
    tiles = nb // S5_PROMPT_SEQS
    zero_state = jnp.zeros((SUPER, tiles, S5_PROMPT_SEQS, SUPER_STATE), F32)
    ys, p_state = _s5(u, w_s5, zero_state, S5_PROMPT_SEQS, n_chunks)
    p_re, p_im = _state_from_super(p_state, nb)
    att = _attn_prompt(qkv, bias_p)
    y_prompt, p_conv = _ffn_prompt(x_prompt, ys, att, gates, wts, FFN_TOKEN_TILE)

    rows = db * dseq
    xs = x_sample.reshape(1, rows, D_MODEL)
    us, qkvs, gates_s, ks32, vs32 = _inproj(xs, g_mix_l, w_in_b, rows, rows)
    us2 = jnp.pad(us, ((0, 0), (0, SAMPLE_STREAM_PAD - db), (0, 0)))
    s0 = _state_to_super(state_ssm_re[l], state_ssm_im[l], 1, SAMPLE_STREAM_PAD)
    yss, s_state = _s5(us2, w_s5, s0, SAMPLE_STREAM_PAD, 1)
    s_re, s_im = _state_from_super(s_state, db)
    yss = yss[:, :db]
    qkvs3 = qkvs.reshape(db, dseq, 3 * D_ATTN)
    kk = jnp.concatenate([cache_attn_k[l].reshape(db, w_cache, D_ATTN).astype(BF16),
                          qkvs3[:, :, D_ATTN:2 * D_ATTN]], axis=1)
    vv = jnp.concatenate([cache_attn_v[l].reshape(db, w_cache, D_ATTN).astype(BF16),
                          qkvs3[:, :, 2 * D_ATTN:]], axis=1)
    att_s = _attn_sample(qkvs3[:, :, :D_ATTN], kk, vv, bias_s)
    hist = cache_conv[l].astype(F32)
    pad_rows = ((0, 0), (0, dseq - 1), (0, 0))
    h1 = jnp.pad(hist[:, 1:2], pad_rows).reshape(rows, D_FF)
    h2 = jnp.pad(hist, ((0, 0), (0, dseq - 2), (0, 0))).reshape(rows, D_FF)
    y_s, a_s = _ffn_sample(xs, yss, att_s.reshape(1, rows, D_ATTN), gates_s, h1, h2, wts, dseq)
    s_conv = a_s.reshape(db, dseq, D_FF)[:, dseq - (CONV_W - 1):]

    heads = lambda a, n, t: a.reshape(1, n, t, N_HEADS, HEAD_DIM)
    return (y_prompt, y_s.reshape(db, dseq, D_MODEL),
            p_re[None], p_im[None],
            heads(k32, nb, kept), heads(v32, nb, kept), p_conv[None],
            s_re[None], s_im[None],
            heads(ks32, db, dseq), heads(vs32, db, dseq), s_conv[None])
```

```python
import functools
import math

import jax
import jax.numpy as jnp
import numpy as np
from jax import lax
from jax.experimental import pallas as pl
from jax.experimental.pallas import tpu as pltpu

F32 = jnp.float32
BF16 = jnp.bfloat16

D_MODEL = 1024
CHUNK = 64
LEFT_CHUNKS = 8
ATTN_REACH = LEFT_CHUNKS * CHUNK
BAND = ATTN_REACH + CHUNK
D_SSM = D_MODEL // 2
D_ATTN = D_MODEL // 2
HEAD_DIM = 64
N_HEADS = D_ATTN // HEAD_DIM
MAX_REL = 128
SSM_GROUP = 16
N_GROUPS = D_SSM // SSM_GROUP
SSM_STATE = 64
D_FF = 2816
CONV_W = 3
D_IN = D_SSM + 3 * D_ATTN + 2 * D_MODEL
RMS_EPS = 1e-6
MASK_VALUE = -1e30

V7X_VMEM_LIMIT_BYTES = 56 * 1024 * 1024
LANES = 128
MERGE_BLOCK = 256
FF_CHUNKS = ((0, 2816),)


def _params(*sem):
    return pltpu.CompilerParams(dimension_semantics=sem, vmem_limit_bytes=V7X_VMEM_LIMIT_BYTES)


def _const_spec(shape):
    zeros = (0,) * len(shape)
    return pl.BlockSpec(shape, lambda *_: zeros, pipeline_mode=pl.Buffered(1))


def _rms(x, g):
    return x * lax.rsqrt(jnp.mean(x * x, axis=-1, keepdims=True) + RMS_EPS) * g


def _gelu(x):
    k = math.sqrt(2.0 / math.pi)
    return (0.5 * x) * (1.0 + jnp.tanh(x * (k + (k * 0.044715) * (x * x))))


def _sigmoid(x):
    return 1.0 / (1.0 + jnp.exp(-x))


def _dot(a, b):
    return jnp.dot(a, b, preferred_element_type=F32)


S5_CHUNK = 16
SUPER = D_SSM // LANES
SUPER_GROUPS = N_GROUPS // SUPER
SUPER_STATE = 2 * SUPER_GROUPS * SSM_STATE
S5_ROW = S5_CHUNK * LANES
S5_PROMPT_SEQS = 8


def _inproj_body(x_ref, g_ref, w_ref, u_ref, qkv_ref, gate_ref, k32_ref, v32_ref):
    xn = _rms(x_ref[0], g_ref[...]).astype(BF16)

    def seg(lo, hi):
        return _dot(xn, w_ref[:, lo:hi])

    u = seg(0, D_SSM)
    n_rows = u.shape[0] // S5_CHUNK
    for s in range(SUPER):
        frames = u[:, s * LANES:(s + 1) * LANES].reshape(n_rows, S5_CHUNK, LANES)
        frames = jnp.swapaxes(frames, 0, 1)
        for j in range(S5_CHUNK):
            u_ref[s, :, j * LANES:(j + 1) * LANES] = frames[j].astype(BF16)
    q0 = D_SSM
    qkv_ref[0, :, 0:D_ATTN] = seg(q0, q0 + D_ATTN).astype(BF16)
    k = seg(q0 + D_ATTN, q0 + 2 * D_ATTN)
    v = seg(q0 + 2 * D_ATTN, q0 + 3 * D_ATTN)
    qkv_ref[0, :, D_ATTN:2 * D_ATTN] = k.astype(BF16)
    qkv_ref[0, :, 2 * D_ATTN:3 * D_ATTN] = v.astype(BF16)
    g0 = q0 + 3 * D_ATTN
    gate_ref[0, :, 0:D_MODEL] = seg(g0, g0 + D_MODEL).astype(BF16)
    gate_ref[0, :, D_MODEL:2 * D_MODEL] = seg(g0 + D_MODEL, g0 + 2 * D_MODEL).astype(BF16)

    @pl.when(pl.program_id(1) == pl.num_programs(1) - 1)
    def _():
        tm, kept = k.shape[0], k32_ref.shape[1]
        k32_ref[0] = k[tm - kept:, :]
        v32_ref[0] = v[tm - kept:, :]


def _inproj(x, g_mix, w_in_b, tm, kept):
    n, t, _ = x.shape
    assert kept <= tm
    tiles = t // tm
    n_rows = tm // S5_CHUNK
    tok = lambda w: pl.BlockSpec((1, tm, w), lambda i, j: (i, j, 0))
    keep = pl.BlockSpec((1, kept, D_ATTN), lambda i, j: (i, 0, 0))
    u_spec = pl.BlockSpec((SUPER, n_rows, S5_ROW), lambda i, j: (0, i * tiles + j, 0))
    return pl.pallas_call(
        _inproj_body,
        grid=(n, tiles),
        in_specs=[tok(D_MODEL), _const_spec((1, D_MODEL)), _const_spec((D_MODEL, D_IN))],
        out_specs=[u_spec, tok(3 * D_ATTN), tok(2 * D_MODEL), keep, keep],
        out_shape=[
            jax.ShapeDtypeStruct((SUPER, n * tiles * n_rows, S5_ROW), BF16),
            jax.ShapeDtypeStruct((n, t, 3 * D_ATTN), BF16),
            jax.ShapeDtypeStruct((n, t, 2 * D_MODEL), BF16),
            jax.ShapeDtypeStruct((n, kept, D_ATTN), F32),
            jax.ShapeDtypeStruct((n, kept, D_ATTN), F32),
        ],
        compiler_params=_params("arbitrary", "arbitrary"),
        name="inproj",
    )(x, g_mix, w_in_b)


def _s5_weights(lam_re, lam_im, log_dt, b_re, b_im, c_re, c_im, d_skip):
    L = S5_CHUNK
    hi = lax.Precision.HIGHEST
    f = lambda a: a.astype(F32)
    cmul = lambda ar, ai, br, bi: (ar * br - ai * bi, ar * bi + ai * br)
    a_re, a_im = f(lam_re), f(lam_im)
    dt = jnp.exp(f(log_dt))[:, None]
    tau = jnp.arange(L + 1, dtype=F32)[None, :, None]
    mag = jnp.exp((a_re * dt)[:, None, :] * tau)
    ang = (a_im * dt)[:, None, :] * tau
    pw_re, pw_im = mag * jnp.cos(ang), mag * jnp.sin(ang)
    n_re, n_im = pw_re[:, 1] - 1.0, pw_im[:, 1]
    den = a_re * a_re + a_im * a_im
    f_re, f_im = (n_re * a_re + n_im * a_im) / den, (n_im * a_re - n_re * a_im) / den
    bb_re, bb_im = cmul(f_re[..., None], f_im[..., None], f(b_re), f(b_im))
    cp_re, cp_im = cmul(f(c_re)[:, :, None, :], f(c_im)[:, :, None, :], pw_re[:, None], pw_im[:, None])
    kt = (jnp.einsum('gatp,gph->ghta', cp_re[:, :, :L], bb_re, precision=hi)
          - jnp.einsum('gatp,gph->ghta', cp_im[:, :, :L], bb_im, precision=hi))
    kt = kt.at[:, :, 0, :].add(f(d_skip)[:, :, None] * jnp.eye(SSM_GROUP, dtype=F32)[None])
    rev = L - 1 - np.arange(L)
    bt_re, bt_im = jnp.swapaxes(bb_re, 1, 2)[:, None], jnp.swapaxes(bb_im, 1, 2)[:, None]
    wst_re, wst_im = cmul(pw_re[:, rev][:, :, None, :], pw_im[:, rev][:, :, None, :], bt_re, bt_im)

    sg = lambda a: a.reshape((SUPER, SUPER_GROUPS) + a.shape[1:])
    kc = jnp.transpose(sg(kt), (0, 2, 3, 1, 4)).reshape(SUPER, SSM_GROUP, S5_ROW)
    even = (np.arange(SUPER_GROUPS) % 2 == 0)[None, None, :, None, None]

    def in_form(w):
        w = jnp.transpose(sg(w), (0, 2, 1, 3, 4))
        w = jnp.concatenate([jnp.where(even, w, 0.0), jnp.where(even, 0.0, w)], axis=-1)
        return w.reshape(SUPER, S5_ROW, 2 * SSM_STATE)

    def out_form(c):
        return jnp.transpose(sg(c), (0, 4, 3, 1, 2)).reshape(SUPER, SSM_STATE, S5_ROW)

    wc = jnp.stack([in_form(wst_re), in_form(wst_im)], axis=1)
    ic = jnp.stack([out_form(cp_re[:, :, 1:L + 1]), out_form(-cp_im[:, :, 1:L + 1])], axis=1)
    half = SUPER_STATE // 2
    lam = jnp.concatenate([pw_re[:, L].reshape(SUPER, half), pw_im[:, L].reshape(SUPER, half)], axis=-1)
    return kc.astype(BF16), wc.astype(BF16), ic.astype(BF16), lam[:, None, :]


def _s5_body(u_ref, kc_ref, wc_ref, ic_ref, lam_ref, s0_ref, y_ref, sfin_ref, toep, wst, wint, x_scr, prev_scr,
             *, ns, nc):
    half = SUPER_STATE // 2

    @pl.when(pl.program_id(1) == 0)
    def _():
        zero = jnp.zeros((), BF16)
        group_of = lambda shape, dim: (lax.broadcasted_iota(jnp.int32, shape, dim) % LANES) // SSM_GROUP
        col_group = group_of((SSM_GROUP, S5_ROW), 1)
        kc = kc_ref[0]
        bblk = jnp.concatenate([jnp.where(col_group == g, kc, zero) for g in range(SUPER_GROUPS)], axis=0)
        for j in range(S5_CHUNK):
            rows = slice(j * LANES, (j + 1) * LANES)
            if j:
                toep[rows, 0:j * LANES] = jnp.zeros((LANES, j * LANES), BF16)
            toep[rows, j * LANES:S5_ROW] = bblk[:, 0:S5_ROW - j * LANES]
        row_pair = group_of((S5_ROW, LANES), 0) // 2
        for part in range(2):
            w = wc_ref[0, part]
            for k in range(SUPER_GROUPS // 2):
                c0 = part * half + k * LANES
                wst[:, c0:c0 + LANES] = jnp.where(row_pair == k, w, zero)
        col_group = group_of((SSM_STATE, S5_ROW), 1)
        for part in range(2):
            c = ic_ref[0, part]
            for g in range(SUPER_GROUPS):
                r0 = part * half + g * SSM_STATE
                wint[r0:r0 + SSM_STATE, :] = jnp.where(col_group == g, c, zero)

    u = u_ref[0]
    x = _dot(u, wst[...])
    regroup = nc > 1
    if regroup:
        x = jnp.swapaxes(x.reshape(ns, nc, SUPER_STATE), 0, 1).reshape(ns * nc, SUPER_STATE)
    nblk = SUPER_STATE // LANES
    hblk = nblk // 2
    for k in range(nblk):
        x_scr[k] = x[:, k * LANES:(k + 1) * LANES]
    lam = [lam_ref[0, :, k * LANES:(k + 1) * LANES] for k in range(nblk)]

    def step(c, s):
        rows = pl.ds(pl.multiple_of(c * ns, ns), ns)
        nxt = []
        for k in range(nblk):
            prev_scr[k, rows, :] = s[k]
        for k in range(hblk):
            l_re, l_im, s_re, s_im = lam[k], lam[hblk + k], s[k], s[hblk + k]
            nxt.append((l_re * s_re - l_im * s_im + x_scr[k, rows, :],
                        l_re * s_im + l_im * s_re + x_scr[hblk + k, rows, :]))
        return tuple(p[0] for p in nxt) + tuple(p[1] for p in nxt)

    s0 = s0_ref[0, 0]
    s_fin = lax.fori_loop(0, nc, step, tuple(s0[:, k * LANES:(k + 1) * LANES] for k in range(nblk)),
                          unroll=min(nc, 4))
    for k in range(nblk):
        sfin_ref[0, 0, :, k * LANES:(k + 1) * LANES] = s_fin[k]

    prev = jnp.concatenate([prev_scr[k] for k in range(nblk)], axis=-1)
    if regroup:
        prev = jnp.swapaxes(prev.reshape(nc, ns, SUPER_STATE), 0, 1).reshape(ns * nc, SUPER_STATE)
    prev = prev.astype(BF16)
    blk = 2 * LANES
    for tb in range(S5_ROW // blk):
        cols = slice(tb * blk, (tb + 1) * blk)
        depth = (tb + 1) * blk
        y = _dot(u[:, 0:depth], toep[0:depth, cols]) + _dot(prev, wint[:, cols])
        y_ref[0, :, cols] = _gelu(y).astype(BF16)


def _s5(u2, weights, s0, ns, nc):
    kc, wc, ic, lam = weights
    tiles = s0.shape[1]
    r = ns * nc
    per_sg = lambda *dims: pl.BlockSpec((1,) + dims, lambda s, i: (s,) + (0,) * len(dims))
    rows = pl.BlockSpec((1, r, S5_ROW), lambda s, i: (s, i, 0))
    state = pl.BlockSpec((1, 1, ns, SUPER_STATE), lambda s, i: (s, i, 0, 0))
    return pl.pallas_call(
        functools.partial(_s5_body, ns=ns, nc=nc),
        grid=(SUPER, tiles),
        in_specs=[rows, per_sg(SSM_GROUP, S5_ROW), per_sg(2, S5_ROW, LANES), per_sg(2, SSM_STATE, S5_ROW),
                  per_sg(1, SUPER_STATE), state],
        out_specs=[rows, state],
        out_shape=[jax.ShapeDtypeStruct((SUPER, tiles * r, S5_ROW), BF16),
                   jax.ShapeDtypeStruct((SUPER, tiles, ns, SUPER_STATE), F32)],
        scratch_shapes=[pltpu.VMEM((S5_ROW, S5_ROW), BF16), pltpu.VMEM((S5_ROW, SUPER_STATE), BF16),
                        pltpu.VMEM((SUPER_STATE, S5_ROW), BF16),
                        pltpu.VMEM((SUPER_STATE // LANES, r, LANES), F32),
                        pltpu.VMEM((SUPER_STATE // LANES, r, LANES), F32)],
        compiler_params=_params("arbitrary", "arbitrary"),
        name="s5",
    )(u2, kc, wc, ic, lam, s0)


def _state_to_super(s_re, s_im, tiles, ns):
    n = s_re.shape[0]
    half = SUPER_STATE // 2
    s = jnp.concatenate([s_re.astype(F32).reshape(n, SUPER, half), s_im.astype(F32).reshape(n, SUPER, half)], -1)
    s = jnp.pad(jnp.transpose(s, (1, 0, 2)), ((0, 0), (0, tiles * ns - n), (0, 0)))
    return s.reshape(SUPER, tiles, ns, SUPER_STATE)


def _state_from_super(s, n):
    half = SUPER_STATE // 2
    s = jnp.transpose(s.reshape(SUPER, -1, SUPER_STATE)[:, :n], (1, 0, 2))
    return (s[:, :, :half].reshape(n, N_GROUPS, SSM_STATE), s[:, :, half:].reshape(n, N_GROUPS, SSM_STATE))


def _attend(qc, kb, vb, bias_ref, valid, skip=0):
    tq = qc.shape[0]
    left = lax.broadcasted_iota(jnp.int32, (tq, LANES), 1) < HEAD_DIM
    zero = jnp.zeros((), BF16)
    pairs = range(N_HEADS // 2)
    cols = [slice(hp * LANES, (hp + 1) * LANES) for hp in pairs]
    scores = []
    for hp in pairs:
        qp = qc[:, cols[hp]]
        q2 = jnp.concatenate([jnp.where(left, qp, zero), jnp.where(left, zero, qp)], axis=0)
        sc = lax.dot_general(q2, kb[:, cols[hp]], (((1,), (1,)), ((), ())), preferred_element_type=F32)
        sc = sc + bias_ref[hp, :, skip:]
        if valid is not None:
            sc = jnp.where(valid, sc, MASK_VALUE)
        scores.append(sc)
    probs, dens = [], []
    for sc in scores:
        e = jnp.exp2(sc - jnp.max(sc, axis=-1, keepdims=True))
        dens.append(jnp.sum(e, axis=-1, keepdims=True))
        probs.append(e.astype(BF16))
    outs = []
    for hp in pairs:
        o2 = _dot(probs[hp], vb[:, cols[hp]]) / dens[hp]
        outs.append(jnp.where(left, o2[:tq], o2[tq:]))
    return jnp.concatenate(outs, axis=-1)


def _attn_prompt_body(q_ref, k_ref, v_ref, bias_ref, o_ref, kpad, vpad):
    t = q_ref.shape[1]
    zeros = jnp.zeros((ATTN_REACH, D_ATTN), BF16)
    kpad[0:ATTN_REACH, :] = zeros
    vpad[0:ATTN_REACH, :] = zeros
    kpad[ATTN_REACH:ATTN_REACH + t, :] = k_ref[0]
    vpad[ATTN_REACH:ATTN_REACH + t, :] = v_ref[0]
    def chunk(c, skip, masked):
        r0 = c * CHUNK if isinstance(c, int) else pl.multiple_of(c * CHUNK, CHUNK)
        qc = q_ref[0, pl.ds(r0, CHUNK), :]
        kb = kpad[pl.ds(r0 + skip, BAND - skip), :]
        vb = vpad[pl.ds(r0 + skip, BAND - skip), :]
        valid = None
        if masked:
            col = lax.broadcasted_iota(jnp.int32, (2 * CHUNK, BAND - skip), 1) + skip
            valid = col >= ATTN_REACH - r0
        o_ref[0, pl.ds(r0, CHUNK), :] = _attend(qc, kb, vb, bias_ref, valid, skip).astype(BF16)

    n_chunks = t // CHUNK
    for c in range(min(LEFT_CHUNKS, n_chunks)):
        seen = (c + 1) * CHUNK
        skip = (BAND - seen) // LANES * LANES
        chunk(c, skip, BAND - skip != seen)
    lax.fori_loop(LEFT_CHUNKS, n_chunks, lambda c, _: chunk(c, 0, False), None, unroll=4)


def _attn_prompt(qkv, bias):
    n, t, _ = qkv.shape
    part = lambda j: pl.BlockSpec((1, t, D_ATTN), lambda i: (i, 0, j))
    return pl.pallas_call(
        _attn_prompt_body,
        grid=(n,),
        in_specs=[part(0), part(1), part(2), _const_spec((N_HEADS // 2, 2 * CHUNK, BAND))],
        out_specs=pl.BlockSpec((1, t, D_ATTN), lambda i: (i, 0, 0)),
        out_shape=jax.ShapeDtypeStruct((n, t, D_ATTN), BF16),
        scratch_shapes=[pltpu.VMEM((ATTN_REACH + t, D_ATTN), BF16), pltpu.VMEM((ATTN_REACH + t, D_ATTN), BF16)],
        compiler_params=_params("arbitrary"),
        name="attn_prompt",
    )(qkv, qkv, qkv, bias)


def _attn_sample_body(q_ref, k_ref, v_ref, bias_ref, o_ref):
    o_ref[0] = _attend(q_ref[0], k_ref[0], v_ref[0], bias_ref, None).astype(BF16)


def _attn_sample(q, kk, vv, bias):
    n, tq, _ = q.shape
    tk = kk.shape[1]
    return pl.pallas_call(
        _attn_sample_body,
        grid=(n,),
        in_specs=[pl.BlockSpec((1, tq, D_ATTN), lambda i: (i, 0, 0)),
                  pl.BlockSpec((1, tk, D_ATTN), lambda i: (i, 0, 0)),
                  pl.BlockSpec((1, tk, D_ATTN), lambda i: (i, 0, 0)),
                  _const_spec((N_HEADS // 2, 2 * tq, tk))],
        out_specs=pl.BlockSpec((1, tq, D_ATTN), lambda i: (i, 0, 0)),
        out_shape=jax.ShapeDtypeStruct((n, tq, D_ATTN), BF16),
        compiler_params=_params("arbitrary"),
        name="attn_sample",
    )(q, kk, vv, bias)


def _rel_bias(table, tq, tk):
    width = tq + tk - 1
    z = np.arange(width)
    idx = np.clip(ATTN_REACH - np.where(z < tk, z, z - width), -MAX_REL, MAX_REL) + MAX_REL
    ext = table.astype(F32)[:, idx]
    flat = jnp.tile(ext, (1, tq))[:, :tq * (width - 1)]
    bias = flat.reshape(N_HEADS, tq, width - 1)[:, :, :tk]
    return bias.reshape(N_HEADS // 2, 2 * tq, tk)


def _merge_blocks(ys_ref, att_ref, gate_ref, wglu, watt):
    tm = att_ref.shape[1]
    br_att = _dot(att_ref[0], watt[...])
    blocks = []
    for s in range(SUPER):
        frames = jnp.stack([ys_ref[s, :, j * LANES:(j + 1) * LANES].astype(F32) for j in range(S5_CHUNK)], axis=0)
        blocks.append(jnp.swapaxes(frames, 0, 1).reshape(tm, LANES))
    ys = jnp.concatenate(blocks, axis=-1).astype(BF16)

    for k in range(D_MODEL // MERGE_BLOCK):
        cols = slice(k * MERGE_BLOCK, (k + 1) * MERGE_BLOCK)
        gate_cols = slice(D_MODEL + k * MERGE_BLOCK, D_MODEL + (k + 1) * MERGE_BLOCK)
        br_ssm = _dot(ys, wglu[:, cols]) * _sigmoid(_dot(ys, wglu[:, gate_cols]))
        mix = (_sigmoid(gate_ref[0, :, cols].astype(F32)) * br_ssm
               + _sigmoid(gate_ref[0, :, gate_cols].astype(F32)) * br_att[:, cols])
        yield mix.astype(BF16)


def _merge_project(x_ref, blocks, wo):
    h = x_ref[0]
    blocks = iter(blocks)
    pending = next(blocks)
    for k in range(D_MODEL // MERGE_BLOCK):
        following = next(blocks, None)
        h = h + _dot(pending, wo[k * MERGE_BLOCK:(k + 1) * MERGE_BLOCK, :])
        pending = following
    return h


def _convglu_up(hn, wup):
    return [(_dot(hn, wup[:, lo:hi]), _dot(hn, wup[:, D_FF + lo:D_FF + hi])) for lo, hi in FF_CHUNKS]


def _convglu_down(ab, a_scr, cw, cb, wdown, history):
    tm = ab[0][0].shape[0]
    acc = jnp.zeros((tm, D_MODEL), F32)
    for (lo, hi), (a, b) in zip(FF_CHUNKS, ab):
        a_scr[8:8 + tm, 0:hi - lo] = a
        a1, a2 = history(a, lo, hi)
        c = cb[:, lo:hi] + cw[0:1, lo:hi] * a2 + cw[1:2, lo:hi] * a1 + cw[2:3, lo:hi] * a
        act = (_gelu(c) * b).astype(BF16)
        acc = acc + _dot(act, wdown[lo:hi, :])
    return acc


def _ffn_sample_body(x_ref, ys_ref, att_ref, gate_ref, h1_ref, h2_ref, wglu, watt, wo, gffn, wup, cw, cb, wdown,
                     gfin, y_ref, conv_ref, a_scr, *, seq_len):
    tm = x_ref.shape[1]
    h = _merge_project(x_ref, _merge_blocks(ys_ref, att_ref, gate_ref, wglu, watt), wo)
    hn = _rms(h, gffn[...]).astype(BF16)
    pos = lax.broadcasted_iota(jnp.int32, (tm, 1), 0) % seq_len

    def history(a, lo, hi):
        w = hi - lo
        a_scr[6:8, 0:w] = jnp.zeros((2, w), F32)
        conv_ref[0, :, lo:hi] = a
        a1 = jnp.where(pos < 1, h1_ref[:, lo:hi], a_scr[7:7 + tm, 0:w])
        a2 = jnp.where(pos < 2, h2_ref[:, lo:hi], a_scr[6:6 + tm, 0:w])
        return a1, a2

    down = _convglu_down(_convglu_up(hn, wup), a_scr, cw, cb, wdown, history)
    y_ref[0] = _rms(h + down, gfin[...])


def _ffn_prompt_body(x_ref, ys_ref, att_ref, gate_ref, wglu, watt, wo, gffn, wup, cw, cb, wdown, gfin,
                     y_ref, conv_ref, a_scr, carry):
    tm = x_ref.shape[1]

    @pl.when(pl.program_id(1) == 0)
    def _():
        carry[...] = jnp.zeros(carry.shape, F32)

    h = _merge_project(x_ref, _merge_blocks(ys_ref, att_ref, gate_ref, wglu, watt), wo)
    hn = _rms(h, gffn[...]).astype(BF16)

    def history(a, lo, hi):
        w = hi - lo
        a_scr[6:8, 0:w] = carry[:, lo:hi]
        carry[:, lo:hi] = a[tm - 2:tm, :]
        return a_scr[7:7 + tm, 0:w], a_scr[6:6 + tm, 0:w]

    down = _convglu_down(_convglu_up(hn, wup), a_scr, cw, cb, wdown, history)
    y_ref[0] = _rms(h + down, gfin[...])

    @pl.when(pl.program_id(1) == pl.num_programs(1) - 1)
    def _():
        conv_ref[0] = carry[...]


def _ffn_weight_specs():
    return [_const_spec((D_SSM, 2 * D_MODEL)), _const_spec((D_ATTN, D_MODEL)), _const_spec((D_MODEL, D_MODEL)),
            _const_spec((1, D_MODEL)), _const_spec((D_MODEL, 2 * D_FF)), _const_spec((CONV_W, D_FF)),
            _const_spec((1, D_FF)), _const_spec((D_FF, D_MODEL)), _const_spec((1, D_MODEL))]


def _ffn_prompt(x, ys, att, gates, wts, tm):
    n, t, _ = x.shape
    tiles = t // tm
    tok = lambda w: pl.BlockSpec((1, tm, w), lambda i, j: (i, j, 0))
    ys_spec = pl.BlockSpec((SUPER, tm // S5_CHUNK, S5_ROW), lambda i, j: (0, i * tiles + j, 0))
    seq = pl.BlockSpec((1, CONV_W - 1, D_FF), lambda i, j: (i, 0, 0))
    fc = max(hi - lo for lo, hi in FF_CHUNKS)
    return pl.pallas_call(
        _ffn_prompt_body,
        grid=(n, tiles),
        in_specs=[tok(D_MODEL), ys_spec, tok(D_ATTN), tok(2 * D_MODEL)] + _ffn_weight_specs(),
        out_specs=[tok(D_MODEL), seq],
        out_shape=[jax.ShapeDtypeStruct((n, t, D_MODEL), F32),
                   jax.ShapeDtypeStruct((n, CONV_W - 1, D_FF), F32)],
        scratch_shapes=[pltpu.VMEM((tm + 8, fc), F32), pltpu.VMEM((CONV_W - 1, D_FF), F32)],
        compiler_params=_params("arbitrary", "arbitrary"),
        name="merge_ffn_prompt",
    )(x, ys, att, gates, *wts)


def _ffn_sample(x, ys, att, gates, h1, h2, wts, seq_len):
    rows = x.shape[1]
    tok = lambda w: pl.BlockSpec((1, rows, w), lambda i: (0, 0, 0))
    ys_spec = pl.BlockSpec((SUPER, rows // S5_CHUNK, S5_ROW), lambda i: (0, 0, 0))
    full = pl.BlockSpec((rows, D_FF), lambda i: (0, 0))
    fc = max(hi - lo for lo, hi in FF_CHUNKS)
    return pl.pallas_call(
        functools.partial(_ffn_sample_body, seq_len=seq_len),
        grid=(1,),
        in_specs=[tok(D_MODEL), ys_spec, tok(D_ATTN), tok(2 * D_MODEL), full, full] + _ffn_weight_specs(),
        out_specs=[tok(D_MODEL), tok(D_FF)],
        out_shape=[jax.ShapeDtypeStruct((1, rows, D_MODEL), F32),
                   jax.ShapeDtypeStruct((1, rows, D_FF), F32)],
        scratch_shapes=[pltpu.VMEM((rows + 8, fc), F32)],
        compiler_params=_params("arbitrary"),
        name="merge_ffn_sample",
    )(x, ys, att, gates, h1, h2, *wts)


PROMPT_TOKEN_TILE = 1024
FFN_TOKEN_TILE = 512
SAMPLE_STREAM_PAD = 16


def kernel(x_prompt, x_sample, state_ssm_re, state_ssm_im, cache_attn_k, cache_attn_v, cache_conv,
           g_mix, w_in, ssm_lambda_re, ssm_lambda_im, ssm_log_dt, ssm_b_re, ssm_b_im,
           ssm_c_re, ssm_c_im, ssm_d, w_ssm_glu, attn_rel_bias, w_attn_up, w_o,
           g_ffn, w_up, conv_w, conv_b, w_down, g_final):
    depth = w_in.shape[0]
    nb, seq, _ = x_prompt.shape
    db, dseq, _ = x_sample.shape
    kept = min(ATTN_REACH, seq)
    assert depth == 1 and seq % PROMPT_TOKEN_TILE == 0 and seq % FFN_TOKEN_TILE == 0
    assert seq % S5_CHUNK == 0 and dseq == S5_CHUNK and nb % S5_PROMPT_SEQS == 0 and db <= SAMPLE_STREAM_PAD
    l = 0

    log2e = math.log2(math.e)
    col_scale = jnp.ones((D_IN,), F32).at[D_SSM:D_SSM + D_ATTN].set(HEAD_DIM ** -0.5 * log2e)
    w_in_b = (w_in[l] * col_scale[None, :]).astype(BF16)
    wts = (w_ssm_glu[l].astype(BF16), w_attn_up[l].astype(BF16), w_o[l].astype(BF16), g_ffn[l][None, :],
           w_up[l].astype(BF16), conv_w[l], conv_b[l][None, :], w_down[l].astype(BF16), g_final[None, :])
    g_mix_l = g_mix[l][None, :]
    w_s5 = _s5_weights(ssm_lambda_re[l], ssm_lambda_im[l], ssm_log_dt[l], ssm_b_re[l], ssm_b_im[l],
                       ssm_c_re[l], ssm_c_im[l], ssm_d[l])
    table = attn_rel_bias[l].astype(F32) * log2e
    bias_p = _rel_bias(table, CHUNK, BAND)
    w_cache = cache_attn_k.shape[2]
    assert w_cache == ATTN_REACH
    bias_s = _rel_bias(table, dseq, w_cache + dseq)

    u, qkv, gates, k32, v32 = _inproj(x_prompt, g_mix_l, w_in_b, PROMPT_TOKEN_TILE, kept)
    n_chunks = seq // S5_CHUNK
    tiles = nb // S5_PROMPT_SEQS
    zero_state = jnp.zeros((SUPER, tiles, S5_PROMPT_SEQS, SUPER_STATE), F32)
    ys, p_state = _s5(u, w_s5, zero_state, S5_PROMPT_SEQS, n_chunks)
    p_re, p_im = _state_from_super(p_state, nb)
    att = _attn_prompt(qkv, bias_p)
    y_prompt, p_conv = _ffn_prompt(x_prompt, ys, att, gates, wts, FFN_TOKEN_TILE)

    rows = db * dseq
    xs = x_sample.reshape(1, rows, D_MODEL)
    us, qkvs, gates_s, ks32, vs32 = _inproj(xs, g_mix_l, w_in_b, rows, rows)
    us2 = jnp.pad(us, ((0, 0), (0, SAMPLE_STREAM_PAD - db), (0, 0)))
    s0 = _state_to_super(state_ssm_re[l], state_ssm_im[l], 1, SAMPLE_STREAM_PAD)
    yss, s_state = _s5(us2, w_s5, s0, SAMPLE_STREAM_PAD, 1)
    s_re, s_im = _state_from_super(s_state, db)
    yss = yss[:, :db]
    qkvs3 = qkvs.reshape(db, dseq, 3 * D_ATTN)
    kk = jnp.concatenate([cache_attn_k[l].reshape(db, w_cache, D_ATTN).astype(BF16),
                          qkvs3[:, :, D_ATTN:2 * D_ATTN]], axis=1)
    vv = jnp.concatenate([cache_attn_v[l].reshape(db, w_cache, D_ATTN).astype(BF16),
                          qkvs3[:, :, 2 * D_ATTN:]], axis=1)
    att_s = _attn_sample(qkvs3[:, :, :D_ATTN], kk, vv, bias_s)
    hist = cache_conv[l].astype(F32)
    pad_rows = ((0, 0), (0, dseq - 1), (0, 0))
    h1 = jnp.pad(hist[:, 1:2], pad_rows).reshape(rows, D_FF)
    h2 = jnp.pad(hist, ((0, 0), (0, dseq - 2), (0, 0))).reshape(rows, D_FF)
    y_s, a_s = _ffn_sample(xs, yss, att_s.reshape(1, rows, D_ATTN), gates_s, h1, h2, wts, dseq)
    s_conv = a_s.reshape(db, dseq, D_FF)[:, dseq - (CONV_W - 1):]

    heads = lambda a, n, t: a.reshape(1, n, t, N_HEADS, HEAD_DIM)
    return (y_prompt, y_s.reshape(db, dseq, D_MODEL),
            p_re[None], p_im[None],
            heads(k32, nb, kept), heads(v32, nb, kept), p_conv[None],
            s_re[None], s_im[None],
            heads(ks32, db, dseq), heads(vs32, db, dseq), s_conv[None])
```

```python
import functools
import math

import jax
import jax.numpy as jnp
import numpy as np
from jax import lax
from jax.experimental import pallas as pl
from jax.experimental.pallas import tpu as pltpu

F32 = jnp.float32
BF16 = jnp.bfloat16

D_MODEL = 1024
CHUNK = 64
LEFT_CHUNKS = 8
ATTN_REACH = LEFT_CHUNKS * CHUNK
BAND = ATTN_REACH + CHUNK
D_SSM = D_MODEL // 2
D_ATTN = D_MODEL // 2
HEAD_DIM = 64
N_HEADS = D_ATTN // HEAD_DIM
MAX_REL = 128
SSM_GROUP = 16
N_GROUPS = D_SSM // SSM_GROUP
SSM_STATE = 64
D_FF = 2816
CONV_W = 3
D_IN = D_SSM + 3 * D_ATTN + 2 * D_MODEL
RMS_EPS = 1e-6
MASK_VALUE = -1e30

V7X_VMEM_LIMIT_BYTES = 56 * 1024 * 1024
LANES = 128
MERGE_BLOCK = 256
FF_CHUNKS = ((0, 2816),)


def _params(*sem):
    return pltpu.CompilerParams(dimension_semantics=sem, vmem_limit_bytes=V7X_VMEM_LIMIT_BYTES)


def _const_spec(shape):
    zeros = (0,) * len(shape)
    return pl.BlockSpec(shape, lambda *_: zeros, pipeline_mode=pl.Buffered(1))


def _rms(x, g):
    return x * lax.rsqrt(jnp.mean(x * x, axis=-1, keepdims=True) + RMS_EPS) * g


def _gelu(x):
    k = math.sqrt(2.0 / math.pi)
    return (0.5 * x) * (1.0 + jnp.tanh(x * (k + (k * 0.044715) * (x * x))))


def _sigmoid(x):
    return 1.0 / (1.0 + jnp.exp(-x))


def _dot(a, b):
    return jnp.dot(a, b, preferred_element_type=F32)


S5_CHUNK = 16
SUPER = D_SSM // LANES
SUPER_GROUPS = N_GROUPS // SUPER
SUPER_STATE = 2 * SUPER_GROUPS * SSM_STATE
S5_ROW = S5_CHUNK * LANES
S5_PROMPT_SEQS = 8


def _inproj_body(x_ref, g_ref, w_ref, u_ref, qkv_ref, gate_ref, k32_ref, v32_ref):
    xn = _rms(x_ref[0], g_ref[...]).astype(BF16)

    def seg(lo, hi):
        return _dot(xn, w_ref[:, lo:hi])

    u = seg(0, D_SSM)
    n_rows = u.shape[0] // S5_CHUNK
    for s in range(SUPER):
        frames = u[:, s * LANES:(s + 1) * LANES].reshape(n_rows, S5_CHUNK, LANES)
        frames = jnp.swapaxes(frames, 0, 1)
        for j in range(S5_CHUNK):
            u_ref[s, :, j * LANES:(j + 1) * LANES] = frames[j].astype(BF16)
    q0 = D_SSM
    qkv = seg(q0, q0 + 3 * D_ATTN)
    k = qkv[:, D_ATTN:2 * D_ATTN]
    v = qkv[:, 2 * D_ATTN:3 * D_ATTN]
    qkv_ref[0] = qkv.astype(BF16)
    g0 = q0 + 3 * D_ATTN
    gate_ref[0] = seg(g0, g0 + 2 * D_MODEL).astype(BF16)

    @pl.when(pl.program_id(1) == pl.num_programs(1) - 1)
    def _():
        tm, kept = k.shape[0], k32_ref.shape[1]
        k32_ref[0] = k[tm - kept:, :]
        v32_ref[0] = v[tm - kept:, :]


def _inproj(x, g_mix, w_in_b, tm, kept):
    n, t, _ = x.shape
    assert kept <= tm
    tiles = t // tm
    n_rows = tm // S5_CHUNK
    tok = lambda w: pl.BlockSpec((1, tm, w), lambda i, j: (i, j, 0))
    keep = pl.BlockSpec((1, kept, D_ATTN), lambda i, j: (i, 0, 0))
    u_spec = pl.BlockSpec((SUPER, n_rows, S5_ROW), lambda i, j: (0, i * tiles + j, 0))
    return pl.pallas_call(
        _inproj_body,
        grid=(n, tiles),
        in_specs=[tok(D_MODEL), _const_spec((1, D_MODEL)), _const_spec((D_MODEL, D_IN))],
        out_specs=[u_spec, tok(3 * D_ATTN), tok(2 * D_MODEL), keep, keep],
        out_shape=[
            jax.ShapeDtypeStruct((SUPER, n * tiles * n_rows, S5_ROW), BF16),
            jax.ShapeDtypeStruct((n, t, 3 * D_ATTN), BF16),
            jax.ShapeDtypeStruct((n, t, 2 * D_MODEL), BF16),
            jax.ShapeDtypeStruct((n, kept, D_ATTN), F32),
            jax.ShapeDtypeStruct((n, kept, D_ATTN), F32),
        ],
        compiler_params=_params("arbitrary", "arbitrary"),
        name="inproj",
    )(x, g_mix, w_in_b)


def _s5_weights(lam_re, lam_im, log_dt, b_re, b_im, c_re, c_im, d_skip):
    L = S5_CHUNK
    hi = lax.Precision.HIGHEST
    f = lambda a: a.astype(F32)
    cmul = lambda ar, ai, br, bi: (ar * br - ai * bi, ar * bi + ai * br)
    a_re, a_im = f(lam_re), f(lam_im)
    dt = jnp.exp(f(log_dt))[:, None]
    tau = jnp.arange(L + 1, dtype=F32)[None, :, None]
    mag = jnp.exp((a_re * dt)[:, None, :] * tau)
    ang = (a_im * dt)[:, None, :] * tau
    pw_re, pw_im = mag * jnp.cos(ang), mag * jnp.sin(ang)
    n_re, n_im = pw_re[:, 1] - 1.0, pw_im[:, 1]
    den = a_re * a_re + a_im * a_im
    f_re, f_im = (n_re * a_re + n_im * a_im) / den, (n_im * a_re - n_re * a_im) / den
    bb_re, bb_im = cmul(f_re[..., None], f_im[..., None], f(b_re), f(b_im))
    cp_re, cp_im = cmul(f(c_re)[:, :, None, :], f(c_im)[:, :, None, :], pw_re[:, None], pw_im[:, None])
    kt = (jnp.einsum('gatp,gph->ghta', cp_re[:, :, :L], bb_re, precision=hi)
          - jnp.einsum('gatp,gph->ghta', cp_im[:, :, :L], bb_im, precision=hi))
    kt = kt.at[:, :, 0, :].add(f(d_skip)[:, :, None] * jnp.eye(SSM_GROUP, dtype=F32)[None])
    rev = L - 1 - np.arange(L)
    bt_re, bt_im = jnp.swapaxes(bb_re, 1, 2)[:, None], jnp.swapaxes(bb_im, 1, 2)[:, None]
    wst_re, wst_im = cmul(pw_re[:, rev][:, :, None, :], pw_im[:, rev][:, :, None, :], bt_re, bt_im)

    sg = lambda a: a.reshape((SUPER, SUPER_GROUPS) + a.shape[1:])
    kc = jnp.transpose(sg(kt), (0, 2, 3, 1, 4)).reshape(SUPER, SSM_GROUP, S5_ROW)
    even = (np.arange(SUPER_GROUPS) % 2 == 0)[None, None, :, None, None]

    def in_form(w):
        w = jnp.transpose(sg(w), (0, 2, 1, 3, 4))
        w = jnp.concatenate([jnp.where(even, w, 0.0), jnp.where(even, 0.0, w)], axis=-1)
        return w.reshape(SUPER, S5_ROW, 2 * SSM_STATE)

    def out_form(c):
        return jnp.transpose(sg(c), (0, 4, 3, 1, 2)).reshape(SUPER, SSM_STATE, S5_ROW)

    wc = jnp.stack([in_form(wst_re), in_form(wst_im)], axis=1)
    ic = jnp.stack([out_form(cp_re[:, :, 1:L + 1]), out_form(-cp_im[:, :, 1:L + 1])], axis=1)
    half = SUPER_STATE // 2
    lam = jnp.concatenate([pw_re[:, L].reshape(SUPER, half), pw_im[:, L].reshape(SUPER, half)], axis=-1)
    return kc.astype(BF16), wc.astype(BF16), ic.astype(BF16), lam[:, None, :]


def _s5_body(u_ref, kc_ref, wc_ref, ic_ref, lam_ref, s0_ref, y_ref, sfin_ref, toep, wst, wint, x_scr, prev_scr,
             *, ns, nc):
    half = SUPER_STATE // 2

    @pl.when(pl.program_id(1) == 0)
    def _():
        zero = jnp.zeros((), BF16)
        group_of = lambda shape, dim: (lax.broadcasted_iota(jnp.int32, shape, dim) % LANES) // SSM_GROUP
        col_group = group_of((SSM_GROUP, S5_ROW), 1)
        kc = kc_ref[0]
        bblk = jnp.concatenate([jnp.where(col_group == g, kc, zero) for g in range(SUPER_GROUPS)], axis=0)
        for j in range(S5_CHUNK):
            rows = slice(j * LANES, (j + 1) * LANES)
            if j:
                toep[rows, 0:j * LANES] = jnp.zeros((LANES, j * LANES), BF16)
            toep[rows, j * LANES:S5_ROW] = bblk[:, 0:S5_ROW - j * LANES]
        row_pair = group_of((S5_ROW, LANES), 0) // 2
        for part in range(2):
            w = wc_ref[0, part]
            for k in range(SUPER_GROUPS // 2):
                c0 = part * half + k * LANES
                wst[:, c0:c0 + LANES] = jnp.where(row_pair == k, w, zero)
        col_group = group_of((SSM_STATE, S5_ROW), 1)
        for part in range(2):
            c = ic_ref[0, part]
            for g in range(SUPER_GROUPS):
                r0 = part * half + g * SSM_STATE
                wint[r0:r0 + SSM_STATE, :] = jnp.where(col_group == g, c, zero)

    u = u_ref[0]
    x = _dot(u, wst[...])
    regroup = nc > 1
    if regroup:
        x = jnp.swapaxes(x.reshape(ns, nc, SUPER_STATE), 0, 1).reshape(ns * nc, SUPER_STATE)
    nblk = SUPER_STATE // LANES
    hblk = nblk // 2
    for k in range(nblk):
        x_scr[k] = x[:, k * LANES:(k + 1) * LANES]
    lam = [lam_ref[0, :, k * LANES:(k + 1) * LANES] for k in range(nblk)]

    def step(c, s):
        rows = pl.ds(pl.multiple_of(c * ns, ns), ns)
        nxt = []
        for k in range(nblk):
            prev_scr[k, rows, :] = s[k]
        for k in range(hblk):
            l_re, l_im, s_re, s_im = lam[k], lam[hblk + k], s[k], s[hblk + k]
            nxt.append((l_re * s_re - l_im * s_im + x_scr[k, rows, :],
                        l_re * s_im + l_im * s_re + x_scr[hblk + k, rows, :]))
        return tuple(p[0] for p in nxt) + tuple(p[1] for p in nxt)

    s0 = s0_ref[0, 0]
    s_fin = lax.fori_loop(0, nc, step, tuple(s0[:, k * LANES:(k + 1) * LANES] for k in range(nblk)),
                          unroll=min(nc, 4))
    for k in range(nblk):
        sfin_ref[0, 0, :, k * LANES:(k + 1) * LANES] = s_fin[k]

    prev = jnp.concatenate([prev_scr[k] for k in range(nblk)], axis=-1)
    if regroup:
        prev = jnp.swapaxes(prev.reshape(nc, ns, SUPER_STATE), 0, 1).reshape(ns * nc, SUPER_STATE)
    prev = prev.astype(BF16)
    blk = 2 * LANES
    for tb in range(S5_ROW // blk):
        cols = slice(tb * blk, (tb + 1) * blk)
        depth = (tb + 1) * blk
        y = _dot(u[:, 0:depth], toep[0:depth, cols]) + _dot(prev, wint[:, cols])
        y_ref[0, :, cols] = _gelu(y).astype(BF16)


def _s5(u2, weights, s0, ns, nc):
    kc, wc, ic, lam = weights
    tiles = s0.shape[1]
    r = ns * nc
    per_sg = lambda *dims: pl.BlockSpec((1,) + dims, lambda s, i: (s,) + (0,) * len(dims))
    rows = pl.BlockSpec((1, r, S5_ROW), lambda s, i: (s, i, 0))
    state = pl.BlockSpec((1, 1, ns, SUPER_STATE), lambda s, i: (s, i, 0, 0))
    return pl.pallas_call(
        functools.partial(_s5_body, ns=ns, nc=nc),
        grid=(SUPER, tiles),
        in_specs=[rows, per_sg(SSM_GROUP, S5_ROW), per_sg(2, S5_ROW, LANES), per_sg(2, SSM_STATE, S5_ROW),
                  per_sg(1, SUPER_STATE), state],
        out_specs=[rows, state],
        out_shape=[jax.ShapeDtypeStruct((SUPER, tiles * r, S5_ROW), BF16),
                   jax.ShapeDtypeStruct((SUPER, tiles, ns, SUPER_STATE), F32)],
        scratch_shapes=[pltpu.VMEM((S5_ROW, S5_ROW), BF16), pltpu.VMEM((S5_ROW, SUPER_STATE), BF16),
                        pltpu.VMEM((SUPER_STATE, S5_ROW), BF16),
                        pltpu.VMEM((SUPER_STATE // LANES, r, LANES), F32),
                        pltpu.VMEM((SUPER_STATE // LANES, r, LANES), F32)],
        compiler_params=_params("arbitrary", "arbitrary"),
        name="s5",
    )(u2, kc, wc, ic, lam, s0)


def _state_to_super(s_re, s_im, tiles, ns):
    n = s_re.shape[0]
    half = SUPER_STATE // 2
    s = jnp.concatenate([s_re.astype(F32).reshape(n, SUPER, half), s_im.astype(F32).reshape(n, SUPER, half)], -1)
    s = jnp.pad(jnp.transpose(s, (1, 0, 2)), ((0, 0), (0, tiles * ns - n), (0, 0)))
    return s.reshape(SUPER, tiles, ns, SUPER_STATE)


def _state_from_super(s, n):
    half = SUPER_STATE // 2
    s = jnp.transpose(s.reshape(SUPER, -1, SUPER_STATE)[:, :n], (1, 0, 2))
    return (s[:, :, :half].reshape(n, N_GROUPS, SSM_STATE), s[:, :, half:].reshape(n, N_GROUPS, SSM_STATE))


def _attend(qc, kb, vb, bias_ref, valid, skip=0):
    tq = qc.shape[0]
    left = lax.broadcasted_iota(jnp.int32, (tq, LANES), 1) < HEAD_DIM
    zero = jnp.zeros((), BF16)
    pairs = range(N_HEADS // 2)
    cols = [slice(hp * LANES, (hp + 1) * LANES) for hp in pairs]
    scores = []
    for hp in pairs:
        qp = qc[:, cols[hp]]
        q2 = jnp.concatenate([jnp.where(left, qp, zero), jnp.where(left, zero, qp)], axis=0)
        sc = lax.dot_general(q2, kb[:, cols[hp]], (((1,), (1,)), ((), ())), preferred_element_type=F32)
        sc = sc + bias_ref[hp, :, skip:]
        if valid is not None:
            sc = jnp.where(valid, sc, MASK_VALUE)
        scores.append(sc)
    probs, dens = [], []
    for sc in scores:
        e = jnp.exp2(sc - jnp.max(sc, axis=-1, keepdims=True))
        dens.append(jnp.sum(e, axis=-1, keepdims=True))
        probs.append(e.astype(BF16))
    outs = []
    for hp in pairs:
        o2 = _dot(probs[hp], vb[:, cols[hp]]) / dens[hp]
        outs.append(jnp.where(left, o2[:tq], o2[tq:]))
    return jnp.concatenate(outs, axis=-1)


def _attn_prompt_body(q_ref, k_ref, v_ref, bias_ref, o_ref, kpad, vpad):
    t = q_ref.shape[1]
    kpad[0:CHUNK, :] = jnp.zeros((CHUNK, D_ATTN), BF16)
    vpad[0:CHUNK, :] = jnp.zeros((CHUNK, D_ATTN), BF16)
    kpad[CHUNK:BAND, :] = k_ref[0, 0:ATTN_REACH, :]
    vpad[CHUNK:BAND, :] = v_ref[0, 0:ATTN_REACH, :]

    def chunk(c, skip, masked, head):
        if head:
            r0 = c * CHUNK
            rows = pl.ds(r0 + skip - ATTN_REACH + CHUNK, BAND - skip)
            kb, vb = kpad[rows, :], vpad[rows, :]
        else:
            r0 = c * CHUNK
            if not isinstance(r0, int):
                r0 = pl.multiple_of(r0, CHUNK)
            rows = pl.ds(r0 - ATTN_REACH, BAND)
            kb, vb = k_ref[0, rows, :], v_ref[0, rows, :]
        qc = q_ref[0, pl.ds(r0, CHUNK), :]
        valid = None
        if masked:
            col = lax.broadcasted_iota(jnp.int32, (2 * CHUNK, BAND - skip), 1) + skip
            valid = col >= ATTN_REACH - r0
        o_ref[0, pl.ds(r0, CHUNK), :] = _attend(qc, kb, vb, bias_ref, valid, skip).astype(BF16)

    for c in range(LEFT_CHUNKS):
        seen = (c + 1) * CHUNK
        skip = (BAND - seen) // LANES * LANES
        chunk(c, skip, BAND - skip != seen, True)
    lax.fori_loop(LEFT_CHUNKS, t // CHUNK, lambda c, _: chunk(c, 0, False, False), None, unroll=6)


def _attn_prompt(qkv, bias):
    n, t, _ = qkv.shape
    assert t >= ATTN_REACH and t % CHUNK == 0
    part = lambda j: pl.BlockSpec((1, t, D_ATTN), lambda i: (i, 0, j))
    return pl.pallas_call(
        _attn_prompt_body,
        grid=(n,),
        in_specs=[part(0), part(1), part(2), _const_spec((N_HEADS // 2, 2 * CHUNK, BAND))],
        out_specs=pl.BlockSpec((1, t, D_ATTN), lambda i: (i, 0, 0)),
        out_shape=jax.ShapeDtypeStruct((n, t, D_ATTN), BF16),
        scratch_shapes=[pltpu.VMEM((BAND, D_ATTN), BF16), pltpu.VMEM((BAND, D_ATTN), BF16)],
        compiler_params=_params("arbitrary"),
        name="attn_prompt",
    )(qkv, qkv, qkv, bias)


def _attn_sample_body(q_ref, k_ref, v_ref, bias_ref, o_ref):
    o_ref[0] = _attend(q_ref[0], k_ref[0], v_ref[0], bias_ref, None).astype(BF16)


def _attn_sample(q, kk, vv, bias):
    n, tq, _ = q.shape
    tk = kk.shape[1]
    return pl.pallas_call(
        _attn_sample_body,
        grid=(n,),
        in_specs=[pl.BlockSpec((1, tq, D_ATTN), lambda i: (i, 0, 0)),
                  pl.BlockSpec((1, tk, D_ATTN), lambda i: (i, 0, 0)),
                  pl.BlockSpec((1, tk, D_ATTN), lambda i: (i, 0, 0)),
                  _const_spec((N_HEADS // 2, 2 * tq, tk))],
        out_specs=pl.BlockSpec((1, tq, D_ATTN), lambda i: (i, 0, 0)),
        out_shape=jax.ShapeDtypeStruct((n, tq, D_ATTN), BF16),
        compiler_params=_params("arbitrary"),
        name="attn_sample",
    )(q, kk, vv, bias)


def _rel_bias(table, tq, tk):
    width = tq + tk - 1
    z = np.arange(width)
    idx = np.clip(ATTN_REACH - np.where(z < tk, z, z - width), -MAX_REL, MAX_REL) + MAX_REL
    ext = table.astype(F32)[:, idx]
    flat = jnp.tile(ext, (1, tq))[:, :tq * (width - 1)]
    bias = flat.reshape(N_HEADS, tq, width - 1)[:, :, :tk]
    return bias.reshape(N_HEADS // 2, 2 * tq, tk)


def _merge_blocks(ys_ref, att_ref, gate_ref, wglu, watt):
    tm = att_ref.shape[1]
    br_att = _dot(att_ref[0], watt[...])
    blocks = []
    for s in range(SUPER):
        frames = jnp.stack([ys_ref[s, :, j * LANES:(j + 1) * LANES].astype(F32) for j in range(S5_CHUNK)], axis=0)
        blocks.append(jnp.swapaxes(frames, 0, 1).reshape(tm, LANES))
    ys = jnp.concatenate(blocks, axis=-1).astype(BF16)

    for k in range(D_MODEL // MERGE_BLOCK):
        cols = slice(k * MERGE_BLOCK, (k + 1) * MERGE_BLOCK)
        gate_cols = slice(D_MODEL + k * MERGE_BLOCK, D_MODEL + (k + 1) * MERGE_BLOCK)
        br_ssm = _dot(ys, wglu[:, cols]) * _sigmoid(_dot(ys, wglu[:, gate_cols]))
        mix = (_sigmoid(gate_ref[0, :, cols].astype(F32)) * br_ssm
               + _sigmoid(gate_ref[0, :, gate_cols].astype(F32)) * br_att[:, cols])
        yield mix.astype(BF16)


def _merge_project(x_ref, blocks, wo):
    h = x_ref[0]
    blocks = iter(blocks)
    pending = next(blocks)
    for k in range(D_MODEL // MERGE_BLOCK):
        following = next(blocks, None)
        h = h + _dot(pending, wo[k * MERGE_BLOCK:(k + 1) * MERGE_BLOCK, :])
        pending = following
    return h


def _convglu_up(hn, wup):
    return [(_dot(hn, wup[:, lo:hi]), _dot(hn, wup[:, D_FF + lo:D_FF + hi])) for lo, hi in FF_CHUNKS]


def _convglu_down(ab, a_scr, cw, cb, wdown, history):
    tm = ab[0][0].shape[0]
    acc = jnp.zeros((tm, D_MODEL), F32)
    for (lo, hi), (a, b) in zip(FF_CHUNKS, ab):
        a_scr[8:8 + tm, 0:hi - lo] = a
        a1, a2 = history(a, lo, hi)
        c = cb[:, lo:hi] + cw[0:1, lo:hi] * a2 + cw[1:2, lo:hi] * a1 + cw[2:3, lo:hi] * a
        act = (_gelu(c) * b).astype(BF16)
        acc = acc + _dot(act, wdown[lo:hi, :])
    return acc


def _ffn_sample_body(x_ref, ys_ref, att_ref, gate_ref, h1_ref, h2_ref, wglu, watt, wo, gffn, wup, cw, cb, wdown,
                     gfin, y_ref, conv_ref, a_scr, *, seq_len):
    tm = x_ref.shape[1]
    h = _merge_project(x_ref, _merge_blocks(ys_ref, att_ref, gate_ref, wglu, watt), wo)
    hn = _rms(h, gffn[...]).astype(BF16)
    pos = lax.broadcasted_iota(jnp.int32, (tm, 1), 0) % seq_len

    def history(a, lo, hi):
        w = hi - lo
        a_scr[6:8, 0:w] = jnp.zeros((2, w), F32)
        conv_ref[0, :, lo:hi] = a
        a1 = jnp.where(pos < 1, h1_ref[:, lo:hi], a_scr[7:7 + tm, 0:w])
        a2 = jnp.where(pos < 2, h2_ref[:, lo:hi], a_scr[6:6 + tm, 0:w])
        return a1, a2

    down = _convglu_down(_convglu_up(hn, wup), a_scr, cw, cb, wdown, history)
    y_ref[0] = _rms(h + down, gfin[...])


def _ffn_prompt_body(x_ref, ys_ref, att_ref, gate_ref, wglu, watt, wo, gffn, wup, cw, cb, wdown, gfin,
                     y_ref, conv_ref, a_scr, carry):
    tm = x_ref.shape[1]

    @pl.when(pl.program_id(1) == 0)
    def _():
        carry[...] = jnp.zeros(carry.shape, F32)

    h = _merge_project(x_ref, _merge_blocks(ys_ref, att_ref, gate_ref, wglu, watt), wo)
    hn = _rms(h, gffn[...]).astype(BF16)

    def history(a, lo, hi):
        w = hi - lo
        a_scr[6:8, 0:w] = carry[:, lo:hi]
        carry[:, lo:hi] = a[tm - 2:tm, :]
        return a_scr[7:7 + tm, 0:w], a_scr[6:6 + tm, 0:w]

    down = _convglu_down(_convglu_up(hn, wup), a_scr, cw, cb, wdown, history)
    y_ref[0] = _rms(h + down, gfin[...])

    @pl.when(pl.program_id(1) == pl.num_programs(1) - 1)
    def _():
        conv_ref[0] = carry[...]


def _ffn_weight_specs():
    return [_const_spec((D_SSM, 2 * D_MODEL)), _const_spec((D_ATTN, D_MODEL)), _const_spec((D_MODEL, D_MODEL)),
            _const_spec((1, D_MODEL)), _const_spec((D_MODEL, 2 * D_FF)), _const_spec((CONV_W, D_FF)),
            _const_spec((1, D_FF)), _const_spec((D_FF, D_MODEL)), _const_spec((1, D_MODEL))]


def _ffn_prompt(x, ys, att, gates, wts, tm):
    n, t, _ = x.shape
    tiles = t // tm
    tok = lambda w: pl.BlockSpec((1, tm, w), lambda i, j: (i, j, 0))
    ys_spec = pl.BlockSpec((SUPER, tm // S5_CHUNK, S5_ROW), lambda i, j: (0, i * tiles + j, 0))
    seq = pl.BlockSpec((1, CONV_W - 1, D_FF), lambda i, j: (i, 0, 0))
    fc = max(hi - lo for lo, hi in FF_CHUNKS)
    return pl.pallas_call(
        _ffn_prompt_body,
        grid=(n, tiles),
        in_specs=[tok(D_MODEL), ys_spec, tok(D_ATTN), tok(2 * D_MODEL)] + _ffn_weight_specs(),
        out_specs=[tok(D_MODEL), seq],
        out_shape=[jax.ShapeDtypeStruct((n, t, D_MODEL), F32),
                   jax.ShapeDtypeStruct((n, CONV_W - 1, D_FF), F32)],
        scratch_shapes=[pltpu.VMEM((tm + 8, fc), F32), pltpu.VMEM((CONV_W - 1, D_FF), F32)],
        compiler_params=_params("arbitrary", "arbitrary"),
        name="merge_ffn_prompt",
    )(x, ys, att, gates, *wts)


def _ffn_sample(x, ys, att, gates, h1, h2, wts, seq_len):
    rows = x.shape[1]
    tok = lambda w: pl.BlockSpec((1, rows, w), lambda i: (0, 0, 0))
    ys_spec = pl.BlockSpec((SUPER, rows // S5_CHUNK, S5_ROW), lambda i: (0, 0, 0))
    full = pl.BlockSpec((rows, D_FF), lambda i: (0, 0))
    fc = max(hi - lo for lo, hi in FF_CHUNKS)
    return pl.pallas_call(
        functools.partial(_ffn_sample_body, seq_len=seq_len),
        grid=(1,),
        in_specs=[tok(D_MODEL), ys_spec, tok(D_ATTN), tok(2 * D_MODEL), full, full] + _ffn_weight_specs(),
        out_specs=[tok(D_MODEL), tok(D_FF)],
        out_shape=[jax.ShapeDtypeStruct((1, rows, D_MODEL), F32),
                   jax.ShapeDtypeStruct((1, rows, D_FF), F32)],
        scratch_shapes=[pltpu.VMEM((rows + 8, fc), F32)],
        compiler_params=_params("arbitrary"),
        name="merge_ffn_sample",
    )(x, ys, att, gates, h1, h2, *wts)


PROMPT_TOKEN_TILE = 1024
FFN_TOKEN_TILE = 512
SAMPLE_STREAM_PAD = 16


def kernel(x_prompt, x_sample, state_ssm_re, state_ssm_im, cache_attn_k, cache_attn_v, cache_conv,
           g_mix, w_in, ssm_lambda_re, ssm_lambda_im, ssm_log_dt, ssm_b_re, ssm_b_im,
           ssm_c_re, ssm_c_im, ssm_d, w_ssm_glu, attn_rel_bias, w_attn_up, w_o,
           g_ffn, w_up, conv_w, conv_b, w_down, g_final):
    depth = w_in.shape[0]
    nb, seq, _ = x_prompt.shape
    db, dseq, _ = x_sample.shape
    kept = min(ATTN_REACH, seq)
    assert depth == 1 and seq % PROMPT_TOKEN_TILE == 0 and seq % FFN_TOKEN_TILE == 0
    assert seq % S5_CHUNK == 0 and dseq == S5_CHUNK and nb % S5_PROMPT_SEQS == 0 and db <= SAMPLE_STREAM_PAD
    l = 0

    log2e = math.log2(math.e)
    col_scale = jnp.ones((D_IN,), F32).at[D_SSM:D_SSM + D_ATTN].set(HEAD_DIM ** -0.5 * log2e)
    w_in_b = (w_in[l] * col_scale[None, :]).astype(BF16)
    wts = (w_ssm_glu[l].astype(BF16), w_attn_up[l].astype(BF16), w_o[l].astype(BF16), g_ffn[l][None, :],
           w_up[l].astype(BF16), conv_w[l], conv_b[l][None, :], w_down[l].astype(BF16), g_final[None, :])
    g_mix_l = g_mix[l][None, :]
    w_s5 = _s5_weights(ssm_lambda_re[l], ssm_lambda_im[l], ssm_log_dt[l], ssm_b_re[l], ssm_b_im[l],
                       ssm_c_re[l], ssm_c_im[l], ssm_d[l])
    table = attn_rel_bias[l].astype(F32) * log2e
    bias_p = _rel_bias(table, CHUNK, BAND)
    w_cache = cache_attn_k.shape[2]
    assert w_cache == ATTN_REACH
    bias_s = _rel_bias(table, dseq, w_cache + dseq)

    u, qkv, gates, k32, v32 = _inproj(x_prompt, g_mix_l, w_in_b, PROMPT_TOKEN_TILE, kept)
    n_chunks = seq // S5_CHUNK
    tiles = nb // S5_PROMPT_SEQS
    zero_state = jnp.zeros((SUPER, tiles, S5_PROMPT_SEQS, SUPER_STATE), F32)
    ys, p_state = _s5(u, w_s5, zero_state, S5_PROMPT_SEQS, n_chunks)
    p_re, p_im = _state_from_super(p_state, nb)
    att = _attn_prompt(qkv, bias_p)
    y_prompt, p_conv = _ffn_prompt(x_prompt, ys, att, gates, wts, FFN_TOKEN_TILE)

    rows = db * dseq
    xs = x_sample.reshape(1, rows, D_MODEL)
    us, qkvs, gates_s, ks32, vs32 = _inproj(xs, g_mix_l, w_in_b, rows, rows)
    us2 = jnp.pad(us, ((0, 0), (0, SAMPLE_STREAM_PAD - db), (0, 0)))
    s0 = _state_to_super(state_ssm_re[l], state_ssm_im[l], 1, SAMPLE_STREAM_PAD)
    yss, s_state = _s5(us2, w_s5, s0, SAMPLE_STREAM_PAD, 1)
    s_re, s_im = _state_from_super(s_state, db)
    yss = yss[:, :db]
    qkvs3 = qkvs.reshape(db, dseq, 3 * D_ATTN)
    kk = jnp.concatenate([cache_attn_k[l].reshape(db, w_cache, D_ATTN).astype(BF16),
                          qkvs3[:, :, D_ATTN:2 * D_ATTN]], axis=1)
    vv = jnp.concatenate([cache_attn_v[l].reshape(db, w_cache, D_ATTN).astype(BF16),
                          qkvs3[:, :, 2 * D_ATTN:]], axis=1)
    att_s = _attn_sample(qkvs3[:, :, :D_ATTN], kk, vv, bias_s)
    hist = cache_conv[l].astype(F32)
    pad_rows = ((0, 0), (0, dseq - 1), (0, 0))
    h1 = jnp.pad(hist[:, 1:2], pad_rows).reshape(rows, D_FF)
    h2 = jnp.pad(hist, ((0, 0), (0, dseq - 2), (0, 0))).reshape(rows, D_FF)
    y_s, a_s = _ffn_sample(xs, yss, att_s.reshape(1, rows, D_ATTN), gates_s, h1, h2, wts, dseq)
    s_conv = a_s.reshape(db, dseq, D_FF)[:, dseq - (CONV_W - 1):]

    heads = lambda a, n, t: a.reshape(1, n, t, N_HEADS, HEAD_DIM)
    return (y_prompt, y_s.reshape(db, dseq, D_MODEL),
            p_re[None], p_im[None],
            heads(k32, nb, kept), heads(v32, nb, kept), p_conv[None],
            s_re[None], s_im[None],
            heads(ks32, db, dseq), heads(vs32, db, dseq), s_conv[None])
```

```python
import functools
import math

import jax
import jax.numpy as jnp
import numpy as np
from jax import lax
from jax.experimental import pallas as pl
from jax.experimental.pallas import tpu as pltpu

F32 = jnp.float32
BF16 = jnp.bfloat16

D_MODEL = 1024
CHUNK = 64
LEFT_CHUNKS = 8
ATTN_REACH = LEFT_CHUNKS * CHUNK
BAND = ATTN_REACH + CHUNK
D_SSM = D_MODEL // 2
D_ATTN = D_MODEL // 2
HEAD_DIM = 64
N_HEADS = D_ATTN // HEAD_DIM
MAX_REL = 128
SSM_GROUP = 16
N_GROUPS = D_SSM // SSM_GROUP
SSM_STATE = 64
D_FF = 2816
CONV_W = 3
D_IN = D_SSM + 3 * D_ATTN + 2 * D_MODEL
RMS_EPS = 1e-6
MASK_VALUE = -1e30

V7X_VMEM_LIMIT_BYTES = 56 * 1024 * 1024
LANES = 128
BF16_SUBLANES = 16
MERGE_BLOCK = 256
FF_CHUNKS = ((0, 2816),)


def _params(*sem):
    return pltpu.CompilerParams(dimension_semantics=sem, vmem_limit_bytes=V7X_VMEM_LIMIT_BYTES)


def _const_spec(shape):
    zeros = (0,) * len(shape)
    return pl.BlockSpec(shape, lambda *_: zeros, pipeline_mode=pl.Buffered(1))


def _rms(x, g):
    return x * lax.rsqrt(jnp.mean(x * x, axis=-1, keepdims=True) + RMS_EPS) * g


def _gelu(x):
    k = math.sqrt(2.0 / math.pi)
    return (0.5 * x) * (1.0 + jnp.tanh(x * (k + (k * 0.044715) * (x * x))))


def _sigmoid(x):
    return 1.0 / (1.0 + jnp.exp(-x))


def _dot(a, b):
    return jnp.dot(a, b, preferred_element_type=F32)


S5_CHUNK = 16
SUPER = D_SSM // LANES
SUPER_GROUPS = N_GROUPS // SUPER
SUPER_STATE = 2 * SUPER_GROUPS * SSM_STATE
S5_ROW = S5_CHUNK * LANES
S5_PROMPT_SEQS = 8


def _inproj_body(x_ref, g_ref, w_ref, u_ref, qkv_ref, gate_ref, k32_ref, v32_ref):
    xn = _rms(x_ref[0], g_ref[...]).astype(BF16)

    def seg(lo, hi):
        return _dot(xn, w_ref[:, lo:hi])

    u = seg(0, D_SSM)
    n_rows = u.shape[0] // S5_CHUNK
    for s in range(SUPER):
        frames = u[:, s * LANES:(s + 1) * LANES].reshape(n_rows, S5_CHUNK, LANES)
        frames = jnp.swapaxes(frames, 0, 1)
        for j in range(S5_CHUNK):
            u_ref[s, :, j * LANES:(j + 1) * LANES] = frames[j].astype(BF16)
    q0 = D_SSM
    qkv = seg(q0, q0 + 3 * D_ATTN)
    k = qkv[:, D_ATTN:2 * D_ATTN]
    v = qkv[:, 2 * D_ATTN:3 * D_ATTN]
    qkv_ref[0] = qkv.astype(BF16)
    g0 = q0 + 3 * D_ATTN
    gate_ref[0] = seg(g0, g0 + 2 * D_MODEL).astype(BF16)

    @pl.when(pl.program_id(1) == pl.num_programs(1) - 1)
    def _():
        tm, kept = k.shape[0], k32_ref.shape[1]
        k32_ref[0] = k[tm - kept:, :]
        v32_ref[0] = v[tm - kept:, :]


def _inproj(x, g_mix, w_in_b, tm, kept):
    n, t, _ = x.shape
    assert kept <= tm
    tiles = t // tm
    n_rows = tm // S5_CHUNK
    tok = lambda w: pl.BlockSpec((1, tm, w), lambda i, j: (i, j, 0))
    keep = pl.BlockSpec((1, kept, D_ATTN), lambda i, j: (i, 0, 0))
    u_spec = pl.BlockSpec((SUPER, n_rows, S5_ROW), lambda i, j: (0, i * tiles + j, 0))
    return pl.pallas_call(
        _inproj_body,
        grid=(n, tiles),
        in_specs=[tok(D_MODEL), _const_spec((1, D_MODEL)), _const_spec((D_MODEL, D_IN))],
        out_specs=[u_spec, tok(3 * D_ATTN), tok(2 * D_MODEL), keep, keep],
        out_shape=[
            jax.ShapeDtypeStruct((SUPER, n * tiles * n_rows, S5_ROW), BF16),
            jax.ShapeDtypeStruct((n, t, 3 * D_ATTN), BF16),
            jax.ShapeDtypeStruct((n, t, 2 * D_MODEL), BF16),
            jax.ShapeDtypeStruct((n, kept, D_ATTN), F32),
            jax.ShapeDtypeStruct((n, kept, D_ATTN), F32),
        ],
        compiler_params=_params("arbitrary", "arbitrary"),
        name="inproj",
    )(x, g_mix, w_in_b)


def _s5_weights(lam_re, lam_im, log_dt, b_re, b_im, c_re, c_im, d_skip):
    L = S5_CHUNK
    hi = lax.Precision.HIGHEST
    f = lambda a: a.astype(F32)
    cmul = lambda ar, ai, br, bi: (ar * br - ai * bi, ar * bi + ai * br)
    a_re, a_im = f(lam_re), f(lam_im)
    dt = jnp.exp(f(log_dt))[:, None]
    tau = jnp.arange(L + 1, dtype=F32)[None, :, None]
    mag = jnp.exp((a_re * dt)[:, None, :] * tau)
    ang = (a_im * dt)[:, None, :] * tau
    pw_re, pw_im = mag * jnp.cos(ang), mag * jnp.sin(ang)
    n_re, n_im = pw_re[:, 1] - 1.0, pw_im[:, 1]
    den = a_re * a_re + a_im * a_im
    f_re, f_im = (n_re * a_re + n_im * a_im) / den, (n_im * a_re - n_re * a_im) / den
    bb_re, bb_im = cmul(f_re[..., None], f_im[..., None], f(b_re), f(b_im))
    cp_re, cp_im = cmul(f(c_re)[:, :, None, :], f(c_im)[:, :, None, :], pw_re[:, None], pw_im[:, None])
    kt = (jnp.einsum('gatp,gph->ghta', cp_re[:, :, :L], bb_re, precision=hi)
          - jnp.einsum('gatp,gph->ghta', cp_im[:, :, :L], bb_im, precision=hi))
    kt = kt.at[:, :, 0, :].add(f(d_skip)[:, :, None] * jnp.eye(SSM_GROUP, dtype=F32)[None])
    rev = L - 1 - np.arange(L)
    bt_re, bt_im = jnp.swapaxes(bb_re, 1, 2)[:, None], jnp.swapaxes(bb_im, 1, 2)[:, None]
    wst_re, wst_im = cmul(pw_re[:, rev][:, :, None, :], pw_im[:, rev][:, :, None, :], bt_re, bt_im)

    sg = lambda a: a.reshape((SUPER, SUPER_GROUPS) + a.shape[1:])
    kc = jnp.transpose(sg(kt), (0, 2, 3, 1, 4)).reshape(SUPER, SSM_GROUP, S5_ROW)
    even = (np.arange(SUPER_GROUPS) % 2 == 0)[None, None, :, None, None]

    def in_form(w):
        w = jnp.transpose(sg(w), (0, 2, 1, 3, 4))
        w = jnp.concatenate([jnp.where(even, w, 0.0), jnp.where(even, 0.0, w)], axis=-1)
        return w.reshape(SUPER, S5_ROW, 2 * SSM_STATE)

    def out_form(c):
        return jnp.transpose(sg(c), (0, 4, 3, 1, 2)).reshape(SUPER, SSM_STATE, S5_ROW)

    wc = jnp.stack([in_form(wst_re), in_form(wst_im)], axis=1)
    ic = jnp.stack([out_form(cp_re[:, :, 1:L + 1]), out_form(-cp_im[:, :, 1:L + 1])], axis=1)
    half = SUPER_STATE // 2
    lam = jnp.concatenate([pw_re[:, L].reshape(SUPER, half), pw_im[:, L].reshape(SUPER, half)], axis=-1)
    return kc.astype(BF16), wc.astype(BF16), ic.astype(BF16), lam[:, None, :]


def _s5_body(u_ref, kc_ref, wc_ref, ic_ref, lam_ref, s0_ref, y_ref, sfin_ref, toep, wst, wint, x_scr, prev_scr,
             *, ns, nc):
    half = SUPER_STATE // 2

    @pl.when(pl.program_id(1) == 0)
    def _():
        zero = jnp.zeros((), BF16)
        group_of = lambda shape, dim: (lax.broadcasted_iota(jnp.int32, shape, dim) % LANES) // SSM_GROUP
        col_group = group_of((SSM_GROUP, S5_ROW), 1)
        kc = kc_ref[0]
        bblk = jnp.concatenate([jnp.where(col_group == g, kc, zero) for g in range(SUPER_GROUPS)], axis=0)
        for j in range(S5_CHUNK):
            rows = slice(j * LANES, (j + 1) * LANES)
            if j:
                toep[rows, 0:j * LANES] = jnp.zeros((LANES, j * LANES), BF16)
            toep[rows, j * LANES:S5_ROW] = bblk[:, 0:S5_ROW - j * LANES]
        row_pair = group_of((S5_ROW, LANES), 0) // 2
        for part in range(2):
            w = wc_ref[0, part]
            for k in range(SUPER_GROUPS // 2):
                c0 = part * half + k * LANES
                wst[:, c0:c0 + LANES] = jnp.where(row_pair == k, w, zero)
        col_group = group_of((SSM_STATE, S5_ROW), 1)
        for part in range(2):
            c = ic_ref[0, part]
            for g in range(SUPER_GROUPS):
                r0 = part * half + g * SSM_STATE
                wint[r0:r0 + SSM_STATE, :] = jnp.where(col_group == g, c, zero)

    u = u_ref[0]
    x = _dot(u, wst[...])
    regroup = nc > 1
    if regroup:
        x = jnp.swapaxes(x.reshape(ns, nc, SUPER_STATE), 0, 1).reshape(ns * nc, SUPER_STATE)
    nblk = SUPER_STATE // LANES
    hblk = nblk // 2
    for k in range(nblk):
        x_scr[k] = x[:, k * LANES:(k + 1) * LANES]
    lam = [lam_ref[0, :, k * LANES:(k + 1) * LANES] for k in range(nblk)]

    def step(c, s):
        rows = pl.ds(pl.multiple_of(c * ns, ns), ns)
        nxt = []
        for k in range(nblk):
            prev_scr[k, rows, :] = s[k]
        for k in range(hblk):
            l_re, l_im, s_re, s_im = lam[k], lam[hblk + k], s[k], s[hblk + k]
            nxt.append((l_re * s_re - l_im * s_im + x_scr[k, rows, :],
                        l_re * s_im + l_im * s_re + x_scr[hblk + k, rows, :]))
        return tuple(p[0] for p in nxt) + tuple(p[1] for p in nxt)

    s0 = s0_ref[0, 0]
    s_fin = lax.fori_loop(0, nc, step, tuple(s0[:, k * LANES:(k + 1) * LANES] for k in range(nblk)),
                          unroll=min(nc, 4))
    for k in range(nblk):
        sfin_ref[0, 0, :, k * LANES:(k + 1) * LANES] = s_fin[k]

    prev = jnp.concatenate([prev_scr[k] for k in range(nblk)], axis=-1)
    if regroup:
        prev = jnp.swapaxes(prev.reshape(nc, ns, SUPER_STATE), 0, 1).reshape(ns * nc, SUPER_STATE)
    prev = prev.astype(BF16)
    blk = 2 * LANES
    for tb in range(S5_ROW // blk):
        cols = slice(tb * blk, (tb + 1) * blk)
        depth = (tb + 1) * blk
        y = _dot(u[:, 0:depth], toep[0:depth, cols]) + _dot(prev, wint[:, cols])
        y_ref[0, :, cols] = _gelu(y).astype(BF16)


def _s5(u2, weights, s0, ns, nc):
    kc, wc, ic, lam = weights
    tiles = s0.shape[1]
    r = ns * nc
    per_sg = lambda *dims: pl.BlockSpec((1,) + dims, lambda s, i: (s,) + (0,) * len(dims))
    rows = pl.BlockSpec((1, r, S5_ROW), lambda s, i: (s, i, 0))
    state = pl.BlockSpec((1, 1, ns, SUPER_STATE), lambda s, i: (s, i, 0, 0))
    return pl.pallas_call(
        functools.partial(_s5_body, ns=ns, nc=nc),
        grid=(SUPER, tiles),
        in_specs=[rows, per_sg(SSM_GROUP, S5_ROW), per_sg(2, S5_ROW, LANES), per_sg(2, SSM_STATE, S5_ROW),
                  per_sg(1, SUPER_STATE), state],
        out_specs=[rows, state],
        out_shape=[jax.ShapeDtypeStruct((SUPER, tiles * r, S5_ROW), BF16),
                   jax.ShapeDtypeStruct((SUPER, tiles, ns, SUPER_STATE), F32)],
        scratch_shapes=[pltpu.VMEM((S5_ROW, S5_ROW), BF16), pltpu.VMEM((S5_ROW, SUPER_STATE), BF16),
                        pltpu.VMEM((SUPER_STATE, S5_ROW), BF16),
                        pltpu.VMEM((SUPER_STATE // LANES, r, LANES), F32),
                        pltpu.VMEM((SUPER_STATE // LANES, r, LANES), F32)],
        compiler_params=_params("arbitrary", "arbitrary"),
        name="s5",
    )(u2, kc, wc, ic, lam, s0)


def _state_to_super(s_re, s_im, tiles, ns):
    n = s_re.shape[0]
    half = SUPER_STATE // 2
    s = jnp.concatenate([s_re.astype(F32).reshape(n, SUPER, half), s_im.astype(F32).reshape(n, SUPER, half)], -1)
    s = jnp.pad(jnp.transpose(s, (1, 0, 2)), ((0, 0), (0, tiles * ns - n), (0, 0)))
    return s.reshape(SUPER, tiles, ns, SUPER_STATE)


def _state_from_super(s, n):
    half = SUPER_STATE // 2
    s = jnp.transpose(s.reshape(SUPER, -1, SUPER_STATE)[:, :n], (1, 0, 2))
    return (s[:, :, :half].reshape(n, N_GROUPS, SSM_STATE), s[:, :, half:].reshape(n, N_GROUPS, SSM_STATE))


def _attend(qc, kb, vb, bias_ref, valid, skip=0):
    tq = qc.shape[0]
    left = lax.broadcasted_iota(jnp.int32, (tq, LANES), 1) < HEAD_DIM
    zero = jnp.zeros((), BF16)
    pairs = range(N_HEADS // 2)
    cols = [slice(hp * LANES, (hp + 1) * LANES) for hp in pairs]
    scores = []
    for hp in pairs:
        qp = qc[:, cols[hp]]
        q2 = jnp.concatenate([jnp.where(left, qp, zero), jnp.where(left, zero, qp)], axis=0)
        sc = lax.dot_general(q2, kb[:, cols[hp]], (((1,), (1,)), ((), ())), preferred_element_type=F32)
        sc = sc + bias_ref[hp, :, skip:]
        if valid is not None:
            sc = jnp.where(valid, sc, MASK_VALUE)
        scores.append(sc)
    probs, dens = [], []
    for sc in scores:
        e = jnp.exp2(sc - jnp.max(sc, axis=-1, keepdims=True))
        dens.append(jnp.sum(e, axis=-1, keepdims=True))
        probs.append(e.astype(BF16))
    outs = []
    for hp in pairs:
        o2 = _dot(probs[hp], vb[:, cols[hp]]) / dens[hp]
        outs.append(jnp.where(left, o2[:tq], o2[tq:]))
    return jnp.concatenate(outs, axis=-1)


def _attn_prompt_body(q_ref, k_ref, v_ref, bias_ref, *rest, n_cast):
    w32_refs, o_ref, w16_refs, (kpad, vpad) = rest[:n_cast], rest[n_cast], rest[n_cast + 1:2 * n_cast + 1], rest[-2:]
    for w32, w16 in zip(w32_refs, w16_refs):
        w16[...] = w32[...].astype(BF16)

    t = q_ref.shape[1]
    kpad[0:CHUNK, :] = jnp.zeros((CHUNK, D_ATTN), BF16)
    vpad[0:CHUNK, :] = jnp.zeros((CHUNK, D_ATTN), BF16)
    kpad[CHUNK:BAND, :] = k_ref[0, 0:ATTN_REACH, :]
    vpad[CHUNK:BAND, :] = v_ref[0, 0:ATTN_REACH, :]

    def chunk(c, skip, masked, head):
        if head:
            r0 = c * CHUNK
            rows = pl.ds(r0 + skip - ATTN_REACH + CHUNK, BAND - skip)
            kb, vb = kpad[rows, :], vpad[rows, :]
        else:
            r0 = c * CHUNK
            if not isinstance(r0, int):
                r0 = pl.multiple_of(r0, CHUNK)
            rows = pl.ds(r0 - ATTN_REACH, BAND)
            kb, vb = k_ref[0, rows, :], v_ref[0, rows, :]
        qc = q_ref[0, pl.ds(r0, CHUNK), :]
        valid = None
        if masked:
            col = lax.broadcasted_iota(jnp.int32, (2 * CHUNK, BAND - skip), 1) + skip
            valid = col >= ATTN_REACH - r0
        o_ref[0, pl.ds(r0, CHUNK), :] = _attend(qc, kb, vb, bias_ref, valid, skip).astype(BF16)

    for c in range(LEFT_CHUNKS):
        seen = (c + 1) * CHUNK
        skip = (BAND - seen) // LANES * LANES
        chunk(c, skip, BAND - skip != seen, True)
    lax.fori_loop(LEFT_CHUNKS, t // CHUNK, lambda c, _: chunk(c, 0, False, False), None, unroll=6)


def _attn_prompt(qkv, bias, weights32):
    n, t, _ = qkv.shape
    assert t >= ATTN_REACH and t % CHUNK == 0
    assert all(w.shape[0] % (n * BF16_SUBLANES) == 0 for w in weights32)
    part = lambda j: pl.BlockSpec((1, t, D_ATTN), lambda i: (i, 0, j))
    slabs = [pl.BlockSpec((w.shape[0] // n, w.shape[1]), lambda i: (i, 0)) for w in weights32]
    out = pl.pallas_call(
        functools.partial(_attn_prompt_body, n_cast=len(weights32)),
        grid=(n,),
        in_specs=[part(0), part(1), part(2), _const_spec((N_HEADS // 2, 2 * CHUNK, BAND))] + slabs,
        out_specs=[pl.BlockSpec((1, t, D_ATTN), lambda i: (i, 0, 0))] + slabs,
        out_shape=[jax.ShapeDtypeStruct((n, t, D_ATTN), BF16)]
        + [jax.ShapeDtypeStruct(w.shape, BF16) for w in weights32],
        scratch_shapes=[pltpu.VMEM((BAND, D_ATTN), BF16), pltpu.VMEM((BAND, D_ATTN), BF16)],
        compiler_params=_params("arbitrary"),
        name="attn_prompt",
    )(qkv, qkv, qkv, bias, *weights32)
    return out[0], out[1:]


def _attn_sample_body(q_ref, k_ref, v_ref, bias_ref, o_ref):
    o_ref[0] = _attend(q_ref[0], k_ref[0], v_ref[0], bias_ref, None).astype(BF16)


def _attn_sample(q, kk, vv, bias):
    n, tq, _ = q.shape
    tk = kk.shape[1]
    return pl.pallas_call(
        _attn_sample_body,
        grid=(n,),
        in_specs=[pl.BlockSpec((1, tq, D_ATTN), lambda i: (i, 0, 0)),
                  pl.BlockSpec((1, tk, D_ATTN), lambda i: (i, 0, 0)),
                  pl.BlockSpec((1, tk, D_ATTN), lambda i: (i, 0, 0)),
                  _const_spec((N_HEADS // 2, 2 * tq, tk))],
        out_specs=pl.BlockSpec((1, tq, D_ATTN), lambda i: (i, 0, 0)),
        out_shape=jax.ShapeDtypeStruct((n, tq, D_ATTN), BF16),
        compiler_params=_params("arbitrary"),
        name="attn_sample",
    )(q, kk, vv, bias)


def _rel_bias(table, tq, tk):
    width = tq + tk - 1
    z = np.arange(width)
    idx = np.clip(ATTN_REACH - np.where(z < tk, z, z - width), -MAX_REL, MAX_REL) + MAX_REL
    ext = table.astype(F32)[:, idx]
    flat = jnp.tile(ext, (1, tq))[:, :tq * (width - 1)]
    bias = flat.reshape(N_HEADS, tq, width - 1)[:, :, :tk]
    return bias.reshape(N_HEADS // 2, 2 * tq, tk)


def _merge_blocks(ys_ref, att_ref, gate_ref, wglu, watt):
    tm = att_ref.shape[1]
    br_att = _dot(att_ref[0], watt[...])
    blocks = []
    for s in range(SUPER):
        frames = jnp.stack([ys_ref[s, :, j * LANES:(j + 1) * LANES].astype(F32) for j in range(S5_CHUNK)], axis=0)
        blocks.append(jnp.swapaxes(frames, 0, 1).reshape(tm, LANES))
    ys = jnp.concatenate(blocks, axis=-1).astype(BF16)

    for k in range(D_MODEL // MERGE_BLOCK):
        cols = slice(k * MERGE_BLOCK, (k + 1) * MERGE_BLOCK)
        gate_cols = slice(D_MODEL + k * MERGE_BLOCK, D_MODEL + (k + 1) * MERGE_BLOCK)
        br_ssm = _dot(ys, wglu[:, cols]) * _sigmoid(_dot(ys, wglu[:, gate_cols]))
        mix = (_sigmoid(gate_ref[0, :, cols].astype(F32)) * br_ssm
               + _sigmoid(gate_ref[0, :, gate_cols].astype(F32)) * br_att[:, cols])
        yield mix.astype(BF16)


def _merge_project(x_ref, blocks, wo):
    h = x_ref[0]
    blocks = iter(blocks)
    pending = next(blocks)
    for k in range(D_MODEL // MERGE_BLOCK):
        following = next(blocks, None)
        h = h + _dot(pending, wo[k * MERGE_BLOCK:(k + 1) * MERGE_BLOCK, :])
        pending = following
    return h


def _convglu_up(hn, wup):
    return [(_dot(hn, wup[:, lo:hi]), _dot(hn, wup[:, D_FF + lo:D_FF + hi])) for lo, hi in FF_CHUNKS]


def _convglu_down(ab, a_scr, cw, cb, wdown, history):
    tm = ab[0][0].shape[0]
    acc = jnp.zeros((tm, D_MODEL), F32)
    for (lo, hi), (a, b) in zip(FF_CHUNKS, ab):
        a_scr[8:8 + tm, 0:hi - lo] = a
        a1, a2 = history(a, lo, hi)
        c = cb[:, lo:hi] + cw[0:1, lo:hi] * a2 + cw[1:2, lo:hi] * a1 + cw[2:3, lo:hi] * a
        act = (_gelu(c) * b).astype(BF16)
        acc = acc + _dot(act, wdown[lo:hi, :])
    return acc


def _ffn_sample_body(x_ref, ys_ref, att_ref, gate_ref, h1_ref, h2_ref, wglu, watt, wo, gffn, wup, cw, cb, wdown,
                     gfin, y_ref, conv_ref, a_scr, *, seq_len):
    tm = x_ref.shape[1]
    h = _merge_project(x_ref, _merge_blocks(ys_ref, att_ref, gate_ref, wglu, watt), wo)
    hn = _rms(h, gffn[...]).astype(BF16)
    pos = lax.broadcasted_iota(jnp.int32, (tm, 1), 0) % seq_len

    def history(a, lo, hi):
        w = hi - lo
        a_scr[6:8, 0:w] = jnp.zeros((2, w), F32)
        conv_ref[0, :, lo:hi] = a
        a1 = jnp.where(pos < 1, h1_ref[:, lo:hi], a_scr[7:7 + tm, 0:w])
        a2 = jnp.where(pos < 2, h2_ref[:, lo:hi], a_scr[6:6 + tm, 0:w])
        return a1, a2

    down = _convglu_down(_convglu_up(hn, wup), a_scr, cw, cb, wdown, history)
    y_ref[0] = _rms(h + down, gfin[...])


def _ffn_prompt_body(x_ref, ys_ref, att_ref, gate_ref, wglu, watt, wo, gffn, wup, cw, cb, wdown, gfin,
                     y_ref, conv_ref, a_scr, carry):
    tm = x_ref.shape[1]

    @pl.when(pl.program_id(1) == 0)
    def _():
        carry[...] = jnp.zeros(carry.shape, F32)

    h = _merge_project(x_ref, _merge_blocks(ys_ref, att_ref, gate_ref, wglu, watt), wo)
    hn = _rms(h, gffn[...]).astype(BF16)

    def history(a, lo, hi):
        w = hi - lo
        a_scr[6:8, 0:w] = carry[:, lo:hi]
        carry[:, lo:hi] = a[tm - 2:tm, :]
        return a_scr[7:7 + tm, 0:w], a_scr[6:6 + tm, 0:w]

    down = _convglu_down(_convglu_up(hn, wup), a_scr, cw, cb, wdown, history)
    y_ref[0] = _rms(h + down, gfin[...])

    @pl.when(pl.program_id(1) == pl.num_programs(1) - 1)
    def _():
        conv_ref[0] = carry[...]


def _ffn_weight_specs():
    return [_const_spec((D_SSM, 2 * D_MODEL)), _const_spec((D_ATTN, D_MODEL)), _const_spec((D_MODEL, D_MODEL)),
            _const_spec((1, D_MODEL)), _const_spec((D_MODEL, 2 * D_FF)), _const_spec((CONV_W, D_FF)),
            _const_spec((1, D_FF)), _const_spec((D_FF, D_MODEL)), _const_spec((1, D_MODEL))]


def _ffn_prompt(x, ys, att, gates, wts, tm):
    n, t, _ = x.shape
    tiles = t // tm
    tok = lambda w: pl.BlockSpec((1, tm, w), lambda i, j: (i, j, 0))
    ys_spec = pl.BlockSpec((SUPER, tm // S5_CHUNK, S5_ROW), lambda i, j: (0, i * tiles + j, 0))
    seq = pl.BlockSpec((1, CONV_W - 1, D_FF), lambda i, j: (i, 0, 0))
    fc = max(hi - lo for lo, hi in FF_CHUNKS)
    return pl.pallas_call(
        _ffn_prompt_body,
        grid=(n, tiles),
        in_specs=[tok(D_MODEL), ys_spec, tok(D_ATTN), tok(2 * D_MODEL)] + _ffn_weight_specs(),
        out_specs=[tok(D_MODEL), seq],
        out_shape=[jax.ShapeDtypeStruct((n, t, D_MODEL), F32),
                   jax.ShapeDtypeStruct((n, CONV_W - 1, D_FF), F32)],
        scratch_shapes=[pltpu.VMEM((tm + 8, fc), F32), pltpu.VMEM((CONV_W - 1, D_FF), F32)],
        compiler_params=_params("arbitrary", "arbitrary"),
        name="merge_ffn_prompt",
    )(x, ys, att, gates, *wts)


def _ffn_sample(x, ys, att, gates, h1, h2, wts, seq_len):
    rows = x.shape[1]
    tok = lambda w: pl.BlockSpec((1, rows, w), lambda i: (0, 0, 0))
    ys_spec = pl.BlockSpec((SUPER, rows // S5_CHUNK, S5_ROW), lambda i: (0, 0, 0))
    full = pl.BlockSpec((rows, D_FF), lambda i: (0, 0))
    fc = max(hi - lo for lo, hi in FF_CHUNKS)
    return pl.pallas_call(
        functools.partial(_ffn_sample_body, seq_len=seq_len),
        grid=(1,),
        in_specs=[tok(D_MODEL), ys_spec, tok(D_ATTN), tok(2 * D_MODEL), full, full] + _ffn_weight_specs(),
        out_specs=[tok(D_MODEL), tok(D_FF)],
        out_shape=[jax.ShapeDtypeStruct((1, rows, D_MODEL), F32),
                   jax.ShapeDtypeStruct((1, rows, D_FF), F32)],
        scratch_shapes=[pltpu.VMEM((rows + 8, fc), F32)],
        compiler_params=_params("arbitrary"),
        name="merge_ffn_sample",
    )(x, ys, att, gates, h1, h2, *wts)


PROMPT_TOKEN_TILE = 1024
FFN_TOKEN_TILE = 512
SAMPLE_STREAM_PAD = 16


def kernel(x_prompt, x_sample, state_ssm_re, state_ssm_im, cache_attn_k, cache_attn_v, cache_conv,
           g_mix, w_in, ssm_lambda_re, ssm_lambda_im, ssm_log_dt, ssm_b_re, ssm_b_im,
           ssm_c_re, ssm_c_im, ssm_d, w_ssm_glu, attn_rel_bias, w_attn_up, w_o,
           g_ffn, w_up, conv_w, conv_b, w_down, g_final):
    depth = w_in.shape[0]
    nb, seq, _ = x_prompt.shape
    db, dseq, _ = x_sample.shape
    kept = min(ATTN_REACH, seq)
    assert depth == 1 and seq % PROMPT_TOKEN_TILE == 0 and seq % FFN_TOKEN_TILE == 0
    assert seq % S5_CHUNK == 0 and dseq == S5_CHUNK and nb % S5_PROMPT_SEQS == 0 and db <= SAMPLE_STREAM_PAD
    l = 0

    log2e = math.log2(math.e)
    col_scale = jnp.ones((D_IN,), F32).at[D_SSM:D_SSM + D_ATTN].set(HEAD_DIM ** -0.5 * log2e)
    w_in_b = (w_in[l] * col_scale[None, :]).astype(BF16)
    later_weights = (w_ssm_glu[l], w_attn_up[l], w_o[l], w_up[l], w_down[l])
    g_mix_l = g_mix[l][None, :]
    w_s5 = _s5_weights(ssm_lambda_re[l], ssm_lambda_im[l], ssm_log_dt[l], ssm_b_re[l], ssm_b_im[l],
                       ssm_c_re[l], ssm_c_im[l], ssm_d[l])
    table = attn_rel_bias[l].astype(F32) * log2e
    bias_p = _rel_bias(table, CHUNK, BAND)
    w_cache = cache_attn_k.shape[2]
    assert w_cache == ATTN_REACH
    bias_s = _rel_bias(table, dseq, w_cache + dseq)

    u, qkv, gates, k32, v32 = _inproj(x_prompt, g_mix_l, w_in_b, PROMPT_TOKEN_TILE, kept)
    n_chunks = seq // S5_CHUNK
    tiles = nb // S5_PROMPT_SEQS
    zero_state = jnp.zeros((SUPER, tiles, S5_PROMPT_SEQS, SUPER_STATE), F32)
    ys, p_state = _s5(u, w_s5, zero_state, S5_PROMPT_SEQS, n_chunks)
    p_re, p_im = _state_from_super(p_state, nb)
    att, (w_glu_b, w_att_b, w_o_b, w_up_b, w_down_b) = _attn_prompt(qkv, bias_p, later_weights)
    wts = (w_glu_b, w_att_b, w_o_b, g_ffn[l][None, :], w_up_b, conv_w[l], conv_b[l][None, :], w_down_b,
           g_final[None, :])
    y_prompt, p_conv = _ffn_prompt(x_prompt, ys, att, gates, wts, FFN_TOKEN_TILE)

    rows = db * dseq
    xs = x_sample.reshape(1, rows, D_MODEL)
    us, qkvs, gates_s, ks32, vs32 = _inproj(xs, g_mix_l, w_in_b, rows, rows)
    us2 = jnp.pad(us, ((0, 0), (0, SAMPLE_STREAM_PAD - db), (0, 0)))
    s0 = _state_to_super(state_ssm_re[l], state_ssm_im[l], 1, SAMPLE_STREAM_PAD)
    yss, s_state = _s5(us2, w_s5, s0, SAMPLE_STREAM_PAD, 1)
    s_re, s_im = _state_from_super(s_state, db)
    yss = yss[:, :db]
    qkvs3 = qkvs.reshape(db, dseq, 3 * D_ATTN)
    kk = jnp.concatenate([cache_attn_k[l].reshape(db, w_cache, D_ATTN).astype(BF16),
                          qkvs3[:, :, D_ATTN:2 * D_ATTN]], axis=1)
    vv = jnp.concatenate([cache_attn_v[l].reshape(db, w_cache, D_ATTN).astype(BF16),
                          qkvs3[:, :, 2 * D_ATTN:]], axis=1)
    att_s = _attn_sample(qkvs3[:, :, :D_ATTN], kk, vv, bias_s)
    hist = cache_conv[l].astype(F32)
    pad_rows = ((0, 0), (0, dseq - 1), (0, 0))
    h1 = jnp.pad(hist[:, 1:2], pad_rows).reshape(rows, D_FF)
    h2 = jnp.pad(hist, ((0, 0), (0, dseq - 2), (0, 0))).reshape(rows, D_FF)
    y_s, a_s = _ffn_sample(xs, yss, att_s.reshape(1, rows, D_ATTN), gates_s, h1, h2, wts, dseq)
    s_conv = a_s.reshape(db, dseq, D_FF)[:, dseq - (CONV_W - 1):]

    heads = lambda a, n, t: a.reshape(1, n, t, N_HEADS, HEAD_DIM)
    return (y_prompt, y_s.reshape(db, dseq, D_MODEL),
            p_re[None], p_im[None],
            heads(k32, nb, kept), heads(v32, nb, kept), p_conv[None],
            s_re[None], s_im[None],
            heads(ks32, db, dseq), heads(vs32, db, dseq), s_conv[None])
```

```python
import functools
import math

import jax
import jax.numpy as jnp
import numpy as np
from jax import lax
from jax.experimental import pallas as pl
from jax.experimental.pallas import tpu as pltpu

F32 = jnp.float32
BF16 = jnp.bfloat16

D_MODEL = 1024
CHUNK = 64
LEFT_CHUNKS = 8
ATTN_REACH = LEFT_CHUNKS * CHUNK
BAND = ATTN_REACH + CHUNK
D_SSM = D_MODEL // 2
D_ATTN = D_MODEL // 2
HEAD_DIM = 64
N_HEADS = D_ATTN // HEAD_DIM
MAX_REL = 128
SSM_GROUP = 16
N_GROUPS = D_SSM // SSM_GROUP
SSM_STATE = 64
D_FF = 2816
CONV_W = 3
D_IN = D_SSM + 3 * D_ATTN + 2 * D_MODEL
RMS_EPS = 1e-6
MASK_VALUE = -1e30

V7X_VMEM_LIMIT_BYTES = 56 * 1024 * 1024
LANES = 128
BF16_SUBLANES = 16
MERGE_BLOCK = 256
FF_CHUNKS = ((0, 2816),)


def _params(*sem):
    return pltpu.CompilerParams(dimension_semantics=sem, vmem_limit_bytes=V7X_VMEM_LIMIT_BYTES)


def _const_spec(shape):
    zeros = (0,) * len(shape)
    return pl.BlockSpec(shape, lambda *_: zeros, pipeline_mode=pl.Buffered(1))


def _rms(x, g):
    return x * lax.rsqrt(jnp.mean(x * x, axis=-1, keepdims=True) + RMS_EPS) * g


def _gelu(x):
    k = math.sqrt(2.0 / math.pi)
    return (0.5 * x) * (1.0 + jnp.tanh(x * (k + (k * 0.044715) * (x * x))))


def _sigmoid(x):
    return 1.0 / (1.0 + jnp.exp(-x))


def _dot(a, b):
    return jnp.dot(a, b, preferred_element_type=F32)


S5_CHUNK = 16
SUPER = D_SSM // LANES
SUPER_GROUPS = N_GROUPS // SUPER
SUPER_STATE = 2 * SUPER_GROUPS * SSM_STATE
S5_ROW = S5_CHUNK * LANES
S5_PROMPT_SEQS = 8


def _inproj_body(x_ref, g_ref, w_ref, u_ref, qkv_ref, gate_ref, k32_ref, v32_ref):
    xn = _rms(x_ref[0], g_ref[...]).astype(BF16)

    def seg(lo, hi):
        return _dot(xn, w_ref[:, lo:hi])

    u = seg(0, D_SSM)
    n_rows = u.shape[0] // S5_CHUNK
    for s in range(SUPER):
        frames = u[:, s * LANES:(s + 1) * LANES].reshape(n_rows, S5_CHUNK, LANES)
        frames = jnp.swapaxes(frames, 0, 1)
        for j in range(S5_CHUNK):
            u_ref[s, :, j * LANES:(j + 1) * LANES] = frames[j].astype(BF16)
    q0 = D_SSM
    qkv = seg(q0, q0 + 3 * D_ATTN)
    k = qkv[:, D_ATTN:2 * D_ATTN]
    v = qkv[:, 2 * D_ATTN:3 * D_ATTN]
    qkv_ref[0] = qkv.astype(BF16)
    g0 = q0 + 3 * D_ATTN
    gate_ref[0] = seg(g0, g0 + 2 * D_MODEL).astype(BF16)

    @pl.when(pl.program_id(1) == pl.num_programs(1) - 1)
    def _():
        tm, kept = k.shape[0], k32_ref.shape[1]
        k32_ref[0] = k[tm - kept:, :]
        v32_ref[0] = v[tm - kept:, :]


def _inproj(x, g_mix, w_in_b, tm, kept):
    n, t, _ = x.shape
    assert kept <= tm
    tiles = t // tm
    n_rows = tm // S5_CHUNK
    tok = lambda w: pl.BlockSpec((1, tm, w), lambda i, j: (i, j, 0))
    keep = pl.BlockSpec((1, kept, D_ATTN), lambda i, j: (i, 0, 0))
    u_spec = pl.BlockSpec((SUPER, n_rows, S5_ROW), lambda i, j: (0, i * tiles + j, 0))
    return pl.pallas_call(
        _inproj_body,
        grid=(n, tiles),
        in_specs=[tok(D_MODEL), _const_spec((1, D_MODEL)), _const_spec((D_MODEL, D_IN))],
        out_specs=[u_spec, tok(3 * D_ATTN), tok(2 * D_MODEL), keep, keep],
        out_shape=[
            jax.ShapeDtypeStruct((SUPER, n * tiles * n_rows, S5_ROW), BF16),
            jax.ShapeDtypeStruct((n, t, 3 * D_ATTN), BF16),
            jax.ShapeDtypeStruct((n, t, 2 * D_MODEL), BF16),
            jax.ShapeDtypeStruct((n, kept, D_ATTN), F32),
            jax.ShapeDtypeStruct((n, kept, D_ATTN), F32),
        ],
        compiler_params=_params("arbitrary", "arbitrary"),
        name="inproj",
    )(x, g_mix, w_in_b)


def _s5_weights(lam_re, lam_im, log_dt, b_re, b_im, c_re, c_im, d_skip):
    L = S5_CHUNK
    hi = lax.Precision.HIGHEST
    f = lambda a: a.astype(F32)
    cmul = lambda ar, ai, br, bi: (ar * br - ai * bi, ar * bi + ai * br)
    a_re, a_im = f(lam_re), f(lam_im)
    dt = jnp.exp(f(log_dt))[:, None]
    tau = jnp.arange(L + 1, dtype=F32)[None, :, None]
    mag = jnp.exp((a_re * dt)[:, None, :] * tau)
    ang = (a_im * dt)[:, None, :] * tau
    pw_re, pw_im = mag * jnp.cos(ang), mag * jnp.sin(ang)
    n_re, n_im = pw_re[:, 1] - 1.0, pw_im[:, 1]
    den = a_re * a_re + a_im * a_im
    f_re, f_im = (n_re * a_re + n_im * a_im) / den, (n_im * a_re - n_re * a_im) / den
    bb_re, bb_im = cmul(f_re[..., None], f_im[..., None], f(b_re), f(b_im))
    cp_re, cp_im = cmul(f(c_re)[:, :, None, :], f(c_im)[:, :, None, :], pw_re[:, None], pw_im[:, None])
    kt = (jnp.einsum('gatp,gph->ghta', cp_re[:, :, :L], bb_re, precision=hi)
          - jnp.einsum('gatp,gph->ghta', cp_im[:, :, :L], bb_im, precision=hi))
    kt = kt.at[:, :, 0, :].add(f(d_skip)[:, :, None] * jnp.eye(SSM_GROUP, dtype=F32)[None])
    rev = L - 1 - np.arange(L)
    bt_re, bt_im = jnp.swapaxes(bb_re, 1, 2)[:, None], jnp.swapaxes(bb_im, 1, 2)[:, None]
    wst_re, wst_im = cmul(pw_re[:, rev][:, :, None, :], pw_im[:, rev][:, :, None, :], bt_re, bt_im)

    sg = lambda a: a.reshape((SUPER, SUPER_GROUPS) + a.shape[1:])
    kc = jnp.transpose(sg(kt), (0, 2, 3, 1, 4)).reshape(SUPER, SSM_GROUP, S5_ROW)
    even = (np.arange(SUPER_GROUPS) % 2 == 0)[None, None, :, None, None]

    def in_form(w):
        w = jnp.transpose(sg(w), (0, 2, 1, 3, 4))
        w = jnp.concatenate([jnp.where(even, w, 0.0), jnp.where(even, 0.0, w)], axis=-1)
        return w.reshape(SUPER, S5_ROW, 2 * SSM_STATE)

    def out_form(c):
        return jnp.transpose(sg(c), (0, 4, 3, 1, 2)).reshape(SUPER, SSM_STATE, S5_ROW)

    wc = jnp.stack([in_form(wst_re), in_form(wst_im)], axis=1)
    ic = jnp.stack([out_form(cp_re[:, :, 1:L + 1]), out_form(-cp_im[:, :, 1:L + 1])], axis=1)
    half = SUPER_STATE // 2
    lam = jnp.concatenate([pw_re[:, L].reshape(SUPER, half), pw_im[:, L].reshape(SUPER, half)], axis=-1)
    return kc.astype(BF16), wc.astype(BF16), ic.astype(BF16), lam[:, None, :]


def _s5_body(u_ref, kc_ref, wc_ref, ic_ref, lam_ref, s0_ref, y_ref, sfin_ref, toep, wst, wint, x_scr, prev_scr,
             *, ns, nc):
    half = SUPER_STATE // 2

    @pl.when(pl.program_id(1) == 0)
    def _():
        zero = jnp.zeros((), BF16)
        group_of = lambda shape, dim: (lax.broadcasted_iota(jnp.int32, shape, dim) % LANES) // SSM_GROUP
        col_group = group_of((SSM_GROUP, S5_ROW), 1)
        kc = kc_ref[0]
        bblk = jnp.concatenate([jnp.where(col_group == g, kc, zero) for g in range(SUPER_GROUPS)], axis=0)
        for j in range(S5_CHUNK):
            rows = slice(j * LANES, (j + 1) * LANES)
            if j:
                toep[rows, 0:j * LANES] = jnp.zeros((LANES, j * LANES), BF16)
            toep[rows, j * LANES:S5_ROW] = bblk[:, 0:S5_ROW - j * LANES]
        row_pair = group_of((S5_ROW, LANES), 0) // 2
        for part in range(2):
            w = wc_ref[0, part]
            for k in range(SUPER_GROUPS // 2):
                c0 = part * half + k * LANES
                wst[:, c0:c0 + LANES] = jnp.where(row_pair == k, w, zero)
        col_group = group_of((SSM_STATE, S5_ROW), 1)
        for part in range(2):
            c = ic_ref[0, part]
            for g in range(SUPER_GROUPS):
                r0 = part * half + g * SSM_STATE
                wint[r0:r0 + SSM_STATE, :] = jnp.where(col_group == g, c, zero)

    u = u_ref[0]
    x = _dot(u, wst[...])
    regroup = nc > 1
    if regroup:
        x = jnp.swapaxes(x.reshape(ns, nc, SUPER_STATE), 0, 1).reshape(ns * nc, SUPER_STATE)
    nblk = SUPER_STATE // LANES
    hblk = nblk // 2
    for k in range(nblk):
        x_scr[k] = x[:, k * LANES:(k + 1) * LANES]
    lam = [lam_ref[0, :, k * LANES:(k + 1) * LANES] for k in range(nblk)]

    def step(c, s):
        rows = pl.ds(pl.multiple_of(c * ns, ns), ns)
        nxt = []
        for k in range(nblk):
            prev_scr[k, rows, :] = s[k]
        for k in range(hblk):
            l_re, l_im, s_re, s_im = lam[k], lam[hblk + k], s[k], s[hblk + k]
            nxt.append((l_re * s_re - l_im * s_im + x_scr[k, rows, :],
                        l_re * s_im + l_im * s_re + x_scr[hblk + k, rows, :]))
        return tuple(p[0] for p in nxt) + tuple(p[1] for p in nxt)

    s0 = s0_ref[0, 0]
    s_fin = lax.fori_loop(0, nc, step, tuple(s0[:, k * LANES:(k + 1) * LANES] for k in range(nblk)),
                          unroll=min(nc, 4))
    for k in range(nblk):
        sfin_ref[0, 0, :, k * LANES:(k + 1) * LANES] = s_fin[k]

    prev = jnp.concatenate([prev_scr[k] for k in range(nblk)], axis=-1)
    if regroup:
        prev = jnp.swapaxes(prev.reshape(nc, ns, SUPER_STATE), 0, 1).reshape(ns * nc, SUPER_STATE)
    prev = prev.astype(BF16)
    blk = 2 * LANES
    for tb in range(S5_ROW // blk):
        cols = slice(tb * blk, (tb + 1) * blk)
        depth = (tb + 1) * blk
        y = _dot(u[:, 0:depth], toep[0:depth, cols]) + _dot(prev, wint[:, cols])
        y_ref[0, :, cols] = _gelu(y).astype(BF16)


def _s5(u2, weights, s0, ns, nc):
    kc, wc, ic, lam = weights
    tiles = s0.shape[1]
    r = ns * nc
    per_sg = lambda *dims: pl.BlockSpec((1,) + dims, lambda s, i: (s,) + (0,) * len(dims))
    rows = pl.BlockSpec((1, r, S5_ROW), lambda s, i: (s, i, 0))
    state = pl.BlockSpec((1, 1, ns, SUPER_STATE), lambda s, i: (s, i, 0, 0))
    return pl.pallas_call(
        functools.partial(_s5_body, ns=ns, nc=nc),
        grid=(SUPER, tiles),
        in_specs=[rows, per_sg(SSM_GROUP, S5_ROW), per_sg(2, S5_ROW, LANES), per_sg(2, SSM_STATE, S5_ROW),
                  per_sg(1, SUPER_STATE), state],
        out_specs=[rows, state],
        out_shape=[jax.ShapeDtypeStruct((SUPER, tiles * r, S5_ROW), BF16),
                   jax.ShapeDtypeStruct((SUPER, tiles, ns, SUPER_STATE), F32)],
        scratch_shapes=[pltpu.VMEM((S5_ROW, S5_ROW), BF16), pltpu.VMEM((S5_ROW, SUPER_STATE), BF16),
                        pltpu.VMEM((SUPER_STATE, S5_ROW), BF16),
                        pltpu.VMEM((SUPER_STATE // LANES, r, LANES), F32),
                        pltpu.VMEM((SUPER_STATE // LANES, r, LANES), F32)],
        compiler_params=_params("arbitrary", "arbitrary"),
        name="s5",
    )(u2, kc, wc, ic, lam, s0)


def _state_to_super(s_re, s_im, tiles, ns):
    n = s_re.shape[0]
    half = SUPER_STATE // 2
    s = jnp.concatenate([s_re.astype(F32).reshape(n, SUPER, half), s_im.astype(F32).reshape(n, SUPER, half)], -1)
    s = jnp.pad(jnp.transpose(s, (1, 0, 2)), ((0, 0), (0, tiles * ns - n), (0, 0)))
    return s.reshape(SUPER, tiles, ns, SUPER_STATE)


def _state_from_super(s, n):
    half = SUPER_STATE // 2
    s = jnp.transpose(s.reshape(SUPER, -1, SUPER_STATE)[:, :n], (1, 0, 2))
    return (s[:, :, :half].reshape(n, N_GROUPS, SSM_STATE), s[:, :, half:].reshape(n, N_GROUPS, SSM_STATE))


def _attend(qc, kb, vb, bias_ref, valid, skip=0):
    tq = qc.shape[0]
    left = lax.broadcasted_iota(jnp.int32, (tq, LANES), 1) < HEAD_DIM
    zero = jnp.zeros((), BF16)
    pairs = range(N_HEADS // 2)
    cols = [slice(hp * LANES, (hp + 1) * LANES) for hp in pairs]
    scores = []
    for hp in pairs:
        qp = qc[:, cols[hp]]
        q2 = jnp.concatenate([jnp.where(left, qp, zero), jnp.where(left, zero, qp)], axis=0)
        sc = lax.dot_general(q2, kb[:, cols[hp]], (((1,), (1,)), ((), ())), preferred_element_type=F32)
        sc = sc + bias_ref[hp, :, skip:]
        if valid is not None:
            sc = jnp.where(valid, sc, MASK_VALUE)
        scores.append(sc)
    probs, dens = [], []
    for sc in scores:
        e = jnp.exp2(sc - jnp.max(sc, axis=-1, keepdims=True))
        dens.append(jnp.sum(e, axis=-1, keepdims=True))
        probs.append(e.astype(BF16))
    outs = []
    for hp in pairs:
        o2 = _dot(probs[hp], vb[:, cols[hp]]) / dens[hp]
        outs.append(jnp.where(left, o2[:tq], o2[tq:]))
    return jnp.concatenate(outs, axis=-1)


def _attn_prompt_body(q_ref, k_ref, v_ref, bias_ref, *rest, n_cast):
    w32_refs, o_ref, w16_refs, (kpad, vpad) = rest[:n_cast], rest[n_cast], rest[n_cast + 1:2 * n_cast + 1], rest[-2:]
    for w32, w16 in zip(w32_refs, w16_refs):
        w16[...] = w32[...].astype(BF16)

    t = q_ref.shape[1]
    kpad[0:CHUNK, :] = jnp.zeros((CHUNK, D_ATTN), BF16)
    vpad[0:CHUNK, :] = jnp.zeros((CHUNK, D_ATTN), BF16)
    kpad[CHUNK:BAND, :] = k_ref[0, 0:ATTN_REACH, :]
    vpad[CHUNK:BAND, :] = v_ref[0, 0:ATTN_REACH, :]

    def chunk(c, skip, masked, head):
        if head:
            r0 = c * CHUNK
            rows = pl.ds(r0 + skip - ATTN_REACH + CHUNK, BAND - skip)
            kb, vb = kpad[rows, :], vpad[rows, :]
        else:
            r0 = c * CHUNK
            if not isinstance(r0, int):
                r0 = pl.multiple_of(r0, CHUNK)
            rows = pl.ds(r0 - ATTN_REACH, BAND)
            kb, vb = k_ref[0, rows, :], v_ref[0, rows, :]
        qc = q_ref[0, pl.ds(r0, CHUNK), :]
        valid = None
        if masked:
            col = lax.broadcasted_iota(jnp.int32, (2 * CHUNK, BAND - skip), 1) + skip
            valid = col >= ATTN_REACH - r0
        o_ref[0, pl.ds(r0, CHUNK), :] = _attend(qc, kb, vb, bias_ref, valid, skip).astype(BF16)

    for c in range(LEFT_CHUNKS):
        seen = (c + 1) * CHUNK
        skip = (BAND - seen) // LANES * LANES
        chunk(c, skip, BAND - skip != seen, True)
    lax.fori_loop(LEFT_CHUNKS, t // CHUNK, lambda c, _: chunk(c, 0, False, False), None, unroll=6)


def _attn_prompt(qkv, bias, weights32):
    n, t, _ = qkv.shape
    assert t >= ATTN_REACH and t % CHUNK == 0
    assert all(w.shape[0] % (n * BF16_SUBLANES) == 0 for w in weights32)
    part = lambda j: pl.BlockSpec((1, t, D_ATTN), lambda i: (i, 0, j))
    slabs = [pl.BlockSpec((w.shape[0] // n, w.shape[1]), lambda i: (i, 0)) for w in weights32]
    out = pl.pallas_call(
        functools.partial(_attn_prompt_body, n_cast=len(weights32)),
        grid=(n,),
        in_specs=[part(0), part(1), part(2), _const_spec((N_HEADS // 2, 2 * CHUNK, BAND))] + slabs,
        out_specs=[pl.BlockSpec((1, t, D_ATTN), lambda i: (i, 0, 0))] + slabs,
        out_shape=[jax.ShapeDtypeStruct((n, t, D_ATTN), BF16)]
        + [jax.ShapeDtypeStruct(w.shape, BF16) for w in weights32],
        scratch_shapes=[pltpu.VMEM((BAND, D_ATTN), BF16), pltpu.VMEM((BAND, D_ATTN), BF16)],
        compiler_params=_params("arbitrary"),
        name="attn_prompt",
    )(qkv, qkv, qkv, bias, *weights32)
    return out[0], out[1:]


def _attn_sample_body(qkv_ref, ck_ref, cv_ref, bias_ref, o_ref):
    q = qkv_ref[0, :, 0:D_ATTN]
    kk = jnp.concatenate([ck_ref[0].astype(BF16), qkv_ref[0, :, D_ATTN:2 * D_ATTN]], axis=0)
    vv = jnp.concatenate([cv_ref[0].astype(BF16), qkv_ref[0, :, 2 * D_ATTN:3 * D_ATTN]], axis=0)
    o_ref[0] = _attend(q, kk, vv, bias_ref, None).astype(BF16)


def _attn_sample(qkv, cache_k, cache_v, bias):
    n, tq, _ = qkv.shape
    tc = cache_k.shape[1]
    cached = pl.BlockSpec((1, tc, D_ATTN), lambda i: (i, 0, 0))
    return pl.pallas_call(
        _attn_sample_body,
        grid=(n,),
        in_specs=[pl.BlockSpec((1, tq, 3 * D_ATTN), lambda i: (i, 0, 0)), cached, cached,
                  _const_spec((N_HEADS // 2, 2 * tq, tc + tq))],
        out_specs=pl.BlockSpec((1, tq, D_ATTN), lambda i: (i, 0, 0)),
        out_shape=jax.ShapeDtypeStruct((n, tq, D_ATTN), BF16),
        compiler_params=_params("arbitrary"),
        name="attn_sample",
    )(qkv, cache_k, cache_v, bias)


def _rel_bias(table, tq, tk):
    width = tq + tk - 1
    z = np.arange(width)
    idx = np.clip(ATTN_REACH - np.where(z < tk, z, z - width), -MAX_REL, MAX_REL) + MAX_REL
    ext = table.astype(F32)[:, idx]
    flat = jnp.tile(ext, (1, tq))[:, :tq * (width - 1)]
    bias = flat.reshape(N_HEADS, tq, width - 1)[:, :, :tk]
    return bias.reshape(N_HEADS // 2, 2 * tq, tk)


def _merge_blocks(ys_ref, att_ref, gate_ref, wglu, watt):
    tm = att_ref.shape[1]
    br_att = _dot(att_ref[0], watt[...])
    blocks = []
    for s in range(SUPER):
        frames = jnp.stack([ys_ref[s, :, j * LANES:(j + 1) * LANES].astype(F32) for j in range(S5_CHUNK)], axis=0)
        blocks.append(jnp.swapaxes(frames, 0, 1).reshape(tm, LANES))
    ys = jnp.concatenate(blocks, axis=-1).astype(BF16)

    for k in range(D_MODEL // MERGE_BLOCK):
        cols = slice(k * MERGE_BLOCK, (k + 1) * MERGE_BLOCK)
        gate_cols = slice(D_MODEL + k * MERGE_BLOCK, D_MODEL + (k + 1) * MERGE_BLOCK)
        br_ssm = _dot(ys, wglu[:, cols]) * _sigmoid(_dot(ys, wglu[:, gate_cols]))
        mix = (_sigmoid(gate_ref[0, :, cols].astype(F32)) * br_ssm
               + _sigmoid(gate_ref[0, :, gate_cols].astype(F32)) * br_att[:, cols])
        yield mix.astype(BF16)


def _merge_project(x_ref, blocks, wo):
    h = x_ref[0]
    blocks = iter(blocks)
    pending = next(blocks)
    for k in range(D_MODEL // MERGE_BLOCK):
        following = next(blocks, None)
        h = h + _dot(pending, wo[k * MERGE_BLOCK:(k + 1) * MERGE_BLOCK, :])
        pending = following
    return h


def _convglu_up(hn, wup):
    return [(_dot(hn, wup[:, lo:hi]), _dot(hn, wup[:, D_FF + lo:D_FF + hi])) for lo, hi in FF_CHUNKS]


def _convglu_down(ab, a_scr, cw, cb, wdown, history):
    tm = ab[0][0].shape[0]
    acc = jnp.zeros((tm, D_MODEL), F32)
    for (lo, hi), (a, b) in zip(FF_CHUNKS, ab):
        a_scr[8:8 + tm, 0:hi - lo] = a
        a1, a2 = history(a, lo, hi)
        c = cb[:, lo:hi] + cw[0:1, lo:hi] * a2 + cw[1:2, lo:hi] * a1 + cw[2:3, lo:hi] * a
        act = (_gelu(c) * b).astype(BF16)
        acc = acc + _dot(act, wdown[lo:hi, :])
    return acc


def _ffn_sample_body(x_ref, ys_ref, att_ref, gate_ref, h1_ref, h2_ref, wglu, watt, wo, gffn, wup, cw, cb, wdown,
                     gfin, y_ref, conv_ref, a_scr, *, seq_len):
    tm = x_ref.shape[1]
    h = _merge_project(x_ref, _merge_blocks(ys_ref, att_ref, gate_ref, wglu, watt), wo)
    hn = _rms(h, gffn[...]).astype(BF16)
    pos = lax.broadcasted_iota(jnp.int32, (tm, 1), 0) % seq_len

    def history(a, lo, hi):
        w = hi - lo
        a_scr[6:8, 0:w] = jnp.zeros((2, w), F32)
        conv_ref[0, :, lo:hi] = a
        a1 = jnp.where(pos < 1, h1_ref[:, lo:hi], a_scr[7:7 + tm, 0:w])
        a2 = jnp.where(pos < 2, h2_ref[:, lo:hi], a_scr[6:6 + tm, 0:w])
        return a1, a2

    down = _convglu_down(_convglu_up(hn, wup), a_scr, cw, cb, wdown, history)
    y_ref[0] = _rms(h + down, gfin[...])


def _ffn_prompt_body(x_ref, ys_ref, att_ref, gate_ref, wglu, watt, wo, gffn, wup, cw, cb, wdown, gfin,
                     y_ref, conv_ref, a_scr, carry):
    tm = x_ref.shape[1]

    @pl.when(pl.program_id(1) == 0)
    def _():
        carry[...] = jnp.zeros(carry.shape, F32)

    h = _merge_project(x_ref, _merge_blocks(ys_ref, att_ref, gate_ref, wglu, watt), wo)
    hn = _rms(h, gffn[...]).astype(BF16)

    def history(a, lo, hi):
        w = hi - lo
        a_scr[6:8, 0:w] = carry[:, lo:hi]
        carry[:, lo:hi] = a[tm - 2:tm, :]
        return a_scr[7:7 + tm, 0:w], a_scr[6:6 + tm, 0:w]

    down = _convglu_down(_convglu_up(hn, wup), a_scr, cw, cb, wdown, history)
    y_ref[0] = _rms(h + down, gfin[...])

    @pl.when(pl.program_id(1) == pl.num_programs(1) - 1)
    def _():
        conv_ref[0] = carry[...]


def _ffn_weight_specs():
    return [_const_spec((D_SSM, 2 * D_MODEL)), _const_spec((D_ATTN, D_MODEL)), _const_spec((D_MODEL, D_MODEL)),
            _const_spec((1, D_MODEL)), _const_spec((D_MODEL, 2 * D_FF)), _const_spec((CONV_W, D_FF)),
            _const_spec((1, D_FF)), _const_spec((D_FF, D_MODEL)), _const_spec((1, D_MODEL))]


def _ffn_prompt(x, ys, att, gates, wts, tm):
    n, t, _ = x.shape
    tiles = t // tm
    tok = lambda w: pl.BlockSpec((1, tm, w), lambda i, j: (i, j, 0))
    ys_spec = pl.BlockSpec((SUPER, tm // S5_CHUNK, S5_ROW), lambda i, j: (0, i * tiles + j, 0))
    seq = pl.BlockSpec((1, CONV_W - 1, D_FF), lambda i, j: (i, 0, 0))
    fc = max(hi - lo for lo, hi in FF_CHUNKS)
    return pl.pallas_call(
        _ffn_prompt_body,
        grid=(n, tiles),
        in_specs=[tok(D_MODEL), ys_spec, tok(D_ATTN), tok(2 * D_MODEL)] + _ffn_weight_specs(),
        out_specs=[tok(D_MODEL), seq],
        out_shape=[jax.ShapeDtypeStruct((n, t, D_MODEL), F32),
                   jax.ShapeDtypeStruct((n, CONV_W - 1, D_FF), F32)],
        scratch_shapes=[pltpu.VMEM((tm + 8, fc), F32), pltpu.VMEM((CONV_W - 1, D_FF), F32)],
        compiler_params=_params("arbitrary", "arbitrary"),
        name="merge_ffn_prompt",
    )(x, ys, att, gates, *wts)


def _ffn_sample(x, ys, att, gates, h1, h2, wts, seq_len):
    rows = x.shape[1]
    tok = lambda w: pl.BlockSpec((1, rows, w), lambda i: (0, 0, 0))
    ys_spec = pl.BlockSpec((SUPER, rows // S5_CHUNK, S5_ROW), lambda i: (0, 0, 0))
    full = pl.BlockSpec((rows, D_FF), lambda i: (0, 0))
    fc = max(hi - lo for lo, hi in FF_CHUNKS)
    return pl.pallas_call(
        functools.partial(_ffn_sample_body, seq_len=seq_len),
        grid=(1,),
        in_specs=[tok(D_MODEL), ys_spec, tok(D_ATTN), tok(2 * D_MODEL), full, full] + _ffn_weight_specs(),
        out_specs=[tok(D_MODEL), tok(D_FF)],
        out_shape=[jax.ShapeDtypeStruct((1, rows, D_MODEL), F32),
                   jax.ShapeDtypeStruct((1, rows, D_FF), F32)],
        scratch_shapes=[pltpu.VMEM((rows + 8, fc), F32)],
        compiler_params=_params("arbitrary"),
        name="merge_ffn_sample",
    )(x, ys, att, gates, h1, h2, *wts)


PROMPT_TOKEN_TILE = 1024
FFN_TOKEN_TILE = 512
SAMPLE_STREAM_PAD = 16


def kernel(x_prompt, x_sample, state_ssm_re, state_ssm_im, cache_attn_k, cache_attn_v, cache_conv,
           g_mix, w_in, ssm_lambda_re, ssm_lambda_im, ssm_log_dt, ssm_b_re, ssm_b_im,
           ssm_c_re, ssm_c_im, ssm_d, w_ssm_glu, attn_rel_bias, w_attn_up, w_o,
           g_ffn, w_up, conv_w, conv_b, w_down, g_final):
    depth = w_in.shape[0]
    nb, seq, _ = x_prompt.shape
    db, dseq, _ = x_sample.shape
    kept = min(ATTN_REACH, seq)
    assert depth == 1 and seq % PROMPT_TOKEN_TILE == 0 and seq % FFN_TOKEN_TILE == 0
    assert seq % S5_CHUNK == 0 and dseq == S5_CHUNK and nb % S5_PROMPT_SEQS == 0 and db <= SAMPLE_STREAM_PAD
    l = 0

    log2e = math.log2(math.e)
    col_scale = jnp.ones((D_IN,), F32).at[D_SSM:D_SSM + D_ATTN].set(HEAD_DIM ** -0.5 * log2e)
    w_in_b = (w_in[l] * col_scale[None, :]).astype(BF16)
    later_weights = (w_ssm_glu[l], w_attn_up[l], w_o[l], w_up[l], w_down[l])
    g_mix_l = g_mix[l][None, :]
    w_s5 = _s5_weights(ssm_lambda_re[l], ssm_lambda_im[l], ssm_log_dt[l], ssm_b_re[l], ssm_b_im[l],
                       ssm_c_re[l], ssm_c_im[l], ssm_d[l])
    table = attn_rel_bias[l].astype(F32) * log2e
    bias_p = _rel_bias(table, CHUNK, BAND)
    w_cache = cache_attn_k.shape[2]
    assert w_cache == ATTN_REACH
    bias_s = _rel_bias(table, dseq, w_cache + dseq)

    u, qkv, gates, k32, v32 = _inproj(x_prompt, g_mix_l, w_in_b, PROMPT_TOKEN_TILE, kept)
    n_chunks = seq // S5_CHUNK
    tiles = nb // S5_PROMPT_SEQS
    zero_state = jnp.zeros((SUPER, tiles, S5_PROMPT_SEQS, SUPER_STATE), F32)
    ys, p_state = _s5(u, w_s5, zero_state, S5_PROMPT_SEQS, n_chunks)
    p_re, p_im = _state_from_super(p_state, nb)
    att, (w_glu_b, w_att_b, w_o_b, w_up_b, w_down_b) = _attn_prompt(qkv, bias_p, later_weights)
    wts = (w_glu_b, w_att_b, w_o_b, g_ffn[l][None, :], w_up_b, conv_w[l], conv_b[l][None, :], w_down_b,
           g_final[None, :])
    y_prompt, p_conv = _ffn_prompt(x_prompt, ys, att, gates, wts, FFN_TOKEN_TILE)

    rows = db * dseq
    xs = x_sample.reshape(1, rows, D_MODEL)
    us, qkvs, gates_s, ks32, vs32 = _inproj(xs, g_mix_l, w_in_b, rows, rows)
    us2 = jnp.pad(us, ((0, 0), (0, SAMPLE_STREAM_PAD - db), (0, 0)))
    s0 = _state_to_super(state_ssm_re[l], state_ssm_im[l], 1, SAMPLE_STREAM_PAD)
    yss, s_state = _s5(us2, w_s5, s0, SAMPLE_STREAM_PAD, 1)
    s_re, s_im = _state_from_super(s_state, db)
    yss = yss[:, :db]
    att_s = _attn_sample(qkvs.reshape(db, dseq, 3 * D_ATTN), cache_attn_k[l].reshape(db, w_cache, D_ATTN),
                         cache_attn_v[l].reshape(db, w_cache, D_ATTN), bias_s)
    hist = cache_conv[l].astype(F32)
    pad_rows = ((0, 0), (0, dseq - 1), (0, 0))
    h1 = jnp.pad(hist[:, 1:2], pad_rows).reshape(rows, D_FF)
    h2 = jnp.pad(hist, ((0, 0), (0, dseq - 2), (0, 0))).reshape(rows, D_FF)
    y_s, a_s = _ffn_sample(xs, yss, att_s.reshape(1, rows, D_ATTN), gates_s, h1, h2, wts, dseq)
    s_conv = a_s.reshape(db, dseq, D_FF)[:, dseq - (CONV_W - 1):]

    heads = lambda a, n, t: a.reshape(1, n, t, N_HEADS, HEAD_DIM)
    return (y_prompt, y_s.reshape(db, dseq, D_MODEL),
            p_re[None], p_im[None],
            heads(k32, nb, kept), heads(v32, nb, kept), p_conv[None],
            s_re[None], s_im[None],
            heads(ks32, db, dseq), heads(vs32, db, dseq), s_conv[None])
```

```python
import functools
import math

import jax
import jax.numpy as jnp
import numpy as np
from jax import lax
from jax.experimental import pallas as pl
from jax.experimental.pallas import tpu as pltpu

F32 = jnp.float32
BF16 = jnp.bfloat16

D_MODEL = 1024
CHUNK = 64
LEFT_CHUNKS = 8
ATTN_REACH = LEFT_CHUNKS * CHUNK
BAND = ATTN_REACH + CHUNK
D_SSM = D_MODEL // 2
D_ATTN = D_MODEL // 2
HEAD_DIM = 64
N_HEADS = D_ATTN // HEAD_DIM
MAX_REL = 128
SSM_GROUP = 16
N_GROUPS = D_SSM // SSM_GROUP
SSM_STATE = 64
D_FF = 2816
CONV_W = 3
D_IN = D_SSM + 3 * D_ATTN + 2 * D_MODEL
RMS_EPS = 1e-6
MASK_VALUE = -1e30

V7X_VMEM_LIMIT_BYTES = 56 * 1024 * 1024
LANES = 128
BF16_SUBLANES = 16
MERGE_BLOCK = 256
FF_CHUNKS = ((0, 2816),)


def _params(*sem):
    return pltpu.CompilerParams(dimension_semantics=sem, vmem_limit_bytes=V7X_VMEM_LIMIT_BYTES)


def _const_spec(shape):
    zeros = (0,) * len(shape)
    return pl.BlockSpec(shape, lambda *_: zeros, pipeline_mode=pl.Buffered(1))


def _rms(x, g):
    return x * lax.rsqrt(jnp.mean(x * x, axis=-1, keepdims=True) + RMS_EPS) * g


def _gelu(x):
    k = math.sqrt(2.0 / math.pi)
    return (0.5 * x) * (1.0 + jnp.tanh(x * (k + (k * 0.044715) * (x * x))))


def _sigmoid(x):
    return 1.0 / (1.0 + jnp.exp(-x))


def _dot(a, b):
    return jnp.dot(a, b, preferred_element_type=F32)


S5_CHUNK = 16
SUPER = D_SSM // LANES
SUPER_GROUPS = N_GROUPS // SUPER
SUPER_STATE = 2 * SUPER_GROUPS * SSM_STATE
S5_ROW = S5_CHUNK * LANES
S5_PROMPT_SEQS = 8


def _inproj_body(x_ref, g_ref, w_ref, u_ref, qkv_ref, gate_ref, k32_ref, v32_ref):
    xn = _rms(x_ref[0], g_ref[...]).astype(BF16)

    def seg(lo, hi):
        return _dot(xn, w_ref[:, lo:hi])

    u = seg(0, D_SSM)
    n_rows = u.shape[0] // S5_CHUNK
    for s in range(SUPER):
        frames = u[:, s * LANES:(s + 1) * LANES].reshape(n_rows, S5_CHUNK, LANES)
        frames = jnp.swapaxes(frames, 0, 1)
        for j in range(S5_CHUNK):
            u_ref[s, :, j * LANES:(j + 1) * LANES] = frames[j].astype(BF16)
    q0 = D_SSM
    qkv = seg(q0, q0 + 3 * D_ATTN)
    k = qkv[:, D_ATTN:2 * D_ATTN]
    v = qkv[:, 2 * D_ATTN:3 * D_ATTN]
    qkv_ref[0] = qkv.astype(BF16)
    g0 = q0 + 3 * D_ATTN
    gate_ref[0] = seg(g0, g0 + 2 * D_MODEL).astype(BF16)

    @pl.when(pl.program_id(1) == pl.num_programs(1) - 1)
    def _():
        tm, kept = k.shape[0], k32_ref.shape[1]
        k32_ref[0] = k[tm - kept:, :]
        v32_ref[0] = v[tm - kept:, :]


def _inproj(x, g_mix, w_in_b, tm, kept):
    n, t, _ = x.shape
    assert kept <= tm
    tiles = t // tm
    n_rows = tm // S5_CHUNK
    tok = lambda w: pl.BlockSpec((1, tm, w), lambda i, j: (i, j, 0))
    keep = pl.BlockSpec((1, kept, D_ATTN), lambda i, j: (i, 0, 0))
    u_spec = pl.BlockSpec((SUPER, n_rows, S5_ROW), lambda i, j: (0, i * tiles + j, 0))
    return pl.pallas_call(
        _inproj_body,
        grid=(n, tiles),
        in_specs=[tok(D_MODEL), _const_spec((1, D_MODEL)), _const_spec((D_MODEL, D_IN))],
        out_specs=[u_spec, tok(3 * D_ATTN), tok(2 * D_MODEL), keep, keep],
        out_shape=[
            jax.ShapeDtypeStruct((SUPER, n * tiles * n_rows, S5_ROW), BF16),
            jax.ShapeDtypeStruct((n, t, 3 * D_ATTN), BF16),
            jax.ShapeDtypeStruct((n, t, 2 * D_MODEL), BF16),
            jax.ShapeDtypeStruct((n, kept, D_ATTN), F32),
            jax.ShapeDtypeStruct((n, kept, D_ATTN), F32),
        ],
        compiler_params=_params("arbitrary", "arbitrary"),
        name="inproj",
    )(x, g_mix, w_in_b)


def _s5_weights(lam_re, lam_im, log_dt, b_re, b_im, c_re, c_im, d_skip):
    L = S5_CHUNK
    hi = lax.Precision.HIGHEST
    f = lambda a: a.astype(F32)
    cmul = lambda ar, ai, br, bi: (ar * br - ai * bi, ar * bi + ai * br)
    a_re, a_im = f(lam_re), f(lam_im)
    dt = jnp.exp(f(log_dt))[:, None]
    tau = jnp.arange(L + 1, dtype=F32)[None, :, None]
    mag = jnp.exp((a_re * dt)[:, None, :] * tau)
    ang = (a_im * dt)[:, None, :] * tau
    pw_re, pw_im = mag * jnp.cos(ang), mag * jnp.sin(ang)
    n_re, n_im = pw_re[:, 1] - 1.0, pw_im[:, 1]
    den = a_re * a_re + a_im * a_im
    f_re, f_im = (n_re * a_re + n_im * a_im) / den, (n_im * a_re - n_re * a_im) / den
    bb_re, bb_im = cmul(f_re[..., None], f_im[..., None], f(b_re), f(b_im))
    cp_re, cp_im = cmul(f(c_re)[:, :, None, :], f(c_im)[:, :, None, :], pw_re[:, None], pw_im[:, None])
    kt = (jnp.einsum('gatp,gph->ghta', cp_re[:, :, :L], bb_re, precision=hi)
          - jnp.einsum('gatp,gph->ghta', cp_im[:, :, :L], bb_im, precision=hi))
    kt = kt.at[:, :, 0, :].add(f(d_skip)[:, :, None] * jnp.eye(SSM_GROUP, dtype=F32)[None])
    rev = L - 1 - np.arange(L)
    bt_re, bt_im = jnp.swapaxes(bb_re, 1, 2)[:, None], jnp.swapaxes(bb_im, 1, 2)[:, None]
    wst_re, wst_im = cmul(pw_re[:, rev][:, :, None, :], pw_im[:, rev][:, :, None, :], bt_re, bt_im)

    sg = lambda a: a.reshape((SUPER, SUPER_GROUPS) + a.shape[1:])
    kc = jnp.transpose(sg(kt), (0, 2, 3, 1, 4)).reshape(SUPER, SSM_GROUP, S5_ROW)
    even = (np.arange(SUPER_GROUPS) % 2 == 0)[None, None, :, None, None]

    def in_form(w):
        w = jnp.transpose(sg(w), (0, 2, 1, 3, 4))
        w = jnp.concatenate([jnp.where(even, w, 0.0), jnp.where(even, 0.0, w)], axis=-1)
        return w.reshape(SUPER, S5_ROW, 2 * SSM_STATE)

    def out_form(c):
        return jnp.transpose(sg(c), (0, 4, 3, 1, 2)).reshape(SUPER, SSM_STATE, S5_ROW)

    wc = jnp.stack([in_form(wst_re), in_form(wst_im)], axis=1)
    ic = jnp.stack([out_form(cp_re[:, :, 1:L + 1]), out_form(-cp_im[:, :, 1:L + 1])], axis=1)
    half = SUPER_STATE // 2
    lam = jnp.concatenate([pw_re[:, L].reshape(SUPER, half), pw_im[:, L].reshape(SUPER, half)], axis=-1)
    return kc.astype(BF16), wc.astype(BF16), ic.astype(BF16), lam[:, None, :]


def _s5_body(u_ref, kc_ref, wc_ref, ic_ref, lam_ref, s0_ref, y_ref, sfin_ref, toep, wst, wint, x_scr, prev_scr,
             *, ns, nc):
    half = SUPER_STATE // 2

    @pl.when(pl.program_id(1) == 0)
    def _():
        zero = jnp.zeros((), BF16)
        group_of = lambda shape, dim: (lax.broadcasted_iota(jnp.int32, shape, dim) % LANES) // SSM_GROUP
        col_group = group_of((SSM_GROUP, S5_ROW), 1)
        kc = kc_ref[0]
        bblk = jnp.concatenate([jnp.where(col_group == g, kc, zero) for g in range(SUPER_GROUPS)], axis=0)
        for j in range(S5_CHUNK):
            rows = slice(j * LANES, (j + 1) * LANES)
            if j:
                toep[rows, 0:j * LANES] = jnp.zeros((LANES, j * LANES), BF16)
            toep[rows, j * LANES:S5_ROW] = bblk[:, 0:S5_ROW - j * LANES]
        row_pair = group_of((S5_ROW, LANES), 0) // 2
        for part in range(2):
            w = wc_ref[0, part]
            for k in range(SUPER_GROUPS // 2):
                c0 = part * half + k * LANES
                wst[:, c0:c0 + LANES] = jnp.where(row_pair == k, w, zero)
        col_group = group_of((SSM_STATE, S5_ROW), 1)
        for part in range(2):
            c = ic_ref[0, part]
            for g in range(SUPER_GROUPS):
                r0 = part * half + g * SSM_STATE
                wint[r0:r0 + SSM_STATE, :] = jnp.where(col_group == g, c, zero)

    u = u_ref[0]
    x = _dot(u, wst[...])
    regroup = nc > 1
    if regroup:
        x = jnp.swapaxes(x.reshape(ns, nc, SUPER_STATE), 0, 1).reshape(ns * nc, SUPER_STATE)
    nblk = SUPER_STATE // LANES
    hblk = nblk // 2
    for k in range(nblk):
        x_scr[k] = x[:, k * LANES:(k + 1) * LANES]
    lam = [lam_ref[0, :, k * LANES:(k + 1) * LANES] for k in range(nblk)]

    def step(c, s):
        rows = pl.ds(pl.multiple_of(c * ns, ns), ns)
        nxt = []
        for k in range(nblk):
            prev_scr[k, rows, :] = s[k]
        for k in range(hblk):
            l_re, l_im, s_re, s_im = lam[k], lam[hblk + k], s[k], s[hblk + k]
            nxt.append((l_re * s_re - l_im * s_im + x_scr[k, rows, :],
                        l_re * s_im + l_im * s_re + x_scr[hblk + k, rows, :]))
        return tuple(p[0] for p in nxt) + tuple(p[1] for p in nxt)

    s0 = s0_ref[0, 0]
    s_fin = lax.fori_loop(0, nc, step, tuple(s0[:, k * LANES:(k + 1) * LANES] for k in range(nblk)),
                          unroll=min(nc, 4))
    for k in range(nblk):
        sfin_ref[0, 0, :, k * LANES:(k + 1) * LANES] = s_fin[k]

    prev = jnp.concatenate([prev_scr[k] for k in range(nblk)], axis=-1)
    if regroup:
        prev = jnp.swapaxes(prev.reshape(nc, ns, SUPER_STATE), 0, 1).reshape(ns * nc, SUPER_STATE)
    prev = prev.astype(BF16)
    blk = 2 * LANES
    for tb in range(S5_ROW // blk):
        cols = slice(tb * blk, (tb + 1) * blk)
        depth = (tb + 1) * blk
        y = _dot(u[:, 0:depth], toep[0:depth, cols]) + _dot(prev, wint[:, cols])
        y_ref[0, :, cols] = _gelu(y).astype(BF16)


def _s5(u2, weights, s0, ns, nc):
    kc, wc, ic, lam = weights
    tiles = s0.shape[1]
    r = ns * nc
    per_sg = lambda *dims: pl.BlockSpec((1,) + dims, lambda s, i: (s,) + (0,) * len(dims))
    rows = pl.BlockSpec((1, r, S5_ROW), lambda s, i: (s, i, 0))
    state = pl.BlockSpec((1, 1, ns, SUPER_STATE), lambda s, i: (s, i, 0, 0))
    return pl.pallas_call(
        functools.partial(_s5_body, ns=ns, nc=nc),
        grid=(SUPER, tiles),
        in_specs=[rows, per_sg(SSM_GROUP, S5_ROW), per_sg(2, S5_ROW, LANES), per_sg(2, SSM_STATE, S5_ROW),
                  per_sg(1, SUPER_STATE), state],
        out_specs=[rows, state],
        out_shape=[jax.ShapeDtypeStruct((SUPER, tiles * r, S5_ROW), BF16),
                   jax.ShapeDtypeStruct((SUPER, tiles, ns, SUPER_STATE), F32)],
        scratch_shapes=[pltpu.VMEM((S5_ROW, S5_ROW), BF16), pltpu.VMEM((S5_ROW, SUPER_STATE), BF16),
                        pltpu.VMEM((SUPER_STATE, S5_ROW), BF16),
                        pltpu.VMEM((SUPER_STATE // LANES, r, LANES), F32),
                        pltpu.VMEM((SUPER_STATE // LANES, r, LANES), F32)],
        compiler_params=_params("arbitrary", "arbitrary"),
        name="s5",
    )(u2, kc, wc, ic, lam, s0)


def _state_to_super(s_re, s_im, tiles, ns):
    n = s_re.shape[0]
    half = SUPER_STATE // 2
    s = jnp.concatenate([s_re.astype(F32).reshape(n, SUPER, half), s_im.astype(F32).reshape(n, SUPER, half)], -1)
    s = jnp.pad(jnp.transpose(s, (1, 0, 2)), ((0, 0), (0, tiles * ns - n), (0, 0)))
    return s.reshape(SUPER, tiles, ns, SUPER_STATE)


def _state_from_super(s, n):
    half = SUPER_STATE // 2
    s = jnp.transpose(s.reshape(SUPER, -1, SUPER_STATE)[:, :n], (1, 0, 2))
    return (s[:, :, :half].reshape(n, N_GROUPS, SSM_STATE), s[:, :, half:].reshape(n, N_GROUPS, SSM_STATE))


def _attend(qc, kb, vb, bias_ref, valid, skip=0):
    tq = qc.shape[0]
    left = lax.broadcasted_iota(jnp.int32, (tq, LANES), 1) < HEAD_DIM
    zero = jnp.zeros((), BF16)
    pairs = range(N_HEADS // 2)
    cols = [slice(hp * LANES, (hp + 1) * LANES) for hp in pairs]
    scores = []
    for hp in pairs:
        qp = qc[:, cols[hp]]
        q2 = jnp.concatenate([jnp.where(left, qp, zero), jnp.where(left, zero, qp)], axis=0)
        sc = lax.dot_general(q2, kb[:, cols[hp]], (((1,), (1,)), ((), ())), preferred_element_type=F32)
        sc = sc + bias_ref[hp, :, skip:]
        if valid is not None:
            sc = jnp.where(valid, sc, MASK_VALUE)
        scores.append(sc)
    probs, dens = [], []
    for sc in scores:
        e = jnp.exp2(sc - jnp.max(sc, axis=-1, keepdims=True))
        dens.append(jnp.sum(e, axis=-1, keepdims=True))
        probs.append(e.astype(BF16))
    outs = []
    for hp in pairs:
        o2 = _dot(probs[hp], vb[:, cols[hp]]) / dens[hp]
        outs.append(jnp.where(left, o2[:tq], o2[tq:]))
    return jnp.concatenate(outs, axis=-1)


def _attn_prompt_body(q_ref, k_ref, v_ref, bias_ref, *rest, n_cast):
    w32_refs, o_ref, w16_refs, (kpad, vpad) = rest[:n_cast], rest[n_cast], rest[n_cast + 1:2 * n_cast + 1], rest[-2:]
    for w32, w16 in zip(w32_refs, w16_refs):
        w16[...] = w32[...].astype(BF16)

    t = q_ref.shape[1]
    kpad[0:CHUNK, :] = jnp.zeros((CHUNK, D_ATTN), BF16)
    vpad[0:CHUNK, :] = jnp.zeros((CHUNK, D_ATTN), BF16)
    kpad[CHUNK:BAND, :] = k_ref[0, 0:ATTN_REACH, :]
    vpad[CHUNK:BAND, :] = v_ref[0, 0:ATTN_REACH, :]

    def chunk(c, skip, masked, head):
        if head:
            r0 = c * CHUNK
            rows = pl.ds(r0 + skip - ATTN_REACH + CHUNK, BAND - skip)
            kb, vb = kpad[rows, :], vpad[rows, :]
        else:
            r0 = c * CHUNK
            if not isinstance(r0, int):
                r0 = pl.multiple_of(r0, CHUNK)
            rows = pl.ds(r0 - ATTN_REACH, BAND)
            kb, vb = k_ref[0, rows, :], v_ref[0, rows, :]
        qc = q_ref[0, pl.ds(r0, CHUNK), :]
        valid = None
        if masked:
            col = lax.broadcasted_iota(jnp.int32, (2 * CHUNK, BAND - skip), 1) + skip
            valid = col >= ATTN_REACH - r0
        o_ref[0, pl.ds(r0, CHUNK), :] = _attend(qc, kb, vb, bias_ref, valid, skip).astype(BF16)

    for c in range(LEFT_CHUNKS):
        seen = (c + 1) * CHUNK
        skip = (BAND - seen) // LANES * LANES
        chunk(c, skip, BAND - skip != seen, True)
    lax.fori_loop(LEFT_CHUNKS, t // CHUNK, lambda c, _: chunk(c, 0, False, False), None, unroll=6)


def _attn_prompt(qkv, bias, weights32):
    n, t, _ = qkv.shape
    assert t >= ATTN_REACH and t % CHUNK == 0
    assert all(w.shape[0] % (n * BF16_SUBLANES) == 0 for w in weights32)
    part = lambda j: pl.BlockSpec((1, t, D_ATTN), lambda i: (i, 0, j))
    slabs = [pl.BlockSpec((w.shape[0] // n, w.shape[1]), lambda i: (i, 0)) for w in weights32]
    out = pl.pallas_call(
        functools.partial(_attn_prompt_body, n_cast=len(weights32)),
        grid=(n,),
        in_specs=[part(0), part(1), part(2), _const_spec((N_HEADS // 2, 2 * CHUNK, BAND))] + slabs,
        out_specs=[pl.BlockSpec((1, t, D_ATTN), lambda i: (i, 0, 0))] + slabs,
        out_shape=[jax.ShapeDtypeStruct((n, t, D_ATTN), BF16)]
        + [jax.ShapeDtypeStruct(w.shape, BF16) for w in weights32],
        scratch_shapes=[pltpu.VMEM((BAND, D_ATTN), BF16), pltpu.VMEM((BAND, D_ATTN), BF16)],
        compiler_params=_params("arbitrary"),
        name="attn_prompt",
    )(qkv, qkv, qkv, bias, *weights32)
    return out[0], out[1:]


def _attn_sample_body(qkv_ref, ck_ref, cv_ref, bias_ref, o_ref):
    q = qkv_ref[0, :, 0:D_ATTN]
    kk = jnp.concatenate([ck_ref[0].astype(BF16), qkv_ref[0, :, D_ATTN:2 * D_ATTN]], axis=0)
    vv = jnp.concatenate([cv_ref[0].astype(BF16), qkv_ref[0, :, 2 * D_ATTN:3 * D_ATTN]], axis=0)
    o_ref[0] = _attend(q, kk, vv, bias_ref, None).astype(BF16)


def _attn_sample(qkv, cache_k, cache_v, bias):
    n, tq, _ = qkv.shape
    tc = cache_k.shape[1]
    cached = pl.BlockSpec((1, tc, D_ATTN), lambda i: (i, 0, 0))
    return pl.pallas_call(
        _attn_sample_body,
        grid=(n,),
        in_specs=[pl.BlockSpec((1, tq, 3 * D_ATTN), lambda i: (i, 0, 0)), cached, cached,
                  _const_spec((N_HEADS // 2, 2 * tq, tc + tq))],
        out_specs=pl.BlockSpec((1, tq, D_ATTN), lambda i: (i, 0, 0)),
        out_shape=jax.ShapeDtypeStruct((n, tq, D_ATTN), BF16),
        compiler_params=_params("arbitrary"),
        name="attn_sample",
    )(qkv, cache_k, cache_v, bias)


def _rel_bias(table, tq, tk):
    width = tq + tk - 1
    z = np.arange(width)
    idx = np.clip(ATTN_REACH - np.where(z < tk, z, z - width), -MAX_REL, MAX_REL) + MAX_REL
    ext = table.astype(F32)[:, idx]
    flat = jnp.tile(ext, (1, tq))[:, :tq * (width - 1)]
    bias = flat.reshape(N_HEADS, tq, width - 1)[:, :, :tk]
    return bias.reshape(N_HEADS // 2, 2 * tq, tk)


def _merge_blocks(ys_ref, att_ref, gate_ref, wglu, watt):
    tm = att_ref.shape[1]
    br_att = _dot(att_ref[0], watt[...])
    blocks = []
    for s in range(SUPER):
        frames = jnp.stack([ys_ref[s, :, j * LANES:(j + 1) * LANES].astype(F32) for j in range(S5_CHUNK)], axis=0)
        blocks.append(jnp.swapaxes(frames, 0, 1).reshape(tm, LANES))
    ys = jnp.concatenate(blocks, axis=-1).astype(BF16)

    for k in range(D_MODEL // MERGE_BLOCK):
        cols = slice(k * MERGE_BLOCK, (k + 1) * MERGE_BLOCK)
        gate_cols = slice(D_MODEL + k * MERGE_BLOCK, D_MODEL + (k + 1) * MERGE_BLOCK)
        br_ssm = _dot(ys, wglu[:, cols]) * _sigmoid(_dot(ys, wglu[:, gate_cols]))
        mix = (_sigmoid(gate_ref[0, :, cols].astype(F32)) * br_ssm
               + _sigmoid(gate_ref[0, :, gate_cols].astype(F32)) * br_att[:, cols])
        yield mix.astype(BF16)


def _merge_project(x_ref, blocks, wo):
    h = x_ref[0]
    blocks = iter(blocks)
    pending = next(blocks)
    for k in range(D_MODEL // MERGE_BLOCK):
        following = next(blocks, None)
        h = h + _dot(pending, wo[k * MERGE_BLOCK:(k + 1) * MERGE_BLOCK, :])
        pending = following
    return h


def _convglu_up(hn, wup):
    return [(_dot(hn, wup[:, lo:hi]), _dot(hn, wup[:, D_FF + lo:D_FF + hi])) for lo, hi in FF_CHUNKS]


def _convglu_down(ab, a_scr, cw, cb, wdown, history):
    tm = ab[0][0].shape[0]
    acc = jnp.zeros((tm, D_MODEL), F32)
    for (lo, hi), (a, b) in zip(FF_CHUNKS, ab):
        a_scr[8:8 + tm, 0:hi - lo] = a
        a1, a2 = history(a, lo, hi)
        c = cb[:, lo:hi] + cw[0:1, lo:hi] * a2 + cw[1:2, lo:hi] * a1 + cw[2:3, lo:hi] * a
        act = (_gelu(c) * b).astype(BF16)
        acc = acc + _dot(act, wdown[lo:hi, :])
    return acc


def _ffn_sample_body(x_ref, ys_ref, att_ref, gate_ref, h1_ref, h2_ref, wglu, watt, wo, gffn, wup, cw, cb, wdown,
                     gfin, y_ref, conv_ref, a_scr, *, seq_len):
    tm = x_ref.shape[1]
    h = _merge_project(x_ref, _merge_blocks(ys_ref, att_ref, gate_ref, wglu, watt), wo)
    hn = _rms(h, gffn[...]).astype(BF16)
    pos = lax.broadcasted_iota(jnp.int32, (tm, 1), 0) % seq_len

    def history(a, lo, hi):
        w = hi - lo
        a_scr[6:8, 0:w] = jnp.zeros((2, w), F32)
        conv_ref[0, :, lo:hi] = a
        a1 = jnp.where(pos < 1, h1_ref[:, lo:hi], a_scr[7:7 + tm, 0:w])
        a2 = jnp.where(pos < 2, h2_ref[:, lo:hi], a_scr[6:6 + tm, 0:w])
        return a1, a2

    down = _convglu_down(_convglu_up(hn, wup), a_scr, cw, cb, wdown, history)
    y_ref[0] = _rms(h + down, gfin[...])


def _ffn_prompt_body(x_ref, ys_ref, att_ref, gate_ref, wglu, watt, wo, gffn, wup, cw, cb, wdown, gfin,
                     y_ref, conv_ref, a_scr, carry):
    tm = x_ref.shape[1]

    @pl.when(pl.program_id(1) == 0)
    def _():
        carry[...] = jnp.zeros(carry.shape, F32)

    h = _merge_project(x_ref, _merge_blocks(ys_ref, att_ref, gate_ref, wglu, watt), wo)
    hn = _rms(h, gffn[...]).astype(BF16)

    def history(a, lo, hi):
        w = hi - lo
        a_scr[6:8, 0:w] = carry[0:CONV_W - 1, lo:hi]
        carry[0:CONV_W - 1, lo:hi] = a[tm - 2:tm, :]
        return a_scr[7:7 + tm, 0:w], a_scr[6:6 + tm, 0:w]

    down = _convglu_down(_convglu_up(hn, wup), a_scr, cw, cb, wdown, history)
    y_ref[0] = _rms(h + down, gfin[...])

    @pl.when(pl.program_id(1) == pl.num_programs(1) - 1)
    def _():
        conv_ref[0] = carry[0:CONV_W - 1, :]


def _ffn_weight_specs():
    return [_const_spec((D_SSM, 2 * D_MODEL)), _const_spec((D_ATTN, D_MODEL)), _const_spec((D_MODEL, D_MODEL)),
            _const_spec((1, D_MODEL)), _const_spec((D_MODEL, 2 * D_FF)), _const_spec((CONV_W, D_FF)),
            _const_spec((1, D_FF)), _const_spec((D_FF, D_MODEL)), _const_spec((1, D_MODEL))]


def _ffn_prompt(x, ys, att, gates, wts, tm):
    n, t, _ = x.shape
    tiles = t // tm
    tok = lambda w: pl.BlockSpec((1, tm, w), lambda i, j: (i, j, 0))
    ys_spec = pl.BlockSpec((SUPER, tm // S5_CHUNK, S5_ROW), lambda i, j: (0, i * tiles + j, 0))
    seq = pl.BlockSpec((1, CONV_W - 1, D_FF), lambda i, j: (i, 0, 0))
    fc = max(hi - lo for lo, hi in FF_CHUNKS)
    return pl.pallas_call(
        _ffn_prompt_body,
        grid=(n, tiles),
        in_specs=[tok(D_MODEL), ys_spec, tok(D_ATTN), tok(2 * D_MODEL)] + _ffn_weight_specs(),
        out_specs=[tok(D_MODEL), seq],
        out_shape=[jax.ShapeDtypeStruct((n, t, D_MODEL), F32),
                   jax.ShapeDtypeStruct((n, CONV_W - 1, D_FF), F32)],
        scratch_shapes=[pltpu.VMEM((tm + 16, fc), F32), pltpu.VMEM((16, D_FF), F32)],
        compiler_params=_params("arbitrary", "arbitrary"),
        name="merge_ffn_prompt",
    )(x, ys, att, gates, *wts)


def _ffn_sample(x, ys, att, gates, h1, h2, wts, seq_len):
    rows = x.shape[1]
    tok = lambda w: pl.BlockSpec((1, rows, w), lambda i: (0, 0, 0))
    ys_spec = pl.BlockSpec((SUPER, rows // S5_CHUNK, S5_ROW), lambda i: (0, 0, 0))
    full = pl.BlockSpec((rows, D_FF), lambda i: (0, 0))
    fc = max(hi - lo for lo, hi in FF_CHUNKS)
    return pl.pallas_call(
        functools.partial(_ffn_sample_body, seq_len=seq_len),
        grid=(1,),
        in_specs=[tok(D_MODEL), ys_spec, tok(D_ATTN), tok(2 * D_MODEL), full, full] + _ffn_weight_specs(),
        out_specs=[tok(D_MODEL), tok(D_FF)],
        out_shape=[jax.ShapeDtypeStruct((1, rows, D_MODEL), F32),
                   jax.ShapeDtypeStruct((1, rows, D_FF), F32)],
        scratch_shapes=[pltpu.VMEM((rows + 8, fc), F32)],
        compiler_params=_params("arbitrary"),
        name="merge_ffn_sample",
    )(x, ys, att, gates, h1, h2, *wts)


PROMPT_TOKEN_TILE = 1024
FFN_TOKEN_TILE = 512
SAMPLE_STREAM_PAD = 16


def kernel(x_prompt, x_sample, state_ssm_re, state_ssm_im, cache_attn_k, cache_attn_v, cache_conv,
           g_mix, w_in, ssm_lambda_re, ssm_lambda_im, ssm_log_dt, ssm_b_re, ssm_b_im,
           ssm_c_re, ssm_c_im, ssm_d, w_ssm_glu, attn_rel_bias, w_attn_up, w_o,
           g_ffn, w_up, conv_w, conv_b, w_down, g_final):
    depth = w_in.shape[0]
    nb, seq, _ = x_prompt.shape
    db, dseq, _ = x_sample.shape
    kept = min(ATTN_REACH, seq)
    assert depth == 1 and seq % PROMPT_TOKEN_TILE == 0 and seq % FFN_TOKEN_TILE == 0
    assert seq % S5_CHUNK == 0 and dseq == S5_CHUNK and nb % S5_PROMPT_SEQS == 0 and db <= SAMPLE_STREAM_PAD
    l = 0

    log2e = math.log2(math.e)
    col_scale = jnp.ones((D_IN,), F32).at[D_SSM:D_SSM + D_ATTN].set(HEAD_DIM ** -0.5 * log2e)
    w_in_b = (w_in[l] * col_scale[None, :]).astype(BF16)
    later_weights = (w_ssm_glu[l], w_attn_up[l], w_o[l], w_up[l], w_down[l])
    g_mix_l = g_mix[l][None, :]
    w_s5 = _s5_weights(ssm_lambda_re[l], ssm_lambda_im[l], ssm_log_dt[l], ssm_b_re[l], ssm_b_im[l],
                       ssm_c_re[l], ssm_c_im[l], ssm_d[l])
    table = attn_rel_bias[l].astype(F32) * log2e
    bias_p = _rel_bias(table, CHUNK, BAND)
    w_cache = cache_attn_k.shape[2]
    assert w_cache == ATTN_REACH
    bias_s = _rel_bias(table, dseq, w_cache + dseq)

    u, qkv, gates, k32, v32 = _inproj(x_prompt, g_mix_l, w_in_b, PROMPT_TOKEN_TILE, kept)
    n_chunks = seq // S5_CHUNK
    tiles = nb // S5_PROMPT_SEQS
    zero_state = jnp.zeros((SUPER, tiles, S5_PROMPT_SEQS, SUPER_STATE), F32)
    ys, p_state = _s5(u, w_s5, zero_state, S5_PROMPT_SEQS, n_chunks)
    p_re, p_im = _state_from_super(p_state, nb)
    att, (w_glu_b, w_att_b, w_o_b, w_up_b, w_down_b) = _attn_prompt(qkv, bias_p, later_weights)
    wts = (w_glu_b, w_att_b, w_o_b, g_ffn[l][None, :], w_up_b, conv_w[l], conv_b[l][None, :], w_down_b,
           g_final[None, :])
    y_prompt, p_conv = _ffn_prompt(x_prompt, ys, att, gates, wts, FFN_TOKEN_TILE)

    rows = db * dseq
    xs = x_sample.reshape(1, rows, D_MODEL)
    us, qkvs, gates_s, ks32, vs32 = _inproj(xs, g_mix_l, w_in_b, rows, rows)
    us2 = jnp.pad(us, ((0, 0), (0, SAMPLE_STREAM_PAD - db), (0, 0)))
    s0 = _state_to_super(state_ssm_re[l], state_ssm_im[l], 1, SAMPLE_STREAM_PAD)
    yss, s_state = _s5(us2, w_s5, s0, SAMPLE_STREAM_PAD, 1)
    s_re, s_im = _state_from_super(s_state, db)
    yss = yss[:, :db]
    att_s = _attn_sample(qkvs.reshape(db, dseq, 3 * D_ATTN), cache_attn_k[l].reshape(db, w_cache, D_ATTN),
                         cache_attn_v[l].reshape(db, w_cache, D_ATTN), bias_s)
    hist = cache_conv[l].astype(F32)
    pad_rows = ((0, 0), (0, dseq - 1), (0, 0))
    h1 = jnp.pad(hist[:, 1:2], pad_rows).reshape(rows, D_FF)
    h2 = jnp.pad(hist, ((0, 0), (0, dseq - 2), (0, 0))).reshape(rows, D_FF)
    y_s, a_s = _ffn_sample(xs, yss, att_s.reshape(1, rows, D_ATTN), gates_s, h1, h2, wts, dseq)
    s_conv = a_s.reshape(db, dseq, D_FF)[:, dseq - (CONV_W - 1):]

    heads = lambda a, n, t: a.reshape(1, n, t, N_HEADS, HEAD_DIM)
    return (y_prompt, y_s.reshape(db, dseq, D_MODEL),
            p_re[None], p_im[None],
            heads(k32, nb, kept), heads(v32, nb, kept), p_conv[None],
            s_re[None], s_im[None],
            heads(ks32, db, dseq), heads(vs32, db, dseq), s_conv[None])
```

```python
import functools
import math

import jax
import jax.numpy as jnp
import numpy as np
from jax import lax
from jax.experimental import pallas as pl
from jax.experimental.pallas import tpu as pltpu

F32 = jnp.float32
BF16 = jnp.bfloat16

D_MODEL = 1024
CHUNK = 64
LEFT_CHUNKS = 8
ATTN_REACH = LEFT_CHUNKS * CHUNK
BAND = ATTN_REACH + CHUNK
D_SSM = D_MODEL // 2
D_ATTN = D_MODEL // 2
HEAD_DIM = 64
N_HEADS = D_ATTN // HEAD_DIM
MAX_REL = 128
SSM_GROUP = 16
N_GROUPS = D_SSM // SSM_GROUP
SSM_STATE = 64
D_FF = 2816
CONV_W = 3
D_IN = D_SSM + 3 * D_ATTN + 2 * D_MODEL
RMS_EPS = 1e-6
MASK_VALUE = -1e30

V7X_VMEM_LIMIT_BYTES = 56 * 1024 * 1024
LANES = 128
BF16_SUBLANES = 16
MERGE_BLOCK = 256
FF_CHUNKS = ((0, 2816),)


def _params(*sem):
    return pltpu.CompilerParams(dimension_semantics=sem, vmem_limit_bytes=V7X_VMEM_LIMIT_BYTES)


def _const_spec(shape):
    zeros = (0,) * len(shape)
    return pl.BlockSpec(shape, lambda *_: zeros, pipeline_mode=pl.Buffered(1))


def _rms(x, g):
    return x * lax.rsqrt(jnp.mean(x * x, axis=-1, keepdims=True) + RMS_EPS) * g


def _gelu(x):
    k = math.sqrt(2.0 / math.pi)
    return (0.5 * x) * (1.0 + jnp.tanh(x * (k + (k * 0.044715) * (x * x))))


def _sigmoid(x):
    return 1.0 / (1.0 + jnp.exp(-x))


def _dot(a, b):
    return jnp.dot(a, b, preferred_element_type=F32)


S5_CHUNK = 16
SUPER = D_SSM // LANES
SUPER_GROUPS = N_GROUPS // SUPER
SUPER_STATE = 2 * SUPER_GROUPS * SSM_STATE
S5_ROW = S5_CHUNK * LANES
S5_PROMPT_SEQS = 8


def _inproj_body(x_ref, g_ref, w_ref, u_ref, qkv_ref, gate_ref, k32_ref, v32_ref):
    xn = _rms(x_ref[0], g_ref[...]).astype(BF16)

    def seg(lo, hi):
        return _dot(xn, w_ref[:, lo:hi])

    u = seg(0, D_SSM)
    n_rows = u.shape[0] // S5_CHUNK
    for s in range(SUPER):
        frames = u[:, s * LANES:(s + 1) * LANES].reshape(n_rows, S5_CHUNK, LANES)
        frames = jnp.swapaxes(frames, 0, 1)
        for j in range(S5_CHUNK):
            u_ref[s, :, j * LANES:(j + 1) * LANES] = frames[j].astype(BF16)
    q0 = D_SSM
    qkv = seg(q0, q0 + 3 * D_ATTN)
    k = qkv[:, D_ATTN:2 * D_ATTN]
    v = qkv[:, 2 * D_ATTN:3 * D_ATTN]
    qkv_ref[0] = qkv.astype(BF16)
    g0 = q0 + 3 * D_ATTN
    gate_ref[0] = seg(g0, g0 + 2 * D_MODEL).astype(BF16)

    @pl.when(pl.program_id(1) == pl.num_programs(1) - 1)
    def _():
        tm, kept = k.shape[0], k32_ref.shape[1]
        k32_ref[0] = k[tm - kept:, :].reshape(kept, N_HEADS, HEAD_DIM)
        v32_ref[0] = v[tm - kept:, :].reshape(kept, N_HEADS, HEAD_DIM)


def _inproj(x, g_mix, w_in_b, tm, kept):
    n, t, _ = x.shape
    assert kept <= tm
    tiles = t // tm
    n_rows = tm // S5_CHUNK
    tok = lambda w: pl.BlockSpec((1, tm, w), lambda i, j: (i, j, 0))
    keep = pl.BlockSpec((1, kept, N_HEADS, HEAD_DIM), lambda i, j: (i, 0, 0, 0))
    u_spec = pl.BlockSpec((SUPER, n_rows, S5_ROW), lambda i, j: (0, i * tiles + j, 0))
    return pl.pallas_call(
        _inproj_body,
        grid=(n, tiles),
        in_specs=[tok(D_MODEL), _const_spec((1, D_MODEL)), _const_spec((D_MODEL, D_IN))],
        out_specs=[u_spec, tok(3 * D_ATTN), tok(2 * D_MODEL), keep, keep],
        out_shape=[
            jax.ShapeDtypeStruct((SUPER, n * tiles * n_rows, S5_ROW), BF16),
            jax.ShapeDtypeStruct((n, t, 3 * D_ATTN), BF16),
            jax.ShapeDtypeStruct((n, t, 2 * D_MODEL), BF16),
            jax.ShapeDtypeStruct((n, kept, N_HEADS, HEAD_DIM), F32),
            jax.ShapeDtypeStruct((n, kept, N_HEADS, HEAD_DIM), F32),
        ],
        compiler_params=_params("arbitrary", "arbitrary"),
        name="inproj",
    )(x, g_mix, w_in_b)


def _s5_weights(lam_re, lam_im, log_dt, b_re, b_im, c_re, c_im, d_skip):
    L = S5_CHUNK
    hi = lax.Precision.HIGHEST
    f = lambda a: a.astype(F32)
    cmul = lambda ar, ai, br, bi: (ar * br - ai * bi, ar * bi + ai * br)
    a_re, a_im = f(lam_re), f(lam_im)
    dt = jnp.exp(f(log_dt))[:, None]
    tau = jnp.arange(L + 1, dtype=F32)[None, :, None]
    mag = jnp.exp((a_re * dt)[:, None, :] * tau)
    ang = (a_im * dt)[:, None, :] * tau
    pw_re, pw_im = mag * jnp.cos(ang), mag * jnp.sin(ang)
    n_re, n_im = pw_re[:, 1] - 1.0, pw_im[:, 1]
    den = a_re * a_re + a_im * a_im
    f_re, f_im = (n_re * a_re + n_im * a_im) / den, (n_im * a_re - n_re * a_im) / den
    bb_re, bb_im = cmul(f_re[..., None], f_im[..., None], f(b_re), f(b_im))
    cp_re, cp_im = cmul(f(c_re)[:, :, None, :], f(c_im)[:, :, None, :], pw_re[:, None], pw_im[:, None])
    kt = (jnp.einsum('gatp,gph->ghta', cp_re[:, :, :L], bb_re, precision=hi)
          - jnp.einsum('gatp,gph->ghta', cp_im[:, :, :L], bb_im, precision=hi))
    kt = kt.at[:, :, 0, :].add(f(d_skip)[:, :, None] * jnp.eye(SSM_GROUP, dtype=F32)[None])
    rev = L - 1 - np.arange(L)
    bt_re, bt_im = jnp.swapaxes(bb_re, 1, 2)[:, None], jnp.swapaxes(bb_im, 1, 2)[:, None]
    wst_re, wst_im = cmul(pw_re[:, rev][:, :, None, :], pw_im[:, rev][:, :, None, :], bt_re, bt_im)

    sg = lambda a: a.reshape((SUPER, SUPER_GROUPS) + a.shape[1:])
    kc = jnp.transpose(sg(kt), (0, 2, 3, 1, 4)).reshape(SUPER, SSM_GROUP, S5_ROW)
    even = (np.arange(SUPER_GROUPS) % 2 == 0)[None, None, :, None, None]

    def in_form(w):
        w = jnp.transpose(sg(w), (0, 2, 1, 3, 4))
        w = jnp.concatenate([jnp.where(even, w, 0.0), jnp.where(even, 0.0, w)], axis=-1)
        return w.reshape(SUPER, S5_ROW, 2 * SSM_STATE)

    def out_form(c):
        return jnp.transpose(sg(c), (0, 4, 3, 1, 2)).reshape(SUPER, SSM_STATE, S5_ROW)

    wc = jnp.stack([in_form(wst_re), in_form(wst_im)], axis=1)
    ic = jnp.stack([out_form(cp_re[:, :, 1:L + 1]), out_form(-cp_im[:, :, 1:L + 1])], axis=1)
    half = SUPER_STATE // 2
    lam = jnp.concatenate([pw_re[:, L].reshape(SUPER, half), pw_im[:, L].reshape(SUPER, half)], axis=-1)
    return kc.astype(BF16), wc.astype(BF16), ic.astype(BF16), lam[:, None, :]


def _s5_body(u_ref, kc_ref, wc_ref, ic_ref, lam_ref, s0_ref, y_ref, sfin_ref, toep, wst, wint, x_scr, prev_scr,
             *, ns, nc):
    half = SUPER_STATE // 2

    @pl.when(pl.program_id(1) == 0)
    def _():
        zero = jnp.zeros((), BF16)
        group_of = lambda shape, dim: (lax.broadcasted_iota(jnp.int32, shape, dim) % LANES) // SSM_GROUP
        col_group = group_of((SSM_GROUP, S5_ROW), 1)
        kc = kc_ref[0]
        bblk = jnp.concatenate([jnp.where(col_group == g, kc, zero) for g in range(SUPER_GROUPS)], axis=0)
        for j in range(S5_CHUNK):
            rows = slice(j * LANES, (j + 1) * LANES)
            if j:
                toep[rows, 0:j * LANES] = jnp.zeros((LANES, j * LANES), BF16)
            toep[rows, j * LANES:S5_ROW] = bblk[:, 0:S5_ROW - j * LANES]
        row_pair = group_of((S5_ROW, LANES), 0) // 2
        for part in range(2):
            w = wc_ref[0, part]
            for k in range(SUPER_GROUPS // 2):
                c0 = part * half + k * LANES
                wst[:, c0:c0 + LANES] = jnp.where(row_pair == k, w, zero)
        col_group = group_of((SSM_STATE, S5_ROW), 1)
        for part in range(2):
            c = ic_ref[0, part]
            for g in range(SUPER_GROUPS):
                r0 = part * half + g * SSM_STATE
                wint[r0:r0 + SSM_STATE, :] = jnp.where(col_group == g, c, zero)

    u = u_ref[0]
    x = _dot(u, wst[...])
    regroup = nc > 1
    if regroup:
        x = jnp.swapaxes(x.reshape(ns, nc, SUPER_STATE), 0, 1).reshape(ns * nc, SUPER_STATE)
    nblk = SUPER_STATE // LANES
    hblk = nblk // 2
    for k in range(nblk):
        x_scr[k] = x[:, k * LANES:(k + 1) * LANES]
    lam = [lam_ref[0, :, k * LANES:(k + 1) * LANES] for k in range(nblk)]

    def step(c, s):
        rows = pl.ds(pl.multiple_of(c * ns, ns), ns)
        nxt = []
        for k in range(nblk):
            prev_scr[k, rows, :] = s[k]
        for k in range(hblk):
            l_re, l_im, s_re, s_im = lam[k], lam[hblk + k], s[k], s[hblk + k]
            nxt.append((l_re * s_re - l_im * s_im + x_scr[k, rows, :],
                        l_re * s_im + l_im * s_re + x_scr[hblk + k, rows, :]))
        return tuple(p[0] for p in nxt) + tuple(p[1] for p in nxt)

    s0 = s0_ref[0, 0]
    s_fin = lax.fori_loop(0, nc, step, tuple(s0[:, k * LANES:(k + 1) * LANES] for k in range(nblk)),
                          unroll=min(nc, 4))
    for k in range(nblk):
        sfin_ref[0, 0, :, k * LANES:(k + 1) * LANES] = s_fin[k]

    prev = jnp.concatenate([prev_scr[k] for k in range(nblk)], axis=-1)
    if regroup:
        prev = jnp.swapaxes(prev.reshape(nc, ns, SUPER_STATE), 0, 1).reshape(ns * nc, SUPER_STATE)
    prev = prev.astype(BF16)
    blk = 2 * LANES
    for tb in range(S5_ROW // blk):
        cols = slice(tb * blk, (tb + 1) * blk)
        depth = (tb + 1) * blk
        y = _dot(u[:, 0:depth], toep[0:depth, cols]) + _dot(prev, wint[:, cols])
        y_ref[0, :, cols] = _gelu(y).astype(BF16)


def _s5(u2, weights, s0, ns, nc):
    kc, wc, ic, lam = weights
    tiles = s0.shape[1]
    r = ns * nc
    per_sg = lambda *dims: pl.BlockSpec((1,) + dims, lambda s, i: (s,) + (0,) * len(dims))
    rows = pl.BlockSpec((1, r, S5_ROW), lambda s, i: (s, i, 0))
    state = pl.BlockSpec((1, 1, ns, SUPER_STATE), lambda s, i: (s, i, 0, 0))
    return pl.pallas_call(
        functools.partial(_s5_body, ns=ns, nc=nc),
        grid=(SUPER, tiles),
        in_specs=[rows, per_sg(SSM_GROUP, S5_ROW), per_sg(2, S5_ROW, LANES), per_sg(2, SSM_STATE, S5_ROW),
                  per_sg(1, SUPER_STATE), state],
        out_specs=[rows, state],
        out_shape=[jax.ShapeDtypeStruct((SUPER, tiles * r, S5_ROW), BF16),
                   jax.ShapeDtypeStruct((SUPER, tiles, ns, SUPER_STATE), F32)],
        scratch_shapes=[pltpu.VMEM((S5_ROW, S5_ROW), BF16), pltpu.VMEM((S5_ROW, SUPER_STATE), BF16),
                        pltpu.VMEM((SUPER_STATE, S5_ROW), BF16),
                        pltpu.VMEM((SUPER_STATE // LANES, r, LANES), F32),
                        pltpu.VMEM((SUPER_STATE // LANES, r, LANES), F32)],
        compiler_params=_params("arbitrary", "arbitrary"),
        name="s5",
    )(u2, kc, wc, ic, lam, s0)


def _state_to_super(s_re, s_im, tiles, ns):
    n = s_re.shape[0]
    half = SUPER_STATE // 2
    s = jnp.concatenate([s_re.astype(F32).reshape(n, SUPER, half), s_im.astype(F32).reshape(n, SUPER, half)], -1)
    s = jnp.pad(jnp.transpose(s, (1, 0, 2)), ((0, 0), (0, tiles * ns - n), (0, 0)))
    return s.reshape(SUPER, tiles, ns, SUPER_STATE)


def _state_from_super(s, n):
    half = SUPER_STATE // 2
    s = jnp.transpose(s.reshape(SUPER, -1, SUPER_STATE)[:, :n], (1, 0, 2))
    return (s[:, :, :half].reshape(n, N_GROUPS, SSM_STATE), s[:, :, half:].reshape(n, N_GROUPS, SSM_STATE))


def _attend(qc, kb, vb, bias_ref, valid, skip=0):
    tq = qc.shape[0]
    left = lax.broadcasted_iota(jnp.int32, (tq, LANES), 1) < HEAD_DIM
    zero = jnp.zeros((), BF16)
    pairs = range(N_HEADS // 2)
    cols = [slice(hp * LANES, (hp + 1) * LANES) for hp in pairs]
    scores = []
    for hp in pairs:
        qp = qc[:, cols[hp]]
        q2 = jnp.concatenate([jnp.where(left, qp, zero), jnp.where(left, zero, qp)], axis=0)
        sc = lax.dot_general(q2, kb[:, cols[hp]], (((1,), (1,)), ((), ())), preferred_element_type=F32)
        sc = sc + bias_ref[hp, :, skip:]
        if valid is not None:
            sc = jnp.where(valid, sc, MASK_VALUE)
        scores.append(sc)
    probs, dens = [], []
    for sc in scores:
        e = jnp.exp2(sc - jnp.max(sc, axis=-1, keepdims=True))
        dens.append(jnp.sum(e, axis=-1, keepdims=True))
        probs.append(e.astype(BF16))
    outs = []
    for hp in pairs:
        o2 = _dot(probs[hp], vb[:, cols[hp]]) / dens[hp]
        outs.append(jnp.where(left, o2[:tq], o2[tq:]))
    return jnp.concatenate(outs, axis=-1)


def _attn_prompt_body(q_ref, k_ref, v_ref, bias_ref, *rest, n_cast):
    w32_refs, o_ref, w16_refs, (kpad, vpad) = rest[:n_cast], rest[n_cast], rest[n_cast + 1:2 * n_cast + 1], rest[-2:]
    for w32, w16 in zip(w32_refs, w16_refs):
        w16[...] = w32[...].astype(BF16)

    t = q_ref.shape[1]
    kpad[0:CHUNK, :] = jnp.zeros((CHUNK, D_ATTN), BF16)
    vpad[0:CHUNK, :] = jnp.zeros((CHUNK, D_ATTN), BF16)
    kpad[CHUNK:BAND, :] = k_ref[0, 0:ATTN_REACH, :]
    vpad[CHUNK:BAND, :] = v_ref[0, 0:ATTN_REACH, :]

    def chunk(c, skip, masked, head):
        if head:
            r0 = c * CHUNK
            rows = pl.ds(r0 + skip - ATTN_REACH + CHUNK, BAND - skip)
            kb, vb = kpad[rows, :], vpad[rows, :]
        else:
            r0 = c * CHUNK
            if not isinstance(r0, int):
                r0 = pl.multiple_of(r0, CHUNK)
            rows = pl.ds(r0 - ATTN_REACH, BAND)
            kb, vb = k_ref[0, rows, :], v_ref[0, rows, :]
        qc = q_ref[0, pl.ds(r0, CHUNK), :]
        valid = None
        if masked:
            col = lax.broadcasted_iota(jnp.int32, (2 * CHUNK, BAND - skip), 1) + skip
            valid = col >= ATTN_REACH - r0
        o_ref[0, pl.ds(r0, CHUNK), :] = _attend(qc, kb, vb, bias_ref, valid, skip).astype(BF16)

    for c in range(LEFT_CHUNKS):
        seen = (c + 1) * CHUNK
        skip = (BAND - seen) // LANES * LANES
        chunk(c, skip, BAND - skip != seen, True)
    lax.fori_loop(LEFT_CHUNKS, t // CHUNK, lambda c, _: chunk(c, 0, False, False), None, unroll=6)


def _attn_prompt(qkv, bias, weights32):
    n, t, _ = qkv.shape
    assert t >= ATTN_REACH and t % CHUNK == 0
    assert all(w.shape[0] % (n * BF16_SUBLANES) == 0 for w in weights32)
    part = lambda j: pl.BlockSpec((1, t, D_ATTN), lambda i: (i, 0, j))
    slabs = [pl.BlockSpec((w.shape[0] // n, w.shape[1]), lambda i: (i, 0)) for w in weights32]
    out = pl.pallas_call(
        functools.partial(_attn_prompt_body, n_cast=len(weights32)),
        grid=(n,),
        in_specs=[part(0), part(1), part(2), _const_spec((N_HEADS // 2, 2 * CHUNK, BAND))] + slabs,
        out_specs=[pl.BlockSpec((1, t, D_ATTN), lambda i: (i, 0, 0))] + slabs,
        out_shape=[jax.ShapeDtypeStruct((n, t, D_ATTN), BF16)]
        + [jax.ShapeDtypeStruct(w.shape, BF16) for w in weights32],
        scratch_shapes=[pltpu.VMEM((BAND, D_ATTN), BF16), pltpu.VMEM((BAND, D_ATTN), BF16)],
        compiler_params=_params("arbitrary"),
        name="attn_prompt",
    )(qkv, qkv, qkv, bias, *weights32)
    return out[0], out[1:]


def _attn_sample_body(qkv_ref, ck_ref, cv_ref, bias_ref, o_ref):
    q = qkv_ref[0, :, 0:D_ATTN]
    kk = jnp.concatenate([ck_ref[0].astype(BF16), qkv_ref[0, :, D_ATTN:2 * D_ATTN]], axis=0)
    vv = jnp.concatenate([cv_ref[0].astype(BF16), qkv_ref[0, :, 2 * D_ATTN:3 * D_ATTN]], axis=0)
    o_ref[0] = _attend(q, kk, vv, bias_ref, None).astype(BF16)


def _attn_sample(qkv, cache_k, cache_v, bias):
    n, tq, _ = qkv.shape
    tc = cache_k.shape[1]
    cached = pl.BlockSpec((1, tc, D_ATTN), lambda i: (i, 0, 0))
    return pl.pallas_call(
        _attn_sample_body,
        grid=(n,),
        in_specs=[pl.BlockSpec((1, tq, 3 * D_ATTN), lambda i: (i, 0, 0)), cached, cached,
                  _const_spec((N_HEADS // 2, 2 * tq, tc + tq))],
        out_specs=pl.BlockSpec((1, tq, D_ATTN), lambda i: (i, 0, 0)),
        out_shape=jax.ShapeDtypeStruct((n, tq, D_ATTN), BF16),
        compiler_params=_params("arbitrary"),
        name="attn_sample",
    )(qkv, cache_k, cache_v, bias)


def _rel_bias(table, tq, tk):
    width = tq + tk - 1
    z = np.arange(width)
    idx = np.clip(ATTN_REACH - np.where(z < tk, z, z - width), -MAX_REL, MAX_REL) + MAX_REL
    ext = table.astype(F32)[:, idx]
    flat = jnp.tile(ext, (1, tq))[:, :tq * (width - 1)]
    bias = flat.reshape(N_HEADS, tq, width - 1)[:, :, :tk]
    return bias.reshape(N_HEADS // 2, 2 * tq, tk)


def _merge_blocks(ys_ref, att_ref, gate_ref, wglu, watt):
    tm = att_ref.shape[1]
    br_att = _dot(att_ref[0], watt[...])
    blocks = []
    for s in range(SUPER):
        frames = jnp.stack([ys_ref[s, :, j * LANES:(j + 1) * LANES].astype(F32) for j in range(S5_CHUNK)], axis=0)
        blocks.append(jnp.swapaxes(frames, 0, 1).reshape(tm, LANES))
    ys = jnp.concatenate(blocks, axis=-1).astype(BF16)

    for k in range(D_MODEL // MERGE_BLOCK):
        cols = slice(k * MERGE_BLOCK, (k + 1) * MERGE_BLOCK)
        gate_cols = slice(D_MODEL + k * MERGE_BLOCK, D_MODEL + (k + 1) * MERGE_BLOCK)
        br_ssm = _dot(ys, wglu[:, cols]) * _sigmoid(_dot(ys, wglu[:, gate_cols]))
        mix = (_sigmoid(gate_ref[0, :, cols].astype(F32)) * br_ssm
               + _sigmoid(gate_ref[0, :, gate_cols].astype(F32)) * br_att[:, cols])
        yield mix.astype(BF16)


def _merge_project(x_ref, blocks, wo):
    h = x_ref[0]
    blocks = iter(blocks)
    pending = next(blocks)
    for k in range(D_MODEL // MERGE_BLOCK):
        following = next(blocks, None)
        h = h + _dot(pending, wo[k * MERGE_BLOCK:(k + 1) * MERGE_BLOCK, :])
        pending = following
    return h


def _convglu_up(hn, wup):
    return [(_dot(hn, wup[:, lo:hi]), _dot(hn, wup[:, D_FF + lo:D_FF + hi])) for lo, hi in FF_CHUNKS]


def _convglu_down(ab, a_scr, cw, cb, wdown, history):
    tm = ab[0][0].shape[0]
    acc = jnp.zeros((tm, D_MODEL), F32)
    for (lo, hi), (a, b) in zip(FF_CHUNKS, ab):
        a_scr[8:8 + tm, 0:hi - lo] = a
        a1, a2 = history(a, lo, hi)
        c = cb[:, lo:hi] + cw[0:1, lo:hi] * a2 + cw[1:2, lo:hi] * a1 + cw[2:3, lo:hi] * a
        act = (_gelu(c) * b).astype(BF16)
        acc = acc + _dot(act, wdown[lo:hi, :])
    return acc


def _ffn_sample_body(x_ref, ys_ref, att_ref, gate_ref, h1_ref, h2_ref, wglu, watt, wo, gffn, wup, cw, cb, wdown,
                     gfin, y_ref, conv_ref, a_scr, *, seq_len):
    tm = x_ref.shape[1]
    h = _merge_project(x_ref, _merge_blocks(ys_ref, att_ref, gate_ref, wglu, watt), wo)
    hn = _rms(h, gffn[...]).astype(BF16)
    pos = lax.broadcasted_iota(jnp.int32, (tm, 1), 0) % seq_len

    def history(a, lo, hi):
        w = hi - lo
        a_scr[6:8, 0:w] = jnp.zeros((2, w), F32)
        conv_ref[0, :, lo:hi] = a
        a1 = jnp.where(pos < 1, h1_ref[:, lo:hi], a_scr[7:7 + tm, 0:w])
        a2 = jnp.where(pos < 2, h2_ref[:, lo:hi], a_scr[6:6 + tm, 0:w])
        return a1, a2

    down = _convglu_down(_convglu_up(hn, wup), a_scr, cw, cb, wdown, history)
    y_ref[0] = _rms(h + down, gfin[...])


def _ffn_prompt_body(x_ref, ys_ref, att_ref, gate_ref, wglu, watt, wo, gffn, wup, cw, cb, wdown, gfin,
                     y_ref, conv_ref, a_scr, carry):
    tm = x_ref.shape[1]

    @pl.when(pl.program_id(1) == 0)
    def _():
        carry[...] = jnp.zeros(carry.shape, F32)

    h = _merge_project(x_ref, _merge_blocks(ys_ref, att_ref, gate_ref, wglu, watt), wo)
    hn = _rms(h, gffn[...]).astype(BF16)

    def history(a, lo, hi):
        w = hi - lo
        a_scr[6:8, 0:w] = carry[0:CONV_W - 1, lo:hi]
        carry[0:CONV_W - 1, lo:hi] = a[tm - 2:tm, :]
        return a_scr[7:7 + tm, 0:w], a_scr[6:6 + tm, 0:w]

    down = _convglu_down(_convglu_up(hn, wup), a_scr, cw, cb, wdown, history)
    y_ref[0] = _rms(h + down, gfin[...])

    @pl.when(pl.program_id(1) == pl.num_programs(1) - 1)
    def _():
        conv_ref[0] = carry[0:CONV_W - 1, :]


def _ffn_weight_specs():
    return [_const_spec((D_SSM, 2 * D_MODEL)), _const_spec((D_ATTN, D_MODEL)), _const_spec((D_MODEL, D_MODEL)),
            _const_spec((1, D_MODEL)), _const_spec((D_MODEL, 2 * D_FF)), _const_spec((CONV_W, D_FF)),
            _const_spec((1, D_FF)), _const_spec((D_FF, D_MODEL)), _const_spec((1, D_MODEL))]


def _ffn_prompt(x, ys, att, gates, wts, tm):
    n, t, _ = x.shape
    tiles = t // tm
    tok = lambda w: pl.BlockSpec((1, tm, w), lambda i, j: (i, j, 0))
    ys_spec = pl.BlockSpec((SUPER, tm // S5_CHUNK, S5_ROW), lambda i, j: (0, i * tiles + j, 0))
    seq = pl.BlockSpec((1, CONV_W - 1, D_FF), lambda i, j: (i, 0, 0))
    fc = max(hi - lo for lo, hi in FF_CHUNKS)
    return pl.pallas_call(
        _ffn_prompt_body,
        grid=(n, tiles),
        in_specs=[tok(D_MODEL), ys_spec, tok(D_ATTN), tok(2 * D_MODEL)] + _ffn_weight_specs(),
        out_specs=[tok(D_MODEL), seq],
        out_shape=[jax.ShapeDtypeStruct((n, t, D_MODEL), F32),
                   jax.ShapeDtypeStruct((n, CONV_W - 1, D_FF), F32)],
        scratch_shapes=[pltpu.VMEM((tm + 16, fc), F32), pltpu.VMEM((16, D_FF), F32)],
        compiler_params=_params("arbitrary", "arbitrary"),
        name="merge_ffn_prompt",
    )(x, ys, att, gates, *wts)


def _ffn_sample(x, ys, att, gates, h1, h2, wts, seq_len):
    rows = x.shape[1]
    tok = lambda w: pl.BlockSpec((1, rows, w), lambda i: (0, 0, 0))
    ys_spec = pl.BlockSpec((SUPER, rows // S5_CHUNK, S5_ROW), lambda i: (0, 0, 0))
    full = pl.BlockSpec((rows, D_FF), lambda i: (0, 0))
    fc = max(hi - lo for lo, hi in FF_CHUNKS)
    return pl.pallas_call(
        functools.partial(_ffn_sample_body, seq_len=seq_len),
        grid=(1,),
        in_specs=[tok(D_MODEL), ys_spec, tok(D_ATTN), tok(2 * D_MODEL), full, full] + _ffn_weight_specs(),
        out_specs=[tok(D_MODEL), tok(D_FF)],
        out_shape=[jax.ShapeDtypeStruct((1, rows, D_MODEL), F32),
                   jax.ShapeDtypeStruct((1, rows, D_FF), F32)],
        scratch_shapes=[pltpu.VMEM((rows + 8, fc), F32)],
        compiler_params=_params("arbitrary"),
        name="merge_ffn_sample",
    )(x, ys, att, gates, h1, h2, *wts)


PROMPT_TOKEN_TILE = 1024
FFN_TOKEN_TILE = 512
SAMPLE_STREAM_PAD = 16


def kernel(x_prompt, x_sample, state_ssm_re, state_ssm_im, cache_attn_k, cache_attn_v, cache_conv,
           g_mix, w_in, ssm_lambda_re, ssm_lambda_im, ssm_log_dt, ssm_b_re, ssm_b_im,
           ssm_c_re, ssm_c_im, ssm_d, w_ssm_glu, attn_rel_bias, w_attn_up, w_o,
           g_ffn, w_up, conv_w, conv_b, w_down, g_final):
    depth = w_in.shape[0]
    nb, seq, _ = x_prompt.shape
    db, dseq, _ = x_sample.shape
    kept = min(ATTN_REACH, seq)
    assert depth == 1 and seq % PROMPT_TOKEN_TILE == 0 and seq % FFN_TOKEN_TILE == 0
    assert seq % S5_CHUNK == 0 and dseq == S5_CHUNK and nb % S5_PROMPT_SEQS == 0 and db <= SAMPLE_STREAM_PAD
    l = 0

    log2e = math.log2(math.e)
    col_scale = jnp.ones((D_IN,), F32).at[D_SSM:D_SSM + D_ATTN].set(HEAD_DIM ** -0.5 * log2e)
    w_in_b = (w_in[l] * col_scale[None, :]).astype(BF16)
    later_weights = (w_ssm_glu[l], w_attn_up[l], w_o[l], w_up[l], w_down[l])
    g_mix_l = g_mix[l][None, :]
    w_s5 = _s5_weights(ssm_lambda_re[l], ssm_lambda_im[l], ssm_log_dt[l], ssm_b_re[l], ssm_b_im[l],
                       ssm_c_re[l], ssm_c_im[l], ssm_d[l])
    table = attn_rel_bias[l].astype(F32) * log2e
    bias_p = _rel_bias(table, CHUNK, BAND)
    w_cache = cache_attn_k.shape[2]
    assert w_cache == ATTN_REACH
    bias_s = _rel_bias(table, dseq, w_cache + dseq)

    u, qkv, gates, k32, v32 = _inproj(x_prompt, g_mix_l, w_in_b, PROMPT_TOKEN_TILE, kept)
    n_chunks = seq // S5_CHUNK
    tiles = nb // S5_PROMPT_SEQS
    zero_state = jnp.zeros((SUPER, tiles, S5_PROMPT_SEQS, SUPER_STATE), F32)
    ys, p_state = _s5(u, w_s5, zero_state, S5_PROMPT_SEQS, n_chunks)
    p_re, p_im = _state_from_super(p_state, nb)
    att, (w_glu_b, w_att_b, w_o_b, w_up_b, w_down_b) = _attn_prompt(qkv, bias_p, later_weights)
    wts = (w_glu_b, w_att_b, w_o_b, g_ffn[l][None, :], w_up_b, conv_w[l], conv_b[l][None, :], w_down_b,
           g_final[None, :])
    y_prompt, p_conv = _ffn_prompt(x_prompt, ys, att, gates, wts, FFN_TOKEN_TILE)

    rows = db * dseq
    xs = x_sample.reshape(1, rows, D_MODEL)
    us, qkvs, gates_s, ks32, vs32 = _inproj(xs, g_mix_l, w_in_b, rows, rows)
    us2 = jnp.pad(us, ((0, 0), (0, SAMPLE_STREAM_PAD - db), (0, 0)))
    s0 = _state_to_super(state_ssm_re[l], state_ssm_im[l], 1, SAMPLE_STREAM_PAD)
    yss, s_state = _s5(us2, w_s5, s0, SAMPLE_STREAM_PAD, 1)
    s_re, s_im = _state_from_super(s_state, db)
    yss = yss[:, :db]
    att_s = _attn_sample(qkvs.reshape(db, dseq, 3 * D_ATTN), cache_attn_k[l].reshape(db, w_cache, D_ATTN),
                         cache_attn_v[l].reshape(db, w_cache, D_ATTN), bias_s)
    hist = cache_conv[l].astype(F32)
    pad_rows = ((0, 0), (0, dseq - 1), (0, 0))
    h1 = jnp.pad(hist[:, 1:2], pad_rows).reshape(rows, D_FF)
    h2 = jnp.pad(hist, ((0, 0), (0, dseq - 2), (0, 0))).reshape(rows, D_FF)
    y_s, a_s = _ffn_sample(xs, yss, att_s.reshape(1, rows, D_ATTN), gates_s, h1, h2, wts, dseq)
    s_conv = a_s.reshape(db, dseq, D_FF)[:, dseq - (CONV_W - 1):]

    heads = lambda a, n, t: a.reshape(1, n, t, N_HEADS, HEAD_DIM)
    return (y_prompt, y_s.reshape(db, dseq, D_MODEL),
            p_re[None], p_im[None],
            heads(k32, nb, kept), heads(v32, nb, kept), p_conv[None],
            s_re[None], s_im[None],
            heads(ks32, db, dseq), heads(vs32, db, dseq), s_conv[None])
```
